```python
import math
import jax
import jax.numpy as jnp
from jax import lax
import numpy as np

D_MODEL = 1024
BATCH = 4
SEQ = 4096
DEPTH = 4
DEC_BATCH = 16
DEC_SEQ = 4096
PAST_LEN = 128

GRID_W = 64
D_MIX = D_MODEL
N_MIXERS = 4
W_GROUP = D_MIX // N_MIXERS
D_FF = 4 * D_MODEL
NORM_EPS = 1e-6

S5_CH = 16
S5_GROUPS = W_GROUP // S5_CH
S5_STATE = 64
S5_IN = W_GROUP

HY_ORDER = 2
HY_BANDS = 8
HY_EMB = 1 + 2 * HY_BANDS
HY_FFN = 64
HY_IN = (HY_ORDER + 1) * W_GROUP

RW_HEAD = 64
RW_HEADS = W_GROUP // RW_HEAD
RW_DECAY_RANK = 64
RW_A_RANK = 64
RW_G_RANK = 128
RW_LN_EPS = 64e-5
RW_IN = 3 * W_GROUP + RW_DECAY_RANK + RW_A_RANK + RW_G_RANK
RW_SPLITS = (W_GROUP, 2 * W_GROUP, 3 * W_GROUP, 3 * W_GROUP + RW_DECAY_RANK, 3 * W_GROUP + RW_DECAY_RANK + RW_A_RANK)

NA_HEAD = 64
NA_HEADS = W_GROUP // NA_HEAD
NA_WIN_R = 8
NA_WIN_C = 16
NA_COL_BLOCK = 16
NA_COL_SPAN = 32
NA_IN = 3 * W_GROUP
NEG_INF = -1e30

D_IN = S5_IN + HY_IN + RW_IN + NA_IN
IN_SPLITS = (S5_IN, S5_IN + HY_IN, S5_IN + HY_IN + RW_IN)

kernel_name = 'hybrid_bidir_encoder_parallel_heads'


def rms_norm(x, g):
    xf = x.astype(jnp.float32)
    y = xf * lax.rsqrt(jnp.mean(xf * xf, axis=-1, keepdims=True) + NORM_EPS)
    return y * g.astype(jnp.float32)


def shift_prev(x):
    return jnp.pad(x[:, :-1], ((0, 0), (1, 0), (0, 0)))


def shift_next(x):
    return jnp.pad(x[:, 1:], ((0, 0), (0, 1), (0, 0)))


def cmul(ar, ai, br, bi):
    return ar * br - ai * bi, ar * bi + ai * br


def s5_scan(u, lam_re, lam_im, log_dt, b_re, b_im, c_re, c_im, reverse):
    f32 = jnp.float32
    lam_re = lam_re.astype(f32)
    lam_im = lam_im.astype(f32)
    dt = jnp.exp(log_dt.astype(f32))[:, None]
    mag = jnp.exp(lam_re * dt)
    ab_re = mag * jnp.cos(lam_im * dt)
    ab_im = mag * jnp.sin(lam_im * dt)
    den = lam_re * lam_re + lam_im * lam_im
    n_re = ab_re - 1.0
    f_re = (n_re * lam_re + ab_im * lam_im) / den
    f_im = (ab_im * lam_re - n_re * lam_im) / den
    bb_re, bb_im = cmul(f_re[..., None], f_im[..., None], b_re.astype(f32), b_im.astype(f32))
    bu_re = jnp.einsum('blgc,gnc->blgn', u, bb_re)
    bu_im = jnp.einsum('blgc,gnc->blgn', u, bb_im)
    a_re = jnp.broadcast_to(ab_re, bu_re.shape)
    a_im = jnp.broadcast_to(ab_im, bu_im.shape)

    def combine(e1, e2):
        a1r, a1i, b1r, b1i = e1
        a2r, a2i, b2r, b2i = e2
        ar, ai = cmul(a2r, a2i, a1r, a1i)
        br, bi = cmul(a2r, a2i, b1r, b1i)
        return ar, ai, br + b2r, bi + b2i

    _, _, s_re, s_im = lax.associative_scan(combine, (a_re, a_im, bu_re, bu_im), reverse=reverse, axis=1)
    return (jnp.einsum('blgn,gcn->blgc', s_re, c_re.astype(f32))
            - jnp.einsum('blgn,gcn->blgc', s_im, c_im.astype(f32)))


def s5_mixer(z, lam_re, lam_im, log_dt, b_re, b_im, c_re, c_im, d, glu_w, glu_b):
    z = z.astype(jnp.float32)
    Bn, L, _ = z.shape
    u = z.reshape(Bn, L, S5_GROUPS, S5_CH)
    y_f = s5_scan(u, lam_re[0], lam_im[0], log_dt[0], b_re[0], b_im[0], c_re[0], c_im[0], False)
    y_b = s5_scan(u, lam_re[1], lam_im[1], log_dt[1], b_re[1], b_im[1], c_re[1], c_im[1], True)
    y = (y_f + y_b).reshape(Bn, L, W_GROUP) + d * z
    g = jax.nn.gelu(y)
    return g * jax.nn.sigmoid(g @ glu_w + glu_b)


def hyena_filter_spectra(L, w1, b1, freq, w2, b2, w3, log_rate):
    t = jnp.arange(L, dtype=jnp.float32) / L
    ang = 2.0 * math.pi * t[:, None] * jnp.arange(1, HY_BANDS + 1, dtype=jnp.float32)
    feats = jnp.concatenate([t[:, None], jnp.sin(ang), jnp.cos(ang)], axis=-1)
    h = jnp.sin(freq[0] * (feats @ w1 + b1))
    h = jnp.sin(freq[1] * (h @ w2 + b2))
    h = (h @ w3).reshape(L, 2, HY_ORDER, W_GROUP)
    h = h * jnp.exp(-jnp.exp(log_rate.astype(jnp.float32))[None] * t[:, None, None, None])
    fwd, bwd = h[:, 0], h[:, 1]
    k = jnp.concatenate([fwd, jnp.zeros_like(fwd[:1]), bwd[:0:-1]], axis=0)
    k = k / jnp.sum(jnp.abs(k), axis=0, keepdims=True)
    return jnp.fft.rfft(k, axis=0)


def hyena_mixer(z, conv_w, conv_b, f_w1, f_b1, f_freq, f_w2, f_b2, f_w3, log_rate, d):
    z = z.astype(jnp.float32)
    Bn, L, _ = z.shape
    z = conv_w[0] * shift_prev(z) + conv_w[1] * z + conv_w[2] * shift_next(z) + conv_b
    v, x1, x2 = jnp.split(z, HY_ORDER + 1, axis=-1)
    kf = hyena_filter_spectra(L, f_w1, f_b1, f_freq, f_w2, f_b2, f_w3, log_rate)
    u = v
    for o, gate in enumerate((x1, x2)):
        conv = jnp.fft.irfft(jnp.fft.rfft(u, n=2 * L, axis=1) * kf[None, :, o], n=2 * L, axis=1)[:, :L]
        u = gate * (conv + d[o] * u)
    return u


def wkv7_scan(r, w, k, v, a, b, reverse):
    Bn, L, H, N = r.shape
    seq = tuple(jnp.moveaxis(t, 1, 0) for t in (r, w, k, v, a, b))

    def step(S, inp):
        rt, wt, kt, vt, at, bt = inp
        sa = jnp.einsum('bhvk,bhk->bhv', S, at)
        S = S * wt[:, :, None, :] + sa[..., None] * bt[:, :, None, :] + vt[..., None] * kt[:, :, None, :]
        return S, jnp.einsum('bhvk,bhk->bhv', S, rt)

    S0 = jnp.zeros((Bn, H, N, N), jnp.float32)
    _, y = lax.scan(step, S0, seq, reverse=reverse)
    return jnp.moveaxis(y, 0, 1)


def rwkv_mixer(z, mu, w0, w2, a0, a2, g2, k_k, k_a, r_k, ln_w, ln_b):
    z = z.astype(jnp.float32)
    Bn, L, _ = z.shape
    z = z + mu * (0.5 * (shift_prev(z) + shift_next(z)) - z)
    r, k, v, wd, ad, gd = jnp.split(z, RW_SPLITS, axis=-1)

    def heads(t):
        return t.reshape(Bn, L, RW_HEADS, RW_HEAD)

    g = jax.nn.sigmoid(gd) @ g2
    kk = heads(k * k_k)
    kk = kk / jnp.maximum(jnp.sqrt(jnp.sum(kk * kk, axis=-1, keepdims=True)), 1e-12)
    ys = []
    for d in range(2):
        w = -jax.nn.softplus(-(w0[d] + jnp.tanh(wd) @ w2[d])) - 0.5
        decay = jnp.exp(-jnp.exp(w))
        a = jax.nn.sigmoid(a0[d] + ad @ a2[d])
        kd = k * (1.0 + (a - 1.0) * k_a)
        ys.append(wkv7_scan(heads(r), heads(decay), heads(kd), heads(v), -kk, kk * heads(a), d == 1))
    y = ys[0] + ys[1]
    mean = jnp.mean(y, axis=-1, keepdims=True)
    var = jnp.mean(jnp.square(y - mean), axis=-1, keepdims=True)
    y = ((y - mean) * lax.rsqrt(var + RW_LN_EPS)).reshape(Bn, L, W_GROUP) * ln_w + ln_b
    bonus = jnp.sum(heads(r) * heads(k) * r_k, axis=-1, keepdims=True) * heads(v)
    return (y + bonus.reshape(Bn, L, W_GROUP)) * g


def na_mixer(z, q_g, k_g, rel_bias):
    z = z.astype(jnp.float32)
    Bn, L, _ = z.shape
    rows = L // GRID_W
    wr = min(NA_WIN_R, rows)
    q, k, v = [t.reshape(Bn, rows, GRID_W, NA_HEADS, NA_HEAD) for t in jnp.split(z, 3, axis=-1)]
    q = rms_norm(q, q_g) * (NA_HEAD ** -0.5)
    k = rms_norm(k, k_g)
    n_cb = GRID_W // NA_COL_BLOCK
    qcol = np.arange(GRID_W).reshape(n_cb, NA_COL_BLOCK)
    span0 = np.clip(qcol[:, 0] - NA_WIN_C // 2, 0, GRID_W - NA_COL_SPAN)
    kcol = span0[:, None] + np.arange(NA_COL_SPAN)[None, :]
    wstart = np.clip(qcol - NA_WIN_C // 2, 0, GRID_W - NA_WIN_C)
    col_mask = (kcol[:, None, :] >= wstart[:, :, None]) & (kcol[:, None, :] < wstart[:, :, None] + NA_WIN_C)
    dc_idx = np.clip(kcol[:, None, :] - qcol[:, :, None] + NA_WIN_C - 1, 0, 2 * NA_WIN_C - 2)
    k_sp = k[:, :, kcol]
    v_sp = v[:, :, kcol]
    q_rows = jnp.moveaxis(q.reshape(Bn, rows, n_cb, NA_COL_BLOCK, NA_HEADS, NA_HEAD), 1, 0)
    bias_tab = rel_bias.astype(jnp.float32)
    mask = jnp.asarray(col_mask)[None, None, :, :, None, :]

    def row_block(args):
        r, q_r = args
        rs = jnp.clip(r - wr // 2, 0, rows - wr)
        k_r = lax.dynamic_slice_in_dim(k_sp, rs, wr, axis=1)
        v_r = lax.dynamic_slice_in_dim(v_sp, rs, wr, axis=1)
        s = jnp.einsum('bcqhd,bwckhd->bhcqwk', q_r, k_r)
        dr_idx = rs + jnp.arange(wr) - r + (NA_WIN_R - 1)
        bias = jnp.transpose(bias_tab[:, dr_idx][:, :, dc_idx], (0, 2, 3, 1, 4))
        s = jnp.where(mask, s + bias[None], NEG_INF)
        shp = s.shape
        pr = jax.nn.softmax(s.reshape(shp[:4] + (wr * NA_COL_SPAN,)), axis=-1).reshape(shp)
        return jnp.einsum('bhcqwk,bwckhd->bcqhd', pr, v_r)

    out = lax.map(row_block, (jnp.arange(rows), q_rows))
    return jnp.moveaxis(out, 0, 1).reshape(Bn, L, W_GROUP)


def encoder_trunk(x, p):
    for l in range(DEPTH):
        h = rms_norm(x, p['ln1_g'][l]).astype(x.dtype)
        z = h @ p['w_in'][l]
        z_s5, z_hy, z_rw, z_na = jnp.split(z, IN_SPLITS, axis=-1)
        y_s5 = s5_mixer(z_s5, p['s5_lam_re'][l], p['s5_lam_im'][l], p['s5_log_dt'][l], p['s5_b_re'][l],
                        p['s5_b_im'][l], p['s5_c_re'][l], p['s5_c_im'][l], p['s5_d'][l],
                        p['s5_glu_w'][l], p['s5_glu_b'][l])
        y_hy = hyena_mixer(z_hy, p['hy_conv_w'][l], p['hy_conv_b'][l], p['hy_f_w1'][l], p['hy_f_b1'][l],
                           p['hy_f_freq'][l], p['hy_f_w2'][l], p['hy_f_b2'][l], p['hy_f_w3'][l],
                           p['hy_log_rate'][l], p['hy_d'][l])
        y_rw = rwkv_mixer(z_rw, p['rw_mu'][l], p['rw_w0'][l], p['rw_w2'][l], p['rw_a0'][l], p['rw_a2'][l],
                          p['rw_g2'][l], p['rw_k_k'][l], p['rw_k_a'][l], p['rw_r_k'][l],
                          p['rw_ln_w'][l], p['rw_ln_b'][l])
        y_na = na_mixer(z_na, p['na_q_g'][l], p['na_k_g'][l], p['na_rel_bias'][l])
        groups = [rms_norm(y, p['grp_g'][l, i]).astype(x.dtype) for i, y in enumerate((y_s5, y_hy, y_rw, y_na))]
        x = x + jnp.concatenate(groups, axis=-1) @ p['w_out'][l]
        h = rms_norm(x, p['ln2_g'][l]).astype(x.dtype)
        x = x + jnp.square(jax.nn.relu(h @ p['w_mlp1'][l])) @ p['w_mlp2'][l]
    return x


def setup_inputs(seed: int = 0) -> dict:
    key = jax.random.key(seed)
    ks = iter(jax.random.split(key, 64))
    L = DEPTH

    def nrm(shape, scale=1.0):
        return jax.random.normal(next(ks), shape, jnp.float32) * scale

    def unif(shape, lo, hi):
        return jax.random.uniform(next(ks), shape, jnp.float32, lo, hi)

    inp = {}
    inp['x_prompt'] = nrm((BATCH, SEQ, D_MODEL))
    inp['x_sample'] = nrm((DEC_BATCH, DEC_SEQ, D_MODEL))
    inp['ln1_g'] = 1.0 + nrm((L, D_MODEL), 0.02)
    inp['w_in'] = nrm((L, D_MODEL, D_IN), D_MODEL ** -0.5)
    inp['s5_lam_re'] = -0.5 + nrm((L, 2, S5_GROUPS, S5_STATE), 0.01)
    inp['s5_lam_im'] = math.pi * jnp.arange(S5_STATE, dtype=jnp.float32) + nrm((L, 2, S5_GROUPS, S5_STATE), 0.01)
    inp['s5_log_dt'] = unif((L, 2, S5_GROUPS), math.log(1e-3), math.log(1e-1))
    inp['s5_b_re'] = nrm((L, 2, S5_GROUPS, S5_STATE, S5_CH), (2 * S5_CH) ** -0.5)
    inp['s5_b_im'] = nrm((L, 2, S5_GROUPS, S5_STATE, S5_CH), (2 * S5_CH) ** -0.5)
    inp['s5_c_re'] = nrm((L, 2, S5_GROUPS, S5_CH, S5_STATE), S5_STATE ** -0.5)
    inp['s5_c_im'] = nrm((L, 2, S5_GROUPS, S5_CH, S5_STATE), S5_STATE ** -0.5)
    inp['s5_d'] = nrm((L, W_GROUP))
    inp['s5_glu_w'] = nrm((L, W_GROUP, W_GROUP), W_GROUP ** -0.5)
    inp['s5_glu_b'] = nrm((L, W_GROUP), 0.02)
    inp['hy_conv_w'] = nrm((L, 3, HY_IN), 3 ** -0.5)
    inp['hy_conv_b'] = nrm((L, HY_IN), 0.02)
    inp['hy_f_w1'] = nrm((L, HY_EMB, HY_FFN), HY_EMB ** -0.5)
    inp['hy_f_b1'] = nrm((L, HY_FFN), 0.1)
    inp['hy_f_freq'] = 1.0 + nrm((L, 2, HY_FFN), 0.1)
    inp['hy_f_w2'] = nrm((L, HY_FFN, HY_FFN), HY_FFN ** -0.5)
    inp['hy_f_b2'] = nrm((L, HY_FFN), 0.1)
    inp['hy_f_w3'] = nrm((L, HY_FFN, 2 * HY_ORDER * W_GROUP), HY_FFN ** -0.5)
    inp['hy_log_rate'] = unif((L, 2, HY_ORDER, W_GROUP), math.log(3.0), math.log(15.0))
    inp['hy_d'] = nrm((L, HY_ORDER, W_GROUP))
    inp['rw_mu'] = unif((L, RW_IN), 0.2, 0.8)
    inp['rw_w0'] = jnp.linspace(-6.5, -1.0, W_GROUP, dtype=jnp.float32) + nrm((L, 2, W_GROUP), 0.1)
    inp['rw_w2'] = nrm((L, 2, RW_DECAY_RANK, W_GROUP), 0.5 * RW_DECAY_RANK ** -0.5)
    inp['rw_a0'] = nrm((L, 2, W_GROUP), 0.1)
    inp['rw_a2'] = nrm((L, 2, RW_A_RANK, W_GROUP), 0.5 * RW_A_RANK ** -0.5)
    inp['rw_g2'] = nrm((L, RW_G_RANK, W_GROUP), RW_G_RANK ** -0.5)
    inp['rw_k_k'] = 0.85 + nrm((L, W_GROUP), 0.02)
    inp['rw_k_a'] = 1.0 + nrm((L, W_GROUP), 0.02)
    inp['rw_r_k'] = nrm((L, RW_HEADS, RW_HEAD), 0.1)
    inp['rw_ln_w'] = 1.0 + nrm((L, W_GROUP), 0.02)
    inp['rw_ln_b'] = nrm((L, W_GROUP), 0.02)
    inp['na_q_g'] = 1.0 + nrm((L, NA_HEAD), 0.02)
    inp['na_k_g'] = 1.0 + nrm((L, NA_HEAD), 0.02)
    inp['na_rel_bias'] = nrm((L, NA_HEADS, 2 * NA_WIN_R - 1, 2 * NA_WIN_C - 1), 0.1)
    inp['grp_g'] = 1.0 + nrm((L, N_MIXERS, W_GROUP), 0.02)
    inp['w_out'] = nrm((L, D_MIX, D_MODEL), D_MIX ** -0.5)
    inp['ln2_g'] = 1.0 + nrm((L, D_MODEL), 0.02)
    inp['w_mlp1'] = nrm((L, D_MODEL, D_FF), D_MODEL ** -0.5)
    inp['w_mlp2'] = nrm((L, D_FF, D_MODEL), D_FF ** -0.5)
    return inp


def reference(x_prompt, x_sample, ln1_g, w_in, s5_lam_re, s5_lam_im, s5_log_dt, s5_b_re, s5_b_im,
              s5_c_re, s5_c_im, s5_d, s5_glu_w, s5_glu_b, hy_conv_w, hy_conv_b, hy_f_w1, hy_f_b1,
              hy_f_freq, hy_f_w2, hy_f_b2, hy_f_w3, hy_log_rate, hy_d, rw_mu, rw_w0, rw_w2, rw_a0,
              rw_a2, rw_g2, rw_k_k, rw_k_a, rw_r_k, rw_ln_w, rw_ln_b, na_q_g, na_k_g, na_rel_bias,
              grp_g, w_out, ln2_g, w_mlp1, w_mlp2):
    p = dict(ln1_g=ln1_g, w_in=w_in, s5_lam_re=s5_lam_re, s5_lam_im=s5_lam_im, s5_log_dt=s5_log_dt,
             s5_b_re=s5_b_re, s5_b_im=s5_b_im, s5_c_re=s5_c_re, s5_c_im=s5_c_im, s5_d=s5_d,
             s5_glu_w=s5_glu_w, s5_glu_b=s5_glu_b, hy_conv_w=hy_conv_w, hy_conv_b=hy_conv_b,
             hy_f_w1=hy_f_w1, hy_f_b1=hy_f_b1, hy_f_freq=hy_f_freq, hy_f_w2=hy_f_w2, hy_f_b2=hy_f_b2,
             hy_f_w3=hy_f_w3, hy_log_rate=hy_log_rate, hy_d=hy_d, rw_mu=rw_mu, rw_w0=rw_w0, rw_w2=rw_w2,
             rw_a0=rw_a0, rw_a2=rw_a2, rw_g2=rw_g2, rw_k_k=rw_k_k, rw_k_a=rw_k_a, rw_r_k=rw_r_k,
             rw_ln_w=rw_ln_w, rw_ln_b=rw_ln_b, na_q_g=na_q_g, na_k_g=na_k_g, na_rel_bias=na_rel_bias,
             grp_g=grp_g, w_out=w_out, ln2_g=ln2_g, w_mlp1=w_mlp1, w_mlp2=w_mlp2)
    y_prompt = encoder_trunk(x_prompt, p)
    y_sample = encoder_trunk(x_sample, p)
    return (y_prompt, y_sample)
```

```python
import functools
import math

import jax
import jax.numpy as jnp
import numpy as np
from jax import lax
from jax.experimental import pallas as pl
from jax.experimental.pallas import tpu as pltpu

D_MODEL = 1024
DEPTH = 4
GRID_W = 64
W_GROUP = 256
N_MIXERS = 4
D_FF = 4 * D_MODEL
NORM_EPS = 1e-6

S5_CH = 16
S5_GROUPS = W_GROUP // S5_CH
S5_STATE = 64
S5_IN = W_GROUP

HY_ORDER = 2
HY_BANDS = 8
HY_IN = (HY_ORDER + 1) * W_GROUP

RW_HEAD = 64
RW_HEADS = W_GROUP // RW_HEAD
RW_DECAY_RANK = 64
RW_A_RANK = 64
RW_G_RANK = 128
RW_LN_EPS = 64e-5
RW_IN = 3 * W_GROUP + RW_DECAY_RANK + RW_A_RANK + RW_G_RANK
RW_SPLITS = (W_GROUP, 2 * W_GROUP, 3 * W_GROUP, 3 * W_GROUP + RW_DECAY_RANK,
             3 * W_GROUP + RW_DECAY_RANK + RW_A_RANK)

NA_HEAD = 64
NA_HEADS = W_GROUP // NA_HEAD
NA_WIN_R = 8
NA_WIN_C = 16
NA_COL_BLOCK = 16
NA_COL_SPAN = 32
NA_IN = 3 * W_GROUP
NEG_INF = -1e30

D_IN = S5_IN + HY_IN + RW_IN + NA_IN
IN_SPLITS = (S5_IN, S5_IN + HY_IN, S5_IN + HY_IN + RW_IN)

V7X_LANES = 128
VMEM_LIMIT = 48 * 1024 * 1024

ROW_TILE = 512
FF_TILE = 1024
SCAN_T = 16


def _rms(x):
    return x * lax.rsqrt(jnp.mean(x * x, axis=-1, keepdims=True) + NORM_EPS)


def _in_proj_kernel(x_ref, g_ref, w_ref, zs_ref, zh_ref, zr_ref, zn_ref):
    h = (_rms(x_ref[...]) * g_ref[...]).astype(jnp.bfloat16)
    bounds = (0,) + IN_SPLITS + (D_IN,)
    for o_ref, lo, hi in zip((zs_ref, zh_ref, zr_ref, zn_ref), bounds[:-1], bounds[1:]):
        o_ref[...] = jnp.dot(h, w_ref[:, lo:hi], preferred_element_type=jnp.float32)


def in_proj(x, g, w_bf16):
    n = x.shape[0]
    widths = (S5_IN, HY_IN, RW_IN, NA_IN)
    return pl.pallas_call(
        _in_proj_kernel,
        grid=(n // ROW_TILE,),
        in_specs=[pl.BlockSpec((ROW_TILE, D_MODEL), lambda i: (i, 0)),
                  pl.BlockSpec((1, D_MODEL), lambda i: (0, 0)),
                  pl.BlockSpec((D_MODEL, D_IN), lambda i: (0, 0))],
        out_specs=[pl.BlockSpec((ROW_TILE, w), lambda i: (i, 0)) for w in widths],
        out_shape=[jax.ShapeDtypeStruct((n, w), jnp.float32) for w in widths],
        compiler_params=pltpu.CompilerParams(dimension_semantics=("parallel",),
                                             vmem_limit_bytes=VMEM_LIMIT),
        name="in_proj",
    )(x, g.reshape(1, D_MODEL), w_bf16)


def _out_proj_kernel(x_ref, y0_ref, y1_ref, y2_ref, y3_ref, g_ref, w_ref, o_ref):
    acc = x_ref[...]
    for i, y_ref in enumerate((y0_ref, y1_ref, y2_ref, y3_ref)):
        n = (_rms(y_ref[...]) * g_ref[i:i + 1, :]).astype(jnp.bfloat16)
        acc = acc + jnp.dot(n, w_ref[i * W_GROUP:(i + 1) * W_GROUP, :],
                            preferred_element_type=jnp.float32)
    o_ref[...] = acc


def out_proj(x, ys, g, w_bf16):
    n = x.shape[0]
    row = lambda w: pl.BlockSpec((ROW_TILE, w), lambda i: (i, 0))
    return pl.pallas_call(
        _out_proj_kernel,
        grid=(n // ROW_TILE,),
        in_specs=[row(D_MODEL)] + [row(W_GROUP)] * N_MIXERS
                 + [pl.BlockSpec((N_MIXERS, W_GROUP), lambda i: (0, 0)),
                    pl.BlockSpec((D_MODEL, D_MODEL), lambda i: (0, 0))],
        out_specs=row(D_MODEL),
        out_shape=jax.ShapeDtypeStruct((n, D_MODEL), jnp.float32),
        compiler_params=pltpu.CompilerParams(dimension_semantics=("parallel",),
                                             vmem_limit_bytes=VMEM_LIMIT),
        name="out_proj",
    )(x, *ys, g, w_bf16)


def _mlp_kernel(x_ref, g_ref, w1_ref, w2_ref, o_ref, h_ref):
    j = pl.program_id(1)

    @pl.when(j == 0)
    def _():
        x = x_ref[...]
        h_ref[...] = (_rms(x) * g_ref[...]).astype(jnp.bfloat16)
        o_ref[...] = x

    a = jnp.dot(h_ref[...], w1_ref[...], preferred_element_type=jnp.float32)
    a = jnp.square(jnp.maximum(a, 0.0)).astype(jnp.bfloat16)
    o_ref[...] += jnp.dot(a, w2_ref[...], preferred_element_type=jnp.float32)


def mlp(x, g, w1_bf16, w2_bf16):
    n = x.shape[0]
    return pl.pallas_call(
        _mlp_kernel,
        grid=(n // ROW_TILE, D_FF // FF_TILE),
        in_specs=[pl.BlockSpec((ROW_TILE, D_MODEL), lambda i, j: (i, 0)),
                  pl.BlockSpec((1, D_MODEL), lambda i, j: (0, 0)),
                  pl.BlockSpec((D_MODEL, FF_TILE), lambda i, j: (0, j)),
                  pl.BlockSpec((FF_TILE, D_MODEL), lambda i, j: (j, 0))],
        out_specs=pl.BlockSpec((ROW_TILE, D_MODEL), lambda i, j: (i, 0)),
        out_shape=jax.ShapeDtypeStruct((n, D_MODEL), jnp.float32),
        scratch_shapes=[pltpu.VMEM((ROW_TILE, D_MODEL), jnp.bfloat16)],
        compiler_params=pltpu.CompilerParams(dimension_semantics=("parallel", "arbitrary"),
                                             vmem_limit_bytes=VMEM_LIMIT),
        name="mlp",
    )(x, g.reshape(1, D_MODEL), w1_bf16, w2_bf16)


def _rwkv_scan_kernel(a_ref, w_ref, b_ref, k_ref, r_ref, v_ref, y_ref, s_ref):
    @pl.when(pl.program_id(0) == 0)
    def _():
        s_ref[...] = jnp.zeros_like(s_ref)

    n_lane_tiles = s_ref.shape[-1] // V7X_LANES

    def step(t, carry):
        for j in range(n_lane_tiles):
            ln = slice(j * V7X_LANES, (j + 1) * V7X_LANES)
            vt = v_ref[t, :, ln]
            sa = s_ref[0, :, ln] * a_ref[t, 0:1, ln]
            for k in range(1, RW_HEAD):
                sa = sa + s_ref[k, :, ln] * a_ref[t, k:k + 1, ln]
            y = None
            for k in range(RW_HEAD):
                s = (s_ref[k, :, ln] * w_ref[t, k:k + 1, ln] + sa * b_ref[t, k:k + 1, ln]
                     + vt * k_ref[t, k:k + 1, ln])
                s_ref[k, :, ln] = s
                yk = s * r_ref[t, k:k + 1, ln]
                y = yk if y is None else y + yk
            y_ref[t, :, ln] = y
        return carry

    lax.fori_loop(0, SCAN_T, step, 0)


def rwkv_scan(a, w, b, k, r, v):
    L, _, nc = a.shape
    blk = pl.BlockSpec((SCAN_T, RW_HEAD, nc), lambda i: (i, 0, 0))
    return pl.pallas_call(
        _rwkv_scan_kernel,
        grid=(L // SCAN_T,),
        in_specs=[blk] * 6,
        out_specs=blk,
        out_shape=jax.ShapeDtypeStruct((L, RW_HEAD, nc), jnp.float32),
        scratch_shapes=[pltpu.VMEM((RW_HEAD, RW_HEAD, nc), jnp.float32)],
        compiler_params=pltpu.CompilerParams(dimension_semantics=("arbitrary",),
                                             vmem_limit_bytes=VMEM_LIMIT),
        name="rwkv_scan",
    )(a, w, b, k, r, v)


def _shift_prev(x):
    return jnp.pad(x[:, :-1], ((0, 0), (1, 0), (0, 0)))


def _shift_next(x):
    return jnp.pad(x[:, 1:], ((0, 0), (0, 1), (0, 0)))


def _to_chains(fwd, bwd, nc):
    Bn, L, _ = fwd.shape

    def one(x):
        return x.reshape(Bn, L, RW_HEADS, RW_HEAD).transpose(1, 3, 0, 2).reshape(L, RW_HEAD, Bn * RW_HEADS)

    out = jnp.concatenate([one(fwd), one(bwd[:, ::-1])], axis=-1)
    return jnp.pad(out, ((0, 0), (0, 0), (0, nc - out.shape[-1])))


def rwkv_mixer(z, mu, w0, w2, a0, a2, g2, k_k, k_a, r_k, ln_w, ln_b):
    Bn, L, _ = z.shape
    z = z + mu * (0.5 * (_shift_prev(z) + _shift_next(z)) - z)
    r, k, v, wd, ad, gd = jnp.split(z, RW_SPLITS, axis=-1)

    def heads(t):
        return t.reshape(Bn, L, RW_HEADS, RW_HEAD)

    g = jax.nn.sigmoid(gd) @ g2
    kk = heads(k * k_k)
    kk = kk / jnp.maximum(jnp.sqrt(jnp.sum(kk * kk, axis=-1, keepdims=True)), 1e-12)
    kk = kk.reshape(Bn, L, W_GROUP)
    decay, kd, bb = [], [], []
    for d in range(2):
        w = -jax.nn.softplus(-(w0[d] + jnp.tanh(wd) @ w2[d])) - 0.5
        decay.append(jnp.exp(-jnp.exp(w)))
        a = jax.nn.sigmoid(a0[d] + ad @ a2[d])
        kd.append(k * (1.0 + (a - 1.0) * k_a))
        bb.append(kk * a)
    n_chain = 2 * Bn * RW_HEADS
    nc = -(-n_chain // V7X_LANES) * V7X_LANES
    y = rwkv_scan(_to_chains(-kk, -kk, nc), _to_chains(decay[0], decay[1], nc),
                  _to_chains(bb[0], bb[1], nc), _to_chains(kd[0], kd[1], nc),
                  _to_chains(r, r, nc), _to_chains(v, v, nc))
    half = Bn * RW_HEADS

    def back(t):
        return t.reshape(L, RW_HEAD, Bn, RW_HEADS).transpose(2, 0, 3, 1)

    y = back(y[..., :half]) + back(y[..., half:2 * half])[:, ::-1]
    mean = jnp.mean(y, axis=-1, keepdims=True)
    var = jnp.mean(jnp.square(y - mean), axis=-1, keepdims=True)
    y = ((y - mean) * lax.rsqrt(var + RW_LN_EPS)).reshape(Bn, L, W_GROUP) * ln_w + ln_b
    bonus = jnp.sum(heads(r) * heads(k) * r_k, axis=-1, keepdims=True) * heads(v)
    return (y + bonus.reshape(Bn, L, W_GROUP)) * g


def _cmul(ar, ai, br, bi):
    return ar * br - ai * bi, ar * bi + ai * br


def _s5_scan(u, lam_re, lam_im, log_dt, b_re, b_im, c_re, c_im, reverse):
    dt = jnp.exp(log_dt)[:, None]
    mag = jnp.exp(lam_re * dt)
    ab_re = mag * jnp.cos(lam_im * dt)
    ab_im = mag * jnp.sin(lam_im * dt)
    den = lam_re * lam_re + lam_im * lam_im
    n_re = ab_re - 1.0
    f_re = (n_re * lam_re + ab_im * lam_im) / den
    f_im = (ab_im * lam_re - n_re * lam_im) / den
    bb_re, bb_im = _cmul(f_re[..., None], f_im[..., None], b_re, b_im)
    bu_re = jnp.einsum('blgc,gnc->blgn', u, bb_re)
    bu_im = jnp.einsum('blgc,gnc->blgn', u, bb_im)
    a_re = jnp.broadcast_to(ab_re, bu_re.shape)
    a_im = jnp.broadcast_to(ab_im, bu_im.shape)

    def combine(e1, e2):
        a1r, a1i, b1r, b1i = e1
        a2r, a2i, b2r, b2i = e2
        ar, ai = _cmul(a2r, a2i, a1r, a1i)
        br, bi = _cmul(a2r, a2i, b1r, b1i)
        return ar, ai, br + b2r, bi + b2i

    _, _, s_re, s_im = lax.associative_scan(combine, (a_re, a_im, bu_re, bu_im), reverse=reverse, axis=1)
    return (jnp.einsum('blgn,gcn->blgc', s_re, c_re) - jnp.einsum('blgn,gcn->blgc', s_im, c_im))


def s5_mixer(z, lam_re, lam_im, log_dt, b_re, b_im, c_re, c_im, d, glu_w, glu_b):
    Bn, L, _ = z.shape
    u = z.reshape(Bn, L, S5_GROUPS, S5_CH)
    y_f = _s5_scan(u, lam_re[0], lam_im[0], log_dt[0], b_re[0], b_im[0], c_re[0], c_im[0], False)
    y_b = _s5_scan(u, lam_re[1], lam_im[1], log_dt[1], b_re[1], b_im[1], c_re[1], c_im[1], True)
    y = (y_f + y_b).reshape(Bn, L, W_GROUP) + d * z
    g = jax.nn.gelu(y)
    return g * jax.nn.sigmoid(g @ glu_w + glu_b)


def _hyena_filter_spectra(L, w1, b1, freq, w2, b2, w3, log_rate):
    t = jnp.arange(L, dtype=jnp.float32) / L
    ang = 2.0 * math.pi * t[:, None] * jnp.arange(1, HY_BANDS + 1, dtype=jnp.float32)
    feats = jnp.concatenate([t[:, None], jnp.sin(ang), jnp.cos(ang)], axis=-1)
    h = jnp.sin(freq[0] * (feats @ w1 + b1))
    h = jnp.sin(freq[1] * (h @ w2 + b2))
    h = (h @ w3).reshape(L, 2, HY_ORDER, W_GROUP)
    h = h * jnp.exp(-jnp.exp(log_rate)[None] * t[:, None, None, None])
    fwd, bwd = h[:, 0], h[:, 1]
    k = jnp.concatenate([fwd, jnp.zeros_like(fwd[:1]), bwd[:0:-1]], axis=0)
    k = k / jnp.sum(jnp.abs(k), axis=0, keepdims=True)
    return jnp.fft.rfft(k, axis=0)


def hyena_mixer(z, conv_w, conv_b, f_w1, f_b1, f_freq, f_w2, f_b2, f_w3, log_rate, d):
    Bn, L, _ = z.shape
    z = conv_w[0] * _shift_prev(z) + conv_w[1] * z + conv_w[2] * _shift_next(z) + conv_b
    v, x1, x2 = jnp.split(z, HY_ORDER + 1, axis=-1)
    kf = _hyena_filter_spectra(L, f_w1, f_b1, f_freq, f_w2, f_b2, f_w3, log_rate)
    u = v
    for o, gate in enumerate((x1, x2)):
        conv = jnp.fft.irfft(jnp.fft.rfft(u, n=2 * L, axis=1) * kf[None, :, o], n=2 * L, axis=1)[:, :L]
        u = gate * (conv + d[o] * u)
    return u


def na_mixer(z, q_g, k_g, rel_bias):
    Bn, L, _ = z.shape
    rows = L // GRID_W
    wr = min(NA_WIN_R, rows)
    q, k, v = [t.reshape(Bn, rows, GRID_W, NA_HEADS, NA_HEAD) for t in jnp.split(z, 3, axis=-1)]
    q = _rms(q) * q_g * (NA_HEAD ** -0.5)
    k = _rms(k) * k_g
    n_cb = GRID_W // NA_COL_BLOCK
    qcol = np.arange(GRID_W).reshape(n_cb, NA_COL_BLOCK)
    span0 = np.clip(qcol[:, 0] - NA_WIN_C // 2, 0, GRID_W - NA_COL_SPAN)
    kcol = span0[:, None] + np.arange(NA_COL_SPAN)[None, :]
    wstart = np.clip(qcol - NA_WIN_C // 2, 0, GRID_W - NA_WIN_C)
    col_mask = (kcol[:, None, :] >= wstart[:, :, None]) & (kcol[:, None, :] < wstart[:, :, None] + NA_WIN_C)
    dc_idx = np.clip(kcol[:, None, :] - qcol[:, :, None] + NA_WIN_C - 1, 0, 2 * NA_WIN_C - 2)
    k_sp = k[:, :, kcol]
    v_sp = v[:, :, kcol]
    q_rows = jnp.moveaxis(q.reshape(Bn, rows, n_cb, NA_COL_BLOCK, NA_HEADS, NA_HEAD), 1, 0)
    mask = jnp.asarray(col_mask)[None, None, :, :, None, :]

    def row_block(args):
        r, q_r = args
        rs = jnp.clip(r - wr // 2, 0, rows - wr)
        k_r = lax.dynamic_slice_in_dim(k_sp, rs, wr, axis=1)
        v_r = lax.dynamic_slice_in_dim(v_sp, rs, wr, axis=1)
        s = jnp.einsum('bcqhd,bwckhd->bhcqwk', q_r, k_r)
        dr_idx = rs + jnp.arange(wr) - r + (NA_WIN_R - 1)
        bias = jnp.transpose(rel_bias[:, dr_idx][:, :, dc_idx], (0, 2, 3, 1, 4))
        s = jnp.where(mask, s + bias[None], NEG_INF)
        shp = s.shape
        pr = jax.nn.softmax(s.reshape(shp[:4] + (wr * NA_COL_SPAN,)), axis=-1).reshape(shp)
        return jnp.einsum('bhcqwk,bwckhd->bcqhd', pr, v_r)

    out = lax.map(row_block, (jnp.arange(rows), q_rows))
    return jnp.moveaxis(out, 0, 1).reshape(Bn, L, W_GROUP)


def _trunk(x, p):
    Bn, L, _ = x.shape
    n = Bn * L
    x = x.reshape(n, D_MODEL)
    bf = jnp.bfloat16
    for l in range(DEPTH):
        z_s5, z_hy, z_rw, z_na = in_proj(x, p['ln1_g'][l], p['w_in'][l].astype(bf))
        y_s5 = s5_mixer(z_s5.reshape(Bn, L, -1), p['s5_lam_re'][l], p['s5_lam_im'][l], p['s5_log_dt'][l],
                        p['s5_b_re'][l], p['s5_b_im'][l], p['s5_c_re'][l], p['s5_c_im'][l], p['s5_d'][l],
                        p['s5_glu_w'][l], p['s5_glu_b'][l])
        y_hy = hyena_mixer(z_hy.reshape(Bn, L, -1), p['hy_conv_w'][l], p['hy_conv_b'][l], p['hy_f_w1'][l],
                           p['hy_f_b1'][l], p['hy_f_freq'][l], p['hy_f_w2'][l], p['hy_f_b2'][l],
                           p['hy_f_w3'][l], p['hy_log_rate'][l], p['hy_d'][l])
        y_rw = rwkv_mixer(z_rw.reshape(Bn, L, -1), p['rw_mu'][l], p['rw_w0'][l], p['rw_w2'][l],
                          p['rw_a0'][l], p['rw_a2'][l], p['rw_g2'][l], p['rw_k_k'][l], p['rw_k_a'][l],
                          p['rw_r_k'][l], p['rw_ln_w'][l], p['rw_ln_b'][l])
        y_na = na_mixer(z_na.reshape(Bn, L, -1), p['na_q_g'][l], p['na_k_g'][l], p['na_rel_bias'][l])
        ys = [y.reshape(n, W_GROUP) for y in (y_s5, y_hy, y_rw, y_na)]
        x = out_proj(x, ys, p['grp_g'][l], p['w_out'][l].astype(bf))
        x = mlp(x, p['ln2_g'][l], p['w_mlp1'][l].astype(bf), p['w_mlp2'][l].astype(bf))
    return x.reshape(Bn, L, D_MODEL)


def kernel(x_prompt, x_sample, ln1_g, w_in, s5_lam_re, s5_lam_im, s5_log_dt, s5_b_re, s5_b_im,
           s5_c_re, s5_c_im, s5_d, s5_glu_w, s5_glu_b, hy_conv_w, hy_conv_b, hy_f_w1, hy_f_b1,
           hy_f_freq, hy_f_w2, hy_f_b2, hy_f_w3, hy_log_rate, hy_d, rw_mu, rw_w0, rw_w2, rw_a0,
           rw_a2, rw_g2, rw_k_k, rw_k_a, rw_r_k, rw_ln_w, rw_ln_b, na_q_g, na_k_g, na_rel_bias,
           grp_g, w_out, ln2_g, w_mlp1, w_mlp2):
    p = dict(ln1_g=ln1_g, w_in=w_in, s5_lam_re=s5_lam_re, s5_lam_im=s5_lam_im, s5_log_dt=s5_log_dt,
             s5_b_re=s5_b_re, s5_b_im=s5_b_im, s5_c_re=s5_c_re, s5_c_im=s5_c_im, s5_d=s5_d,
             s5_glu_w=s5_glu_w, s5_glu_b=s5_glu_b, hy_conv_w=hy_conv_w, hy_conv_b=hy_conv_b,
             hy_f_w1=hy_f_w1, hy_f_b1=hy_f_b1, hy_f_freq=hy_f_freq, hy_f_w2=hy_f_w2, hy_f_b2=hy_f_b2,
             hy_f_w3=hy_f_w3, hy_log_rate=hy_log_rate, hy_d=hy_d, rw_mu=rw_mu, rw_w0=rw_w0, rw_w2=rw_w2,
             rw_a0=rw_a0, rw_a2=rw_a2, rw_g2=rw_g2, rw_k_k=rw_k_k, rw_k_a=rw_k_a, rw_r_k=rw_r_k,
             rw_ln_w=rw_ln_w, rw_ln_b=rw_ln_b, na_q_g=na_q_g, na_k_g=na_k_g, na_rel_bias=na_rel_bias,
             grp_g=grp_g, w_out=w_out, ln2_g=ln2_g, w_mlp1=w_mlp1, w_mlp2=w_mlp2)
    nb = x_prompt.shape[0]
    y = _trunk(jnp.concatenate([x_prompt, x_sample], axis=0), p)
    return (y[:nb], y[nb:])
```

```python
import math

import jax
import jax.numpy as jnp
import numpy as np
from jax import lax
from jax.experimental import pallas as pl
from jax.experimental.pallas import tpu as pltpu

D_MODEL = 1024
DEPTH = 4
GRID_W = 64
W_GROUP = 256
N_MIXERS = 4
D_FF = 4 * D_MODEL
NORM_EPS = 1e-6

S5_CH = 16
S5_GROUPS = W_GROUP // S5_CH
S5_STATE = 64
S5_IN = W_GROUP

HY_ORDER = 2
HY_BANDS = 8
HY_IN = (HY_ORDER + 1) * W_GROUP

RW_HEAD = 64
RW_HEADS = W_GROUP // RW_HEAD
RW_DECAY_RANK = 64
RW_A_RANK = 64
RW_G_RANK = 128
RW_LN_EPS = 64e-5
RW_IN = 3 * W_GROUP + RW_DECAY_RANK + RW_A_RANK + RW_G_RANK
RW_SPLITS = (W_GROUP, 2 * W_GROUP, 3 * W_GROUP, 3 * W_GROUP + RW_DECAY_RANK,
             3 * W_GROUP + RW_DECAY_RANK + RW_A_RANK)

NA_HEAD = 64
NA_HEADS = W_GROUP // NA_HEAD
NA_WIN_R = 8
NA_WIN_C = 16
NEG_INF = -1e30

D_IN = S5_IN + HY_IN + RW_IN + 3 * W_GROUP

V7X_LANES = 128
VMEM_LIMIT = 48 * 1024 * 1024

ROW_TILE = 512
FF_TILE = 1024
SCAN_T = 16


def _rms(x):
    return x * lax.rsqrt(jnp.mean(x * x, axis=-1, keepdims=True) + NORM_EPS)


IN_WIDTHS = (S5_IN, HY_IN, RW_IN, W_GROUP, W_GROUP, W_GROUP)


def _in_proj_kernel(x_ref, g_ref, w_ref, *o_refs):
    h = (_rms(x_ref[...]) * g_ref[...]).astype(jnp.bfloat16)
    lo = 0
    for o_ref, width in zip(o_refs, IN_WIDTHS):
        o_ref[...] = jnp.dot(h, w_ref[:, lo:lo + width], preferred_element_type=jnp.float32)
        lo += width


def in_proj(x, g, w_bf16):
    n = x.shape[0]
    return pl.pallas_call(
        _in_proj_kernel,
        grid=(n // ROW_TILE,),
        in_specs=[pl.BlockSpec((ROW_TILE, D_MODEL), lambda i: (i, 0)),
                  pl.BlockSpec((1, D_MODEL), lambda i: (0, 0)),
                  pl.BlockSpec((D_MODEL, D_IN), lambda i: (0, 0))],
        out_specs=[pl.BlockSpec((ROW_TILE, w), lambda i: (i, 0)) for w in IN_WIDTHS],
        out_shape=[jax.ShapeDtypeStruct((n, w), jnp.float32) for w in IN_WIDTHS],
        compiler_params=pltpu.CompilerParams(dimension_semantics=("parallel",),
                                             vmem_limit_bytes=VMEM_LIMIT),
        name="in_proj",
    )(x, g.reshape(1, D_MODEL), w_bf16)


def _out_proj_kernel(x_ref, y0_ref, y1_ref, y2_ref, y3_ref, g_ref, w_ref, o_ref):
    acc = x_ref[...]
    for i, y_ref in enumerate((y0_ref, y1_ref, y2_ref, y3_ref)):
        n = (_rms(y_ref[...]) * g_ref[i:i + 1, :]).astype(jnp.bfloat16)
        acc = acc + jnp.dot(n, w_ref[i * W_GROUP:(i + 1) * W_GROUP, :],
                            preferred_element_type=jnp.float32)
    o_ref[...] = acc


def out_proj(x, ys, g, w_bf16):
    n = x.shape[0]
    row = lambda w: pl.BlockSpec((ROW_TILE, w), lambda i: (i, 0))
    return pl.pallas_call(
        _out_proj_kernel,
        grid=(n // ROW_TILE,),
        in_specs=[row(D_MODEL)] + [row(W_GROUP)] * N_MIXERS
                 + [pl.BlockSpec((N_MIXERS, W_GROUP), lambda i: (0, 0)),
                    pl.BlockSpec((D_MODEL, D_MODEL), lambda i: (0, 0))],
        out_specs=row(D_MODEL),
        out_shape=jax.ShapeDtypeStruct((n, D_MODEL), jnp.float32),
        compiler_params=pltpu.CompilerParams(dimension_semantics=("parallel",),
                                             vmem_limit_bytes=VMEM_LIMIT),
        name="out_proj",
    )(x, *ys, g, w_bf16)


def _mlp_kernel(x_ref, g_ref, w1_ref, w2_ref, o_ref, h_ref):
    j = pl.program_id(1)

    @pl.when(j == 0)
    def _():
        x = x_ref[...]
        h_ref[...] = (_rms(x) * g_ref[...]).astype(jnp.bfloat16)
        o_ref[...] = x

    a = jnp.dot(h_ref[...], w1_ref[...], preferred_element_type=jnp.float32)
    a = jnp.square(jnp.maximum(a, 0.0)).astype(jnp.bfloat16)
    o_ref[...] += jnp.dot(a, w2_ref[...], preferred_element_type=jnp.float32)


def mlp(x, g, w1_bf16, w2_bf16):
    n = x.shape[0]
    return pl.pallas_call(
        _mlp_kernel,
        grid=(n // ROW_TILE, D_FF // FF_TILE),
        in_specs=[pl.BlockSpec((ROW_TILE, D_MODEL), lambda i, j: (i, 0)),
                  pl.BlockSpec((1, D_MODEL), lambda i, j: (0, 0)),
                  pl.BlockSpec((D_MODEL, FF_TILE), lambda i, j: (0, j)),
                  pl.BlockSpec((FF_TILE, D_MODEL), lambda i, j: (j, 0))],
        out_specs=pl.BlockSpec((ROW_TILE, D_MODEL), lambda i, j: (i, 0)),
        out_shape=jax.ShapeDtypeStruct((n, D_MODEL), jnp.float32),
        scratch_shapes=[pltpu.VMEM((ROW_TILE, D_MODEL), jnp.bfloat16)],
        compiler_params=pltpu.CompilerParams(dimension_semantics=("parallel", "arbitrary"),
                                             vmem_limit_bytes=VMEM_LIMIT),
        name="mlp",
    )(x, g.reshape(1, D_MODEL), w1_bf16, w2_bf16)


def _rwkv_scan_kernel(a_ref, w_ref, b_ref, k_ref, r_ref, v_ref, y_ref, s_ref):
    @pl.when(pl.program_id(0) == 0)
    def _():
        s_ref[...] = jnp.zeros_like(s_ref)

    n_lane_tiles = s_ref.shape[-1] // V7X_LANES

    def step(t, carry):
        for j in range(n_lane_tiles):
            ln = slice(j * V7X_LANES, (j + 1) * V7X_LANES)
            vt = v_ref[t, :, ln]
            sa = s_ref[0, :, ln] * a_ref[t, 0:1, ln]
            for k in range(1, RW_HEAD):
                sa = sa + s_ref[k, :, ln] * a_ref[t, k:k + 1, ln]
            y = None
            for k in range(RW_HEAD):
                s = (s_ref[k, :, ln] * w_ref[t, k:k + 1, ln] + sa * b_ref[t, k:k + 1, ln]
                     + vt * k_ref[t, k:k + 1, ln])
                s_ref[k, :, ln] = s
                yk = s * r_ref[t, k:k + 1, ln]
                y = yk if y is None else y + yk
            y_ref[t, :, ln] = y
        return carry

    lax.fori_loop(0, SCAN_T, step, 0)


def rwkv_scan(a, w, b, k, r, v):
    L, _, nc = a.shape
    blk = pl.BlockSpec((SCAN_T, RW_HEAD, nc), lambda i: (i, 0, 0))
    return pl.pallas_call(
        _rwkv_scan_kernel,
        grid=(L // SCAN_T,),
        in_specs=[blk] * 6,
        out_specs=blk,
        out_shape=jax.ShapeDtypeStruct((L, RW_HEAD, nc), jnp.float32),
        scratch_shapes=[pltpu.VMEM((RW_HEAD, RW_HEAD, nc), jnp.float32)],
        compiler_params=pltpu.CompilerParams(dimension_semantics=("arbitrary",),
                                             vmem_limit_bytes=VMEM_LIMIT),
        name="rwkv_scan",
    )(a, w, b, k, r, v)


def _shift_prev(x):
    return jnp.pad(x[:, :-1], ((0, 0), (1, 0), (0, 0)))


def _shift_next(x):
    return jnp.pad(x[:, 1:], ((0, 0), (0, 1), (0, 0)))


def _to_chains(fwd, bwd, nc):
    Bn, L, _ = fwd.shape

    def one(x):
        return x.reshape(Bn, L, RW_HEADS, RW_HEAD).transpose(1, 3, 0, 2).reshape(L, RW_HEAD, Bn * RW_HEADS)

    out = jnp.concatenate([one(fwd), one(bwd[:, ::-1])], axis=-1)
    return jnp.pad(out, ((0, 0), (0, 0), (0, nc - out.shape[-1])))


def rwkv_mixer(z, mu, w0, w2, a0, a2, g2, k_k, k_a, r_k, ln_w, ln_b):
    Bn, L, _ = z.shape
    z = z + mu * (0.5 * (_shift_prev(z) + _shift_next(z)) - z)
    r, k, v, wd, ad, gd = jnp.split(z, RW_SPLITS, axis=-1)

    def heads(t):
        return t.reshape(Bn, L, RW_HEADS, RW_HEAD)

    g = jax.nn.sigmoid(gd) @ g2
    kk = heads(k * k_k)
    kk = kk / jnp.maximum(jnp.sqrt(jnp.sum(kk * kk, axis=-1, keepdims=True)), 1e-12)
    kk = kk.reshape(Bn, L, W_GROUP)
    decay, kd, bb = [], [], []
    for d in range(2):
        w = -jax.nn.softplus(-(w0[d] + jnp.tanh(wd) @ w2[d])) - 0.5
        decay.append(jnp.exp(-jnp.exp(w)))
        a = jax.nn.sigmoid(a0[d] + ad @ a2[d])
        kd.append(k * (1.0 + (a - 1.0) * k_a))
        bb.append(kk * a)
    n_chain = 2 * Bn * RW_HEADS
    nc = -(-n_chain // V7X_LANES) * V7X_LANES
    y = rwkv_scan(_to_chains(-kk, -kk, nc), _to_chains(decay[0], decay[1], nc),
                  _to_chains(bb[0], bb[1], nc), _to_chains(kd[0], kd[1], nc),
                  _to_chains(r, r, nc), _to_chains(v, v, nc))
    half = Bn * RW_HEADS

    def back(t):
        return t.reshape(L, RW_HEAD, Bn, RW_HEADS).transpose(2, 0, 3, 1)

    y = back(y[..., :half]) + back(y[..., half:2 * half])[:, ::-1]
    mean = jnp.mean(y, axis=-1, keepdims=True)
    var = jnp.mean(jnp.square(y - mean), axis=-1, keepdims=True)
    y = ((y - mean) * lax.rsqrt(var + RW_LN_EPS)).reshape(Bn, L, W_GROUP) * ln_w + ln_b
    bonus = jnp.sum(heads(r) * heads(k) * r_k, axis=-1, keepdims=True) * heads(v)
    return (y + bonus.reshape(Bn, L, W_GROUP)) * g


S5_NSTATE = S5_GROUPS * S5_STATE
S5_T = 512


def _cmul(ar, ai, br, bi):
    return ar * br - ai * bi, ar * bi + ai * br


def _s5_operators(lam_re, lam_im, log_dt, b_re, b_im, c_re, c_im):
    dt = jnp.exp(log_dt)[..., None]
    mag = jnp.exp(lam_re * dt)
    ab_re = mag * jnp.cos(lam_im * dt)
    ab_im = mag * jnp.sin(lam_im * dt)
    den = lam_re * lam_re + lam_im * lam_im
    n_re = ab_re - 1.0
    f_re = (n_re * lam_re + ab_im * lam_im) / den
    f_im = (ab_im * lam_re - n_re * lam_im) / den
    bb_re, bb_im = _cmul(f_re[..., None], f_im[..., None], b_re, b_im)
    eye = jnp.eye(S5_GROUPS, dtype=jnp.float32)

    def in_map(bb):
        return jnp.einsum('gh,dhnc->dgchn', eye, bb).reshape(2, W_GROUP, S5_NSTATE)

    def out_map(c):
        return jnp.einsum('hg,dgcn->dhngc', eye, c).reshape(2, S5_NSTATE, W_GROUP)

    bmat = jnp.concatenate([in_map(bb_re), in_map(bb_im)], axis=-1)
    cmat = jnp.concatenate([out_map(c_re), -out_map(c_im)], axis=1)
    lam = jnp.stack([ab_re.reshape(2, S5_NSTATE), ab_im.reshape(2, S5_NSTATE)], axis=1)
    return lam, bmat.astype(jnp.bfloat16), cmat.astype(jnp.bfloat16)


def _s5_scan_kernel(u_ref, lam_ref, bmat_ref, cmat_ref, y_ref, st_ref, carry_ref):
    d = pl.program_id(1)

    @pl.when(pl.program_id(2) == 0)
    def _():
        carry_ref[...] = jnp.zeros_like(carry_ref)

    st_ref[...] = jnp.dot(u_ref[...].astype(jnp.bfloat16), bmat_ref[0], preferred_element_type=jnp.float32)
    lam_r = lam_ref[0, 0:1, :]
    lam_i = lam_ref[0, 1:2, :]
    re = slice(0, S5_NSTATE)
    im = slice(S5_NSTATE, 2 * S5_NSTATE)

    def step(i, carry):
        sr, si = carry
        t = i + d * (S5_T - 1 - 2 * i)
        nr = lam_r * sr - lam_i * si + st_ref[pl.ds(t, 1), re]
        ni = lam_r * si + lam_i * sr + st_ref[pl.ds(t, 1), im]
        st_ref[pl.ds(t, 1), re] = nr
        st_ref[pl.ds(t, 1), im] = ni
        return nr, ni

    sr, si = lax.fori_loop(0, S5_T, step, (carry_ref[0:1, re], carry_ref[0:1, im]), unroll=4)
    carry_ref[0:1, re] = sr
    carry_ref[0:1, im] = si
    y_ref[0] = jnp.dot(st_ref[...].astype(jnp.bfloat16), cmat_ref[0], preferred_element_type=jnp.float32)


def s5_scan(z, lam, bmat, cmat, n_seq):
    n = z.shape[0]
    nch = n // n_seq // S5_T

    def row_block(b, d, c):
        return b * nch + c + d * (nch - 1 - 2 * c)

    return pl.pallas_call(
        _s5_scan_kernel,
        grid=(n_seq, 2, nch),
        in_specs=[pl.BlockSpec((S5_T, W_GROUP), lambda b, d, c: (row_block(b, d, c), 0)),
                  pl.BlockSpec((1, 2, S5_NSTATE), lambda b, d, c: (d, 0, 0)),
                  pl.BlockSpec((1, W_GROUP, 2 * S5_NSTATE), lambda b, d, c: (d, 0, 0)),
                  pl.BlockSpec((1, 2 * S5_NSTATE, W_GROUP), lambda b, d, c: (d, 0, 0))],
        out_specs=pl.BlockSpec((1, S5_T, W_GROUP), lambda b, d, c: (d, row_block(b, d, c), 0)),
        out_shape=jax.ShapeDtypeStruct((2, n, W_GROUP), jnp.float32),
        scratch_shapes=[pltpu.VMEM((S5_T, 2 * S5_NSTATE), jnp.float32),
                        pltpu.VMEM((8, 2 * S5_NSTATE), jnp.float32)],
        compiler_params=pltpu.CompilerParams(dimension_semantics=("arbitrary", "arbitrary", "arbitrary"),
                                             vmem_limit_bytes=VMEM_LIMIT),
        name="s5_scan",
    )(z, lam, bmat, cmat)


def _s5_finish_kernel(y_ref, z_ref, d_ref, w_ref, b_ref, o_ref):
    y = y_ref[0] + y_ref[1] + d_ref[...] * z_ref[...]
    g = jax.nn.gelu(y)
    gate = jnp.dot(g.astype(jnp.bfloat16), w_ref[...], preferred_element_type=jnp.float32) + b_ref[...]
    o_ref[...] = g * jax.nn.sigmoid(gate)


def s5_finish(y2, z, d, glu_w_bf16, glu_b):
    n = z.shape[0]
    vec = pl.BlockSpec((1, W_GROUP), lambda i: (0, 0))
    return pl.pallas_call(
        _s5_finish_kernel,
        grid=(n // ROW_TILE,),
        in_specs=[pl.BlockSpec((2, ROW_TILE, W_GROUP), lambda i: (0, i, 0)),
                  pl.BlockSpec((ROW_TILE, W_GROUP), lambda i: (i, 0)),
                  vec, pl.BlockSpec((W_GROUP, W_GROUP), lambda i: (0, 0)), vec],
        out_specs=pl.BlockSpec((ROW_TILE, W_GROUP), lambda i: (i, 0)),
        out_shape=jax.ShapeDtypeStruct((n, W_GROUP), jnp.float32),
        compiler_params=pltpu.CompilerParams(dimension_semantics=("parallel",),
                                             vmem_limit_bytes=VMEM_LIMIT),
        name="s5_finish",
    )(y2, z, d.reshape(1, W_GROUP), glu_w_bf16, glu_b.reshape(1, W_GROUP))


def s5_mixer(z, n_seq, lam_re, lam_im, log_dt, b_re, b_im, c_re, c_im, d, glu_w, glu_b):
    lam, bmat, cmat = _s5_operators(lam_re, lam_im, log_dt, b_re, b_im, c_re, c_im)
    y2 = s5_scan(z, lam, bmat, cmat, n_seq)
    return s5_finish(y2, z, d, glu_w.astype(jnp.bfloat16), glu_b)


def _hyena_filter_spectra(L, w1, b1, freq, w2, b2, w3, log_rate):
    t = jnp.arange(L, dtype=jnp.float32) / L
    ang = 2.0 * math.pi * t[:, None] * jnp.arange(1, HY_BANDS + 1, dtype=jnp.float32)
    feats = jnp.concatenate([t[:, None], jnp.sin(ang), jnp.cos(ang)], axis=-1)
    h = jnp.sin(freq[0] * (feats @ w1 + b1))
    h = jnp.sin(freq[1] * (h @ w2 + b2))
    h = (h @ w3).reshape(L, 2, HY_ORDER, W_GROUP)
    h = h * jnp.exp(-jnp.exp(log_rate)[None] * t[:, None, None, None])
    fwd, bwd = h[:, 0], h[:, 1]
    k = jnp.concatenate([fwd, jnp.zeros_like(fwd[:1]), bwd[:0:-1]], axis=0)
    k = k / jnp.sum(jnp.abs(k), axis=0, keepdims=True)
    return jnp.fft.rfft(k, axis=0)


def hyena_mixer(z, conv_w, conv_b, f_w1, f_b1, f_freq, f_w2, f_b2, f_w3, log_rate, d):
    Bn, L, _ = z.shape
    z = conv_w[0] * _shift_prev(z) + conv_w[1] * z + conv_w[2] * _shift_next(z) + conv_b
    v, x1, x2 = jnp.split(z, HY_ORDER + 1, axis=-1)
    kf = _hyena_filter_spectra(L, f_w1, f_b1, f_freq, f_w2, f_b2, f_w3, log_rate)
    u = v
    for o, gate in enumerate((x1, x2)):
        conv = jnp.fft.irfft(jnp.fft.rfft(u, n=2 * L, axis=1) * kf[None, :, o], n=2 * L, axis=1)[:, :L]
        u = gate * (conv + d[o] * u)
    return u


NA_BAND = 8
NA_KROWS = 2 * NA_BAND
NA_KBLK = 4


def _na_bias_tables(rel_bias, rows):
    tabs = []
    for band in (0, 1, rows // NA_BAND - 1):
        kb = int(np.clip(NA_BAND * band - NA_WIN_R // 2, 0, rows - NA_KROWS))
        r = NA_BAND * band + np.arange(NA_BAND)
        rs = np.clip(r - NA_WIN_R // 2, 0, rows - NA_WIN_R)
        krow = kb + np.arange(NA_KROWS)
        row_ok = (krow[None, :] >= rs[:, None]) & (krow[None, :] < rs[:, None] + NA_WIN_R)
        dr = np.clip(krow[None, :] - r[:, None] + NA_WIN_R - 1, 0, 2 * NA_WIN_R - 2)
        qc = np.arange(GRID_W)
        ws = np.clip(qc - NA_WIN_C // 2, 0, GRID_W - NA_WIN_C)
        col_ok = (qc[None, :] >= ws[:, None]) & (qc[None, :] < ws[:, None] + NA_WIN_C)
        dc = np.clip(qc[None, :] - qc[:, None] + NA_WIN_C - 1, 0, 2 * NA_WIN_C - 2)
        bias = rel_bias[:, dr[:, None, :, None], dc[None, :, None, :]]
        ok = row_ok[:, None, :, None] & col_ok[None, :, None, :]
        tab = jnp.where(jnp.asarray(ok)[None], bias, NEG_INF)
        tabs.append(tab.reshape(NA_HEADS, NA_BAND * GRID_W, NA_KROWS * GRID_W))
    return jnp.stack(tabs)


def _na_kernel(q_ref, k0_ref, k1_ref, k2_ref, k3_ref, v0_ref, v1_ref, v2_ref, v3_ref, qg_ref, kg_ref,
               tab_ref, o_ref):
    q = q_ref[...]
    k = jnp.concatenate([k0_ref[...], k1_ref[...], k2_ref[...], k3_ref[...]], axis=0)
    v = jnp.concatenate([v0_ref[...], v1_ref[...], v2_ref[...], v3_ref[...]], axis=0)
    for h in range(NA_HEADS):
        sl = slice(h * NA_HEAD, (h + 1) * NA_HEAD)
        qh = (_rms(q[:, sl]) * qg_ref[...] * (NA_HEAD ** -0.5)).astype(jnp.bfloat16)
        kh = (_rms(k[:, sl]) * kg_ref[...]).astype(jnp.bfloat16)
        s = lax.dot_general(qh, kh, (((1,), (1,)), ((), ())), preferred_element_type=jnp.float32)
        s = s + tab_ref[0, h]
        p = jnp.exp(s - jnp.max(s, axis=-1, keepdims=True))
        den = jnp.sum(p, axis=-1, keepdims=True)
        o = jnp.dot(p.astype(jnp.bfloat16), v[:, sl].astype(jnp.bfloat16), preferred_element_type=jnp.float32)
        o_ref[:, sl] = o / den


def na_mixer(q, k, v, n_seq, q_g, k_g, rel_bias):
    n = q.shape[0]
    rows = n // n_seq // GRID_W
    n_band = rows // NA_BAND
    tab = _na_bias_tables(rel_bias, rows)
    qtok = NA_BAND * GRID_W
    ktok = NA_KBLK * GRID_W
    kblk_per_seq = rows // NA_KBLK

    def kv_spec(j):
        def index(i, b):
            first = jnp.clip(NA_BAND // NA_KBLK * i - 1, 0, kblk_per_seq - NA_KROWS // NA_KBLK)
            return (b * kblk_per_seq + first + j, 0)
        return pl.BlockSpec((ktok, W_GROUP), index)

    def tab_index(i, b):
        return (jnp.where(i == 0, 0, jnp.where(i == n_band - 1, 2, 1)), 0, 0, 0)

    qspec = pl.BlockSpec((qtok, W_GROUP), lambda i, b: (b * n_band + i, 0))
    gspec = pl.BlockSpec((1, NA_HEAD), lambda i, b: (0, 0))
    kvs = [kv_spec(j) for j in range(NA_KROWS // NA_KBLK)]
    return pl.pallas_call(
        _na_kernel,
        grid=(n_band, n_seq),
        in_specs=[qspec] + kvs + kvs + [gspec, gspec,
                  pl.BlockSpec((1, NA_HEADS, qtok, NA_KROWS * GRID_W), tab_index)],
        out_specs=qspec,
        out_shape=jax.ShapeDtypeStruct((n, W_GROUP), jnp.float32),
        compiler_params=pltpu.CompilerParams(dimension_semantics=("arbitrary", "arbitrary"),
                                             vmem_limit_bytes=VMEM_LIMIT),
        name="na_attn",
    )(q, k, k, k, k, v, v, v, v, q_g.reshape(1, NA_HEAD), k_g.reshape(1, NA_HEAD), tab)


def _trunk(x, p):
    Bn, L, _ = x.shape
    n = Bn * L
    x = x.reshape(n, D_MODEL)
    bf = jnp.bfloat16
    for l in range(DEPTH):
        z_s5, z_hy, z_rw, z_q, z_k, z_v = in_proj(x, p['ln1_g'][l], p['w_in'][l].astype(bf))
        y_s5 = s5_mixer(z_s5, Bn, p['s5_lam_re'][l], p['s5_lam_im'][l], p['s5_log_dt'][l],
                        p['s5_b_re'][l], p['s5_b_im'][l], p['s5_c_re'][l], p['s5_c_im'][l], p['s5_d'][l],
                        p['s5_glu_w'][l], p['s5_glu_b'][l])
        y_hy = hyena_mixer(z_hy.reshape(Bn, L, -1), p['hy_conv_w'][l], p['hy_conv_b'][l], p['hy_f_w1'][l],
                           p['hy_f_b1'][l], p['hy_f_freq'][l], p['hy_f_w2'][l], p['hy_f_b2'][l],
                           p['hy_f_w3'][l], p['hy_log_rate'][l], p['hy_d'][l])
        y_rw = rwkv_mixer(z_rw.reshape(Bn, L, -1), p['rw_mu'][l], p['rw_w0'][l], p['rw_w2'][l],
                          p['rw_a0'][l], p['rw_a2'][l], p['rw_g2'][l], p['rw_k_k'][l], p['rw_k_a'][l],
                          p['rw_r_k'][l], p['rw_ln_w'][l], p['rw_ln_b'][l])
        y_na = na_mixer(z_q, z_k, z_v, Bn, p['na_q_g'][l], p['na_k_g'][l], p['na_rel_bias'][l])
        ys = [y_s5, y_hy.reshape(n, W_GROUP), y_rw.reshape(n, W_GROUP), y_na]
        x = out_proj(x, ys, p['grp_g'][l], p['w_out'][l].astype(bf))
        x = mlp(x, p['ln2_g'][l], p['w_mlp1'][l].astype(bf), p['w_mlp2'][l].astype(bf))
    return x.reshape(Bn, L, D_MODEL)


def kernel(x_prompt, x_sample, ln1_g, w_in, s5_lam_re, s5_lam_im, s5_log_dt, s5_b_re, s5_b_im,
           s5_c_re, s5_c_im, s5_d, s5_glu_w, s5_glu_b, hy_conv_w, hy_conv_b, hy_f_w1, hy_f_b1,
           hy_f_freq, hy_f_w2, hy_f_b2, hy_f_w3, hy_log_rate, hy_d, rw_mu, rw_w0, rw_w2, rw_a0,
           rw_a2, rw_g2, rw_k_k, rw_k_a, rw_r_k, rw_ln_w, rw_ln_b, na_q_g, na_k_g, na_rel_bias,
           grp_g, w_out, ln2_g, w_mlp1, w_mlp2):
    p = dict(ln1_g=ln1_g, w_in=w_in, s5_lam_re=s5_lam_re, s5_lam_im=s5_lam_im, s5_log_dt=s5_log_dt,
             s5_b_re=s5_b_re, s5_b_im=s5_b_im, s5_c_re=s5_c_re, s5_c_im=s5_c_im, s5_d=s5_d,
             s5_glu_w=s5_glu_w, s5_glu_b=s5_glu_b, hy_conv_w=hy_conv_w, hy_conv_b=hy_conv_b,
             hy_f_w1=hy_f_w1, hy_f_b1=hy_f_b1, hy_f_freq=hy_f_freq, hy_f_w2=hy_f_w2, hy_f_b2=hy_f_b2,
             hy_f_w3=hy_f_w3, hy_log_rate=hy_log_rate, hy_d=hy_d, rw_mu=rw_mu, rw_w0=rw_w0, rw_w2=rw_w2,
             rw_a0=rw_a0, rw_a2=rw_a2, rw_g2=rw_g2, rw_k_k=rw_k_k, rw_k_a=rw_k_a, rw_r_k=rw_r_k,
             rw_ln_w=rw_ln_w, rw_ln_b=rw_ln_b, na_q_g=na_q_g, na_k_g=na_k_g, na_rel_bias=na_rel_bias,
             grp_g=grp_g, w_out=w_out, ln2_g=ln2_g, w_mlp1=w_mlp1, w_mlp2=w_mlp2)
    nb = x_prompt.shape[0]
    y = _trunk(jnp.concatenate([x_prompt, x_sample], axis=0), p)
    return (y[:nb], y[nb:])
```

```python
import math

import jax
import jax.numpy as jnp
import numpy as np
from jax import lax
from jax.experimental import pallas as pl
from jax.experimental.pallas import tpu as pltpu

D_MODEL = 1024
DEPTH = 4
GRID_W = 64
W_GROUP = 256
N_MIXERS = 4
D_FF = 4 * D_MODEL
NORM_EPS = 1e-6

S5_CH = 16
S5_GROUPS = W_GROUP // S5_CH
S5_STATE = 64
S5_IN = W_GROUP

HY_ORDER = 2
HY_BANDS = 8
HY_IN = (HY_ORDER + 1) * W_GROUP

RW_HEAD = 64
RW_HEADS = W_GROUP // RW_HEAD
RW_DECAY_RANK = 64
RW_A_RANK = 64
RW_G_RANK = 128
RW_LN_EPS = 64e-5
RW_IN = 3 * W_GROUP + RW_DECAY_RANK + RW_A_RANK + RW_G_RANK
RW_SPLITS = (W_GROUP, 2 * W_GROUP, 3 * W_GROUP, 3 * W_GROUP + RW_DECAY_RANK,
             3 * W_GROUP + RW_DECAY_RANK + RW_A_RANK)

NA_HEAD = 64
NA_HEADS = W_GROUP // NA_HEAD
NA_WIN_R = 8
NA_WIN_C = 16
NEG_INF = -1e30

D_IN = S5_IN + HY_IN + RW_IN + 3 * W_GROUP

V7X_LANES = 128
VMEM_LIMIT = 48 * 1024 * 1024

ROW_TILE = 512
FF_TILE = 1024
SCAN_T = 16


def _rms(x):
    return x * lax.rsqrt(jnp.mean(x * x, axis=-1, keepdims=True) + NORM_EPS)


IN_WIDTHS = (S5_IN, HY_IN, RW_IN, W_GROUP, W_GROUP, W_GROUP)


def _in_proj_kernel(x_ref, g_ref, w_ref, *o_refs):
    h = (_rms(x_ref[...]) * g_ref[...]).astype(jnp.bfloat16)
    lo = 0
    for o_ref, width in zip(o_refs, IN_WIDTHS):
        o_ref[...] = jnp.dot(h, w_ref[:, lo:lo + width], preferred_element_type=jnp.float32)
        lo += width


def in_proj(x, g, w_bf16):
    n = x.shape[0]
    return pl.pallas_call(
        _in_proj_kernel,
        grid=(n // ROW_TILE,),
        in_specs=[pl.BlockSpec((ROW_TILE, D_MODEL), lambda i: (i, 0)),
                  pl.BlockSpec((1, D_MODEL), lambda i: (0, 0)),
                  pl.BlockSpec((D_MODEL, D_IN), lambda i: (0, 0))],
        out_specs=[pl.BlockSpec((ROW_TILE, w), lambda i: (i, 0)) for w in IN_WIDTHS],
        out_shape=[jax.ShapeDtypeStruct((n, w), jnp.float32) for w in IN_WIDTHS],
        compiler_params=pltpu.CompilerParams(dimension_semantics=("parallel",),
                                             vmem_limit_bytes=VMEM_LIMIT),
        name="in_proj",
    )(x, g.reshape(1, D_MODEL), w_bf16)


def _out_proj_kernel(x_ref, y0_ref, y1_ref, y2_ref, y3_ref, g_ref, w_ref, o_ref):
    acc = x_ref[...]
    for i, y_ref in enumerate((y0_ref, y1_ref, y2_ref, y3_ref)):
        n = (_rms(y_ref[...]) * g_ref[i:i + 1, :]).astype(jnp.bfloat16)
        acc = acc + jnp.dot(n, w_ref[i * W_GROUP:(i + 1) * W_GROUP, :],
                            preferred_element_type=jnp.float32)
    o_ref[...] = acc


def out_proj(x, ys, g, w_bf16):
    n = x.shape[0]
    row = lambda w: pl.BlockSpec((ROW_TILE, w), lambda i: (i, 0))
    return pl.pallas_call(
        _out_proj_kernel,
        grid=(n // ROW_TILE,),
        in_specs=[row(D_MODEL)] + [row(W_GROUP)] * N_MIXERS
                 + [pl.BlockSpec((N_MIXERS, W_GROUP), lambda i: (0, 0)),
                    pl.BlockSpec((D_MODEL, D_MODEL), lambda i: (0, 0))],
        out_specs=row(D_MODEL),
        out_shape=jax.ShapeDtypeStruct((n, D_MODEL), jnp.float32),
        compiler_params=pltpu.CompilerParams(dimension_semantics=("parallel",),
                                             vmem_limit_bytes=VMEM_LIMIT),
        name="out_proj",
    )(x, *ys, g, w_bf16)


def _mlp_kernel(x_ref, g_ref, w1_ref, w2_ref, o_ref, h_ref):
    j = pl.program_id(1)

    @pl.when(j == 0)
    def _():
        x = x_ref[...]
        h_ref[...] = (_rms(x) * g_ref[...]).astype(jnp.bfloat16)
        o_ref[...] = x

    a = jnp.dot(h_ref[...], w1_ref[...], preferred_element_type=jnp.float32)
    a = jnp.square(jnp.maximum(a, 0.0)).astype(jnp.bfloat16)
    o_ref[...] += jnp.dot(a, w2_ref[...], preferred_element_type=jnp.float32)


def mlp(x, g, w1_bf16, w2_bf16):
    n = x.shape[0]
    return pl.pallas_call(
        _mlp_kernel,
        grid=(n // ROW_TILE, D_FF // FF_TILE),
        in_specs=[pl.BlockSpec((ROW_TILE, D_MODEL), lambda i, j: (i, 0)),
                  pl.BlockSpec((1, D_MODEL), lambda i, j: (0, 0)),
                  pl.BlockSpec((D_MODEL, FF_TILE), lambda i, j: (0, j)),
                  pl.BlockSpec((FF_TILE, D_MODEL), lambda i, j: (j, 0))],
        out_specs=pl.BlockSpec((ROW_TILE, D_MODEL), lambda i, j: (i, 0)),
        out_shape=jax.ShapeDtypeStruct((n, D_MODEL), jnp.float32),
        scratch_shapes=[pltpu.VMEM((ROW_TILE, D_MODEL), jnp.bfloat16)],
        compiler_params=pltpu.CompilerParams(dimension_semantics=("parallel", "arbitrary"),
                                             vmem_limit_bytes=VMEM_LIMIT),
        name="mlp",
    )(x, g.reshape(1, D_MODEL), w1_bf16, w2_bf16)


def _head_sum(x):
    lane = lax.broadcasted_iota(jnp.int32, (W_GROUP, W_GROUP), 0) // RW_HEAD
    col = lax.broadcasted_iota(jnp.int32, (W_GROUP, W_GROUP), 1) // RW_HEAD
    ones = (lane == col).astype(jnp.float32)
    return jnp.dot(x, ones, precision=lax.Precision.HIGHEST, preferred_element_type=jnp.float32)


def _softplus(x):
    return jnp.maximum(x, 0.0) + jnp.log(1.0 + jnp.exp(-jnp.abs(x)))


def _shifted(z, prev_row, next_row):
    t = z.shape[0]
    row = lax.broadcasted_iota(jnp.int32, z.shape, 0)
    zp = jnp.where(row == 0, prev_row, pltpu.roll(z, 1, 0))
    zn = jnp.where(row == t - 1, next_row, pltpu.roll(z, t - 1, 0))
    return zp, zn


def _rwkv_prep_kernel(z_ref, zp_ref, zn_ref, mu_ref, w0_ref, w2_ref, a0_ref, a2_ref, g2_ref, kk_ref, ka_ref,
                      rk_ref, nkk_ref, r_ref, v_ref, dec0_ref, dec1_ref, kd0_ref, kd1_ref, b0_ref, b1_ref,
                      g_ref, bonus_ref):
    z = z_ref[...]
    zp, zn = _shifted(z, zp_ref[0], zn_ref[0])
    z = z + mu_ref[...] * (0.5 * (zp + zn) - z)
    r = z[:, 0:W_GROUP]
    k = z[:, W_GROUP:2 * W_GROUP]
    v = z[:, 2 * W_GROUP:3 * W_GROUP]
    wd = z[:, RW_SPLITS[2]:RW_SPLITS[3]]
    ad = z[:, RW_SPLITS[3]:RW_SPLITS[4]]
    gd = z[:, RW_SPLITS[4]:RW_IN]
    bf = jnp.bfloat16
    g_ref[...] = jnp.dot(jax.nn.sigmoid(gd).astype(bf), g2_ref[...], preferred_element_type=jnp.float32)
    kk = k * kk_ref[...]
    kk = kk / jnp.maximum(jnp.sqrt(_head_sum(kk * kk)), 1e-12)
    nkk_ref[...] = -kk
    r_ref[...] = r
    v_ref[...] = v
    bonus_ref[...] = _head_sum(r * k * rk_ref[...]) * v
    tw = jnp.tanh(wd).astype(bf)
    adb = ad.astype(bf)
    for d, (dec_ref, kd_ref, b_ref) in enumerate(((dec0_ref, kd0_ref, b0_ref), (dec1_ref, kd1_ref, b1_ref))):
        w = w0_ref[d:d + 1, :] + jnp.dot(tw, w2_ref[d], preferred_element_type=jnp.float32)
        w = -_softplus(-w) - 0.5
        dec_ref[...] = jnp.exp(-jnp.exp(w))
        a = jax.nn.sigmoid(a0_ref[d:d + 1, :] + jnp.dot(adb, a2_ref[d], preferred_element_type=jnp.float32))
        kd_ref[...] = k * (1.0 + (a - 1.0) * ka_ref[...])
        b_ref[...] = kk * a


def _halo_rows(z, n_seq, tile):
    n, c = z.shape
    zt = z.reshape(n_seq, n // n_seq // tile, tile, c)
    zero = jnp.zeros((n_seq, 1, c), z.dtype)
    prev = jnp.concatenate([zero, zt[:, :-1, -1]], axis=1).reshape(n // tile, 1, c)
    nxt = jnp.concatenate([zt[:, 1:, 0], zero], axis=1).reshape(n // tile, 1, c)
    return prev, nxt


def rwkv_prep(z, n_seq, mu, w0, w2, a0, a2, g2, k_k, k_a, r_k):
    n = z.shape[0]
    prev, nxt = _halo_rows(z, n_seq, ROW_TILE)
    bf = jnp.bfloat16
    full = lambda *s: pl.BlockSpec(s, lambda i: (0,) * len(s))
    row = pl.BlockSpec((ROW_TILE, W_GROUP), lambda i: (i, 0))
    halo = pl.BlockSpec((1, 1, RW_IN), lambda i: (i, 0, 0))
    vec = lambda x: x.reshape(1, -1)
    return pl.pallas_call(
        _rwkv_prep_kernel,
        grid=(n // ROW_TILE,),
        in_specs=[pl.BlockSpec((ROW_TILE, RW_IN), lambda i: (i, 0)), halo, halo, full(1, RW_IN),
                  full(2, W_GROUP), full(2, RW_DECAY_RANK, W_GROUP), full(2, W_GROUP),
                  full(2, RW_A_RANK, W_GROUP), full(RW_G_RANK, W_GROUP), full(1, W_GROUP), full(1, W_GROUP),
                  full(1, W_GROUP)],
        out_specs=[row] * 11,
        out_shape=[jax.ShapeDtypeStruct((n, W_GROUP), jnp.float32)] * 11,
        compiler_params=pltpu.CompilerParams(dimension_semantics=("parallel",),
                                             vmem_limit_bytes=VMEM_LIMIT),
        name="rwkv_prep",
    )(z, prev, nxt, vec(mu), w0, w2.astype(bf), a0, a2.astype(bf), g2.astype(bf), vec(k_k), vec(k_a), vec(r_k))


def _rwkv_scan_kernel(af_ref, wf_ref, bf_ref, kf_ref, rf_ref, vf_ref, ab_ref, wb_ref, bb_ref, kb_ref, rb_ref,
                      vb_ref, yf_ref, yb_ref, s_ref):
    @pl.when(pl.program_id(0) == 0)
    def _():
        s_ref[...] = jnp.zeros_like(s_ref)

    dirs = ((af_ref, wf_ref, bf_ref, kf_ref, rf_ref, vf_ref, yf_ref),
            (ab_ref, wb_ref, bb_ref, kb_ref, rb_ref, vb_ref, yb_ref))

    def step(i, carry):
        for j, (a_ref, w_ref, b_ref, k_ref, r_ref, v_ref, y_ref) in enumerate(dirs):
            t = i if j == 0 else SCAN_T - 1 - i
            vt = v_ref[t]
            sa = s_ref[j, 0] * a_ref[t, 0:1, :]
            for k in range(1, RW_HEAD):
                sa = sa + s_ref[j, k] * a_ref[t, k:k + 1, :]
            y = None
            for k in range(RW_HEAD):
                s = s_ref[j, k] * w_ref[t, k:k + 1, :] + sa * b_ref[t, k:k + 1, :] + vt * k_ref[t, k:k + 1, :]
                s_ref[j, k] = s
                yk = s * r_ref[t, k:k + 1, :]
                y = yk if y is None else y + yk
            y_ref[t] = y
        return carry

    lax.fori_loop(0, SCAN_T, step, 0)


def rwkv_scan(a, r, v, w_f, b_f, k_f, w_b, b_b, k_b):
    L = a.shape[0]
    nblk = L // SCAN_T
    fwd = pl.BlockSpec((SCAN_T, RW_HEAD, V7X_LANES), lambda i: (i, 0, 0))
    bwd = pl.BlockSpec((SCAN_T, RW_HEAD, V7X_LANES), lambda i: (nblk - 1 - i, 0, 0))
    out = jax.ShapeDtypeStruct((L, RW_HEAD, V7X_LANES), jnp.float32)
    return pl.pallas_call(
        _rwkv_scan_kernel,
        grid=(nblk,),
        in_specs=[fwd] * 6 + [bwd] * 6,
        out_specs=[fwd, bwd],
        out_shape=[out, out],
        scratch_shapes=[pltpu.VMEM((2, RW_HEAD, RW_HEAD, V7X_LANES), jnp.float32)],
        compiler_params=pltpu.CompilerParams(dimension_semantics=("arbitrary",),
                                             vmem_limit_bytes=VMEM_LIMIT),
        name="rwkv_scan",
    )(a, w_f, b_f, k_f, r, v, a, w_b, b_b, k_b, r, v)


def _rwkv_post_kernel(yf_ref, yb_ref, bonus_ref, g_ref, lw_ref, lb_ref, o_ref):
    y = yf_ref[...] + yb_ref[...]
    mean = _head_sum(y) * (1.0 / RW_HEAD)
    c = y - mean
    var = _head_sum(c * c) * (1.0 / RW_HEAD)
    y = c * lax.rsqrt(var + RW_LN_EPS) * lw_ref[...] + lb_ref[...]
    o_ref[...] = (y + bonus_ref[...]) * g_ref[...]


def rwkv_post(y_f, y_b, bonus, g, ln_w, ln_b):
    n = y_f.shape[0]
    row = pl.BlockSpec((ROW_TILE, W_GROUP), lambda i: (i, 0))
    vec = pl.BlockSpec((1, W_GROUP), lambda i: (0, 0))
    return pl.pallas_call(
        _rwkv_post_kernel,
        grid=(n // ROW_TILE,),
        in_specs=[row, row, row, row, vec, vec],
        out_specs=row,
        out_shape=jax.ShapeDtypeStruct((n, W_GROUP), jnp.float32),
        compiler_params=pltpu.CompilerParams(dimension_semantics=("parallel",),
                                             vmem_limit_bytes=VMEM_LIMIT),
        name="rwkv_post",
    )(y_f, y_b, bonus, g, ln_w.reshape(1, W_GROUP), ln_b.reshape(1, W_GROUP))


def rwkv_mixer(z, n_seq, mu, w0, w2, a0, a2, g2, k_k, k_a, r_k, ln_w, ln_b):
    n = z.shape[0]
    L = n // n_seq
    n_chain = n_seq * RW_HEADS
    assert n_chain <= V7X_LANES
    nkk, r, v, dec0, dec1, kd0, kd1, b0, b1, g, bonus = rwkv_prep(z, n_seq, mu, w0, w2, a0, a2, g2, k_k, k_a, r_k)

    def chains(x):
        x = x.reshape(n_seq, L, RW_HEADS, RW_HEAD).transpose(1, 3, 0, 2).reshape(L, RW_HEAD, n_chain)
        return jnp.pad(x, ((0, 0), (0, 0), (0, V7X_LANES - n_chain)))

    def tokens(y):
        return y[..., :n_chain].reshape(L, RW_HEAD, n_seq, RW_HEADS).transpose(2, 0, 3, 1).reshape(n, W_GROUP)

    y_f, y_b = rwkv_scan(chains(nkk), chains(r), chains(v), chains(dec0), chains(b0), chains(kd0),
                         chains(dec1), chains(b1), chains(kd1))
    return rwkv_post(tokens(y_f), tokens(y_b), bonus, g, ln_w, ln_b)


S5_NSTATE = S5_GROUPS * S5_STATE
S5_T = 512


def _cmul(ar, ai, br, bi):
    return ar * br - ai * bi, ar * bi + ai * br


def _s5_operators(lam_re, lam_im, log_dt, b_re, b_im, c_re, c_im):
    dt = jnp.exp(log_dt)[..., None]
    mag = jnp.exp(lam_re * dt)
    ab_re = mag * jnp.cos(lam_im * dt)
    ab_im = mag * jnp.sin(lam_im * dt)
    den = lam_re * lam_re + lam_im * lam_im
    n_re = ab_re - 1.0
    f_re = (n_re * lam_re + ab_im * lam_im) / den
    f_im = (ab_im * lam_re - n_re * lam_im) / den
    bb_re, bb_im = _cmul(f_re[..., None], f_im[..., None], b_re, b_im)
    eye = jnp.eye(S5_GROUPS, dtype=jnp.float32)

    def in_map(bb):
        return jnp.einsum('gh,dhnc->dgchn', eye, bb).reshape(2, W_GROUP, S5_NSTATE)

    def out_map(c):
        return jnp.einsum('hg,dgcn->dhngc', eye, c).reshape(2, S5_NSTATE, W_GROUP)

    bmat = jnp.concatenate([in_map(bb_re), in_map(bb_im)], axis=-1)
    cmat = jnp.concatenate([out_map(c_re), -out_map(c_im)], axis=1)
    lam = jnp.stack([ab_re.reshape(2, S5_NSTATE), ab_im.reshape(2, S5_NSTATE)], axis=1)
    return lam, bmat.astype(jnp.bfloat16), cmat.astype(jnp.bfloat16)


def _s5_scan_kernel(u_ref, lam_ref, bmat_ref, cmat_ref, y_ref, st_ref, carry_ref):
    d = pl.program_id(1)

    @pl.when(pl.program_id(2) == 0)
    def _():
        carry_ref[...] = jnp.zeros_like(carry_ref)

    st_ref[...] = jnp.dot(u_ref[...].astype(jnp.bfloat16), bmat_ref[0], preferred_element_type=jnp.float32)
    lam_r = lam_ref[0, 0:1, :]
    lam_i = lam_ref[0, 1:2, :]
    re = slice(0, S5_NSTATE)
    im = slice(S5_NSTATE, 2 * S5_NSTATE)

    def step(i, carry):
        sr, si = carry
        t = i + d * (S5_T - 1 - 2 * i)
        nr = lam_r * sr - lam_i * si + st_ref[pl.ds(t, 1), re]
        ni = lam_r * si + lam_i * sr + st_ref[pl.ds(t, 1), im]
        st_ref[pl.ds(t, 1), re] = nr
        st_ref[pl.ds(t, 1), im] = ni
        return nr, ni

    sr, si = lax.fori_loop(0, S5_T, step, (carry_ref[0:1, re], carry_ref[0:1, im]), unroll=4)
    carry_ref[0:1, re] = sr
    carry_ref[0:1, im] = si
    y_ref[0] = jnp.dot(st_ref[...].astype(jnp.bfloat16), cmat_ref[0], preferred_element_type=jnp.float32)


def s5_scan(z, lam, bmat, cmat, n_seq):
    n = z.shape[0]
    nch = n // n_seq // S5_T

    def row_block(b, d, c):
        return b * nch + c + d * (nch - 1 - 2 * c)

    return pl.pallas_call(
        _s5_scan_kernel,
        grid=(n_seq, 2, nch),
        in_specs=[pl.BlockSpec((S5_T, W_GROUP), lambda b, d, c: (row_block(b, d, c), 0)),
                  pl.BlockSpec((1, 2, S5_NSTATE), lambda b, d, c: (d, 0, 0)),
                  pl.BlockSpec((1, W_GROUP, 2 * S5_NSTATE), lambda b, d, c: (d, 0, 0)),
                  pl.BlockSpec((1, 2 * S5_NSTATE, W_GROUP), lambda b, d, c: (d, 0, 0))],
        out_specs=pl.BlockSpec((1, S5_T, W_GROUP), lambda b, d, c: (d, row_block(b, d, c), 0)),
        out_shape=jax.ShapeDtypeStruct((2, n, W_GROUP), jnp.float32),
        scratch_shapes=[pltpu.VMEM((S5_T, 2 * S5_NSTATE), jnp.float32),
                        pltpu.VMEM((8, 2 * S5_NSTATE), jnp.float32)],
        compiler_params=pltpu.CompilerParams(dimension_semantics=("arbitrary", "arbitrary", "arbitrary"),
                                             vmem_limit_bytes=VMEM_LIMIT),
        name="s5_scan",
    )(z, lam, bmat, cmat)


def _s5_finish_kernel(y_ref, z_ref, d_ref, w_ref, b_ref, o_ref):
    y = y_ref[0] + y_ref[1] + d_ref[...] * z_ref[...]
    g = jax.nn.gelu(y)
    gate = jnp.dot(g.astype(jnp.bfloat16), w_ref[...], preferred_element_type=jnp.float32) + b_ref[...]
    o_ref[...] = g * jax.nn.sigmoid(gate)


def s5_finish(y2, z, d, glu_w_bf16, glu_b):
    n = z.shape[0]
    vec = pl.BlockSpec((1, W_GROUP), lambda i: (0, 0))
    return pl.pallas_call(
        _s5_finish_kernel,
        grid=(n // ROW_TILE,),
        in_specs=[pl.BlockSpec((2, ROW_TILE, W_GROUP), lambda i: (0, i, 0)),
                  pl.BlockSpec((ROW_TILE, W_GROUP), lambda i: (i, 0)),
                  vec, pl.BlockSpec((W_GROUP, W_GROUP), lambda i: (0, 0)), vec],
        out_specs=pl.BlockSpec((ROW_TILE, W_GROUP), lambda i: (i, 0)),
        out_shape=jax.ShapeDtypeStruct((n, W_GROUP), jnp.float32),
        compiler_params=pltpu.CompilerParams(dimension_semantics=("parallel",),
                                             vmem_limit_bytes=VMEM_LIMIT),
        name="s5_finish",
    )(y2, z, d.reshape(1, W_GROUP), glu_w_bf16, glu_b.reshape(1, W_GROUP))


def s5_mixer(z, n_seq, lam_re, lam_im, log_dt, b_re, b_im, c_re, c_im, d, glu_w, glu_b):
    lam, bmat, cmat = _s5_operators(lam_re, lam_im, log_dt, b_re, b_im, c_re, c_im)
    y2 = s5_scan(z, lam, bmat, cmat, n_seq)
    return s5_finish(y2, z, d, glu_w.astype(jnp.bfloat16), glu_b)


def _shift_prev(x):
    return jnp.pad(x[:, :-1], ((0, 0), (1, 0), (0, 0)))


def _shift_next(x):
    return jnp.pad(x[:, 1:], ((0, 0), (0, 1), (0, 0)))


def _hyena_filter_spectra(L, w1, b1, freq, w2, b2, w3, log_rate):
    t = jnp.arange(L, dtype=jnp.float32) / L
    ang = 2.0 * math.pi * t[:, None] * jnp.arange(1, HY_BANDS + 1, dtype=jnp.float32)
    feats = jnp.concatenate([t[:, None], jnp.sin(ang), jnp.cos(ang)], axis=-1)
    h = jnp.sin(freq[0] * (feats @ w1 + b1))
    h = jnp.sin(freq[1] * (h @ w2 + b2))
    h = (h @ w3).reshape(L, 2, HY_ORDER, W_GROUP)
    h = h * jnp.exp(-jnp.exp(log_rate)[None] * t[:, None, None, None])
    fwd, bwd = h[:, 0], h[:, 1]
    k = jnp.concatenate([fwd, jnp.zeros_like(fwd[:1]), bwd[:0:-1]], axis=0)
    k = k / jnp.sum(jnp.abs(k), axis=0, keepdims=True)
    return jnp.fft.rfft(k, axis=0)


def hyena_mixer(z, conv_w, conv_b, f_w1, f_b1, f_freq, f_w2, f_b2, f_w3, log_rate, d):
    Bn, L, _ = z.shape
    z = conv_w[0] * _shift_prev(z) + conv_w[1] * z + conv_w[2] * _shift_next(z) + conv_b
    v, x1, x2 = jnp.split(z, HY_ORDER + 1, axis=-1)
    kf = _hyena_filter_spectra(L, f_w1, f_b1, f_freq, f_w2, f_b2, f_w3, log_rate)
    u = v
    for o, gate in enumerate((x1, x2)):
        conv = jnp.fft.irfft(jnp.fft.rfft(u, n=2 * L, axis=1) * kf[None, :, o], n=2 * L, axis=1)[:, :L]
        u = gate * (conv + d[o] * u)
    return u


NA_BAND = 8
NA_KROWS = 2 * NA_BAND
NA_KBLK = 4


def _na_bias_tables(rel_bias, rows):
    hi = lax.Precision.HIGHEST
    qc = np.arange(GRID_W)
    ws = np.clip(qc - NA_WIN_C // 2, 0, GRID_W - NA_WIN_C)
    col_ok = (qc[None, :] >= ws[:, None]) & (qc[None, :] < ws[:, None] + NA_WIN_C)
    dc = np.clip(qc[None, :] - qc[:, None] + NA_WIN_C - 1, 0, 2 * NA_WIN_C - 2)
    dc_sel = (dc[..., None] == np.arange(2 * NA_WIN_C - 1)).astype(np.float32)
    col_bias = jnp.einsum('qkc,hdc->hdqk', dc_sel, rel_bias, precision=hi)
    tabs = []
    for band in (0, 1, rows // NA_BAND - 1):
        kb = int(np.clip(NA_BAND * band - NA_WIN_R // 2, 0, rows - NA_KROWS))
        r = NA_BAND * band + np.arange(NA_BAND)
        rs = np.clip(r - NA_WIN_R // 2, 0, rows - NA_WIN_R)
        krow = kb + np.arange(NA_KROWS)
        row_ok = (krow[None, :] >= rs[:, None]) & (krow[None, :] < rs[:, None] + NA_WIN_R)
        dr = np.clip(krow[None, :] - r[:, None] + NA_WIN_R - 1, 0, 2 * NA_WIN_R - 2)
        dr_sel = (dr[..., None] == np.arange(2 * NA_WIN_R - 1)).astype(np.float32)
        bias = jnp.einsum('jwd,hdqk->hjqwk', dr_sel, col_bias, precision=hi)
        ok = row_ok[:, None, :, None] & col_ok[None, :, None, :]
        tab = jnp.where(jnp.asarray(ok)[None], bias, NEG_INF)
        tabs.append(tab.reshape(NA_HEADS, NA_BAND * GRID_W, NA_KROWS * GRID_W))
    return jnp.stack(tabs)


def _na_kernel(q_ref, k0_ref, k1_ref, k2_ref, k3_ref, v0_ref, v1_ref, v2_ref, v3_ref, qg_ref, kg_ref,
               tab_ref, o_ref):
    q = q_ref[...]
    k = jnp.concatenate([k0_ref[...], k1_ref[...], k2_ref[...], k3_ref[...]], axis=0)
    v = jnp.concatenate([v0_ref[...], v1_ref[...], v2_ref[...], v3_ref[...]], axis=0)
    for h in range(NA_HEADS):
        sl = slice(h * NA_HEAD, (h + 1) * NA_HEAD)
        qh = (_rms(q[:, sl]) * qg_ref[...] * (NA_HEAD ** -0.5)).astype(jnp.bfloat16)
        kh = (_rms(k[:, sl]) * kg_ref[...]).astype(jnp.bfloat16)
        s = lax.dot_general(qh, kh, (((1,), (1,)), ((), ())), preferred_element_type=jnp.float32)
        s = s + tab_ref[0, h]
        p = jnp.exp(s - jnp.max(s, axis=-1, keepdims=True))
        den = jnp.sum(p, axis=-1, keepdims=True)
        o = jnp.dot(p.astype(jnp.bfloat16), v[:, sl].astype(jnp.bfloat16), preferred_element_type=jnp.float32)
        o_ref[:, sl] = o / den


def na_mixer(q, k, v, n_seq, q_g, k_g, rel_bias):
    n = q.shape[0]
    rows = n // n_seq // GRID_W
    n_band = rows // NA_BAND
    tab = _na_bias_tables(rel_bias, rows)
    qtok = NA_BAND * GRID_W
    ktok = NA_KBLK * GRID_W
    kblk_per_seq = rows // NA_KBLK

    def kv_spec(j):
        def index(i, b):
            first = jnp.clip(NA_BAND // NA_KBLK * i - 1, 0, kblk_per_seq - NA_KROWS // NA_KBLK)
            return (b * kblk_per_seq + first + j, 0)
        return pl.BlockSpec((ktok, W_GROUP), index)

    def tab_index(i, b):
        return (jnp.where(i == 0, 0, jnp.where(i == n_band - 1, 2, 1)), 0, 0, 0)

    qspec = pl.BlockSpec((qtok, W_GROUP), lambda i, b: (b * n_band + i, 0))
    gspec = pl.BlockSpec((1, NA_HEAD), lambda i, b: (0, 0))
    kvs = [kv_spec(j) for j in range(NA_KROWS // NA_KBLK)]
    return pl.pallas_call(
        _na_kernel,
        grid=(n_band, n_seq),
        in_specs=[qspec] + kvs + kvs + [gspec, gspec,
                  pl.BlockSpec((1, NA_HEADS, qtok, NA_KROWS * GRID_W), tab_index)],
        out_specs=qspec,
        out_shape=jax.ShapeDtypeStruct((n, W_GROUP), jnp.float32),
        compiler_params=pltpu.CompilerParams(dimension_semantics=("arbitrary", "arbitrary"),
                                             vmem_limit_bytes=VMEM_LIMIT),
        name="na_attn",
    )(q, k, k, k, k, v, v, v, v, q_g.reshape(1, NA_HEAD), k_g.reshape(1, NA_HEAD), tab)


def _trunk(x, p):
    Bn, L, _ = x.shape
    n = Bn * L
    x = x.reshape(n, D_MODEL)
    bf = jnp.bfloat16
    for l in range(DEPTH):
        z_s5, z_hy, z_rw, z_q, z_k, z_v = in_proj(x, p['ln1_g'][l], p['w_in'][l].astype(bf))
        y_s5 = s5_mixer(z_s5, Bn, p['s5_lam_re'][l], p['s5_lam_im'][l], p['s5_log_dt'][l],
                        p['s5_b_re'][l], p['s5_b_im'][l], p['s5_c_re'][l], p['s5_c_im'][l], p['s5_d'][l],
                        p['s5_glu_w'][l], p['s5_glu_b'][l])
        y_hy = hyena_mixer(z_hy.reshape(Bn, L, -1), p['hy_conv_w'][l], p['hy_conv_b'][l], p['hy_f_w1'][l],
                           p['hy_f_b1'][l], p['hy_f_freq'][l], p['hy_f_w2'][l], p['hy_f_b2'][l],
                           p['hy_f_w3'][l], p['hy_log_rate'][l], p['hy_d'][l])
        y_rw = rwkv_mixer(z_rw, Bn, p['rw_mu'][l], p['rw_w0'][l], p['rw_w2'][l],
                          p['rw_a0'][l], p['rw_a2'][l], p['rw_g2'][l], p['rw_k_k'][l], p['rw_k_a'][l],
                          p['rw_r_k'][l], p['rw_ln_w'][l], p['rw_ln_b'][l])
        y_na = na_mixer(z_q, z_k, z_v, Bn, p['na_q_g'][l], p['na_k_g'][l], p['na_rel_bias'][l])
        ys = [y_s5, y_hy.reshape(n, W_GROUP), y_rw, y_na]
        x = out_proj(x, ys, p['grp_g'][l], p['w_out'][l].astype(bf))
        x = mlp(x, p['ln2_g'][l], p['w_mlp1'][l].astype(bf), p['w_mlp2'][l].astype(bf))
    return x.reshape(Bn, L, D_MODEL)


def kernel(x_prompt, x_sample, ln1_g, w_in, s5_lam_re, s5_lam_im, s5_log_dt, s5_b_re, s5_b_im,
           s5_c_re, s5_c_im, s5_d, s5_glu_w, s5_glu_b, hy_conv_w, hy_conv_b, hy_f_w1, hy_f_b1,
           hy_f_freq, hy_f_w2, hy_f_b2, hy_f_w3, hy_log_rate, hy_d, rw_mu, rw_w0, rw_w2, rw_a0,
           rw_a2, rw_g2, rw_k_k, rw_k_a, rw_r_k, rw_ln_w, rw_ln_b, na_q_g, na_k_g, na_rel_bias,
           grp_g, w_out, ln2_g, w_mlp1, w_mlp2):
    p = dict(ln1_g=ln1_g, w_in=w_in, s5_lam_re=s5_lam_re, s5_lam_im=s5_lam_im, s5_log_dt=s5_log_dt,
             s5_b_re=s5_b_re, s5_b_im=s5_b_im, s5_c_re=s5_c_re, s5_c_im=s5_c_im, s5_d=s5_d,
             s5_glu_w=s5_glu_w, s5_glu_b=s5_glu_b, hy_conv_w=hy_conv_w, hy_conv_b=hy_conv_b,
             hy_f_w1=hy_f_w1, hy_f_b1=hy_f_b1, hy_f_freq=hy_f_freq, hy_f_w2=hy_f_w2, hy_f_b2=hy_f_b2,
             hy_f_w3=hy_f_w3, hy_log_rate=hy_log_rate, hy_d=hy_d, rw_mu=rw_mu, rw_w0=rw_w0, rw_w2=rw_w2,
             rw_a0=rw_a0, rw_a2=rw_a2, rw_g2=rw_g2, rw_k_k=rw_k_k, rw_k_a=rw_k_a, rw_r_k=rw_r_k,
             rw_ln_w=rw_ln_w, rw_ln_b=rw_ln_b, na_q_g=na_q_g, na_k_g=na_k_g, na_rel_bias=na_rel_bias,
             grp_g=grp_g, w_out=w_out, ln2_g=ln2_g, w_mlp1=w_mlp1, w_mlp2=w_mlp2)
    nb = x_prompt.shape[0]
    y = _trunk(jnp.concatenate([x_prompt, x_sample], axis=0), p)
    return (y[:nb], y[nb:])
```

```python
import math

import jax
import jax.numpy as jnp
import numpy as np
from jax import lax
from jax.experimental import pallas as pl
from jax.experimental.pallas import tpu as pltpu

D_MODEL = 1024
DEPTH = 4
GRID_W = 64
W_GROUP = 256
N_MIXERS = 4
D_FF = 4 * D_MODEL
NORM_EPS = 1e-6

S5_CH = 16
S5_GROUPS = W_GROUP // S5_CH
S5_STATE = 64
S5_IN = W_GROUP

HY_ORDER = 2
HY_BANDS = 8
HY_IN = (HY_ORDER + 1) * W_GROUP

RW_HEAD = 64
RW_HEADS = W_GROUP // RW_HEAD
RW_DECAY_RANK = 64
RW_A_RANK = 64
RW_G_RANK = 128
RW_LN_EPS = 64e-5
RW_IN = 3 * W_GROUP + RW_DECAY_RANK + RW_A_RANK + RW_G_RANK
RW_SPLITS = (W_GROUP, 2 * W_GROUP, 3 * W_GROUP, 3 * W_GROUP + RW_DECAY_RANK,
             3 * W_GROUP + RW_DECAY_RANK + RW_A_RANK)

NA_HEAD = 64
NA_HEADS = W_GROUP // NA_HEAD
NA_WIN_R = 8
NA_WIN_C = 16
NEG_INF = -1e30

D_IN = S5_IN + HY_IN + RW_IN + 3 * W_GROUP

V7X_LANES = 128
VMEM_LIMIT = 48 * 1024 * 1024

ROW_TILE = 512
FF_TILE = 1024
SCAN_T = 16


def _rms(x):
    return x * lax.rsqrt(jnp.mean(x * x, axis=-1, keepdims=True) + NORM_EPS)


IN_WIDTHS = (S5_IN, HY_IN, RW_IN, W_GROUP, W_GROUP, W_GROUP)


def _in_proj_kernel(x_ref, g_ref, w_ref, *o_refs):
    h = (_rms(x_ref[...]) * g_ref[...]).astype(jnp.bfloat16)
    lo = 0
    for o_ref, width in zip(o_refs, IN_WIDTHS):
        o_ref[...] = jnp.dot(h, w_ref[:, lo:lo + width], preferred_element_type=jnp.float32)
        lo += width


def in_proj(x, g, w_bf16):
    n = x.shape[0]
    return pl.pallas_call(
        _in_proj_kernel,
        grid=(n // ROW_TILE,),
        in_specs=[pl.BlockSpec((ROW_TILE, D_MODEL), lambda i: (i, 0)),
                  pl.BlockSpec((1, D_MODEL), lambda i: (0, 0)),
                  pl.BlockSpec((D_MODEL, D_IN), lambda i: (0, 0))],
        out_specs=[pl.BlockSpec((ROW_TILE, w), lambda i: (i, 0)) for w in IN_WIDTHS],
        out_shape=[jax.ShapeDtypeStruct((n, w), jnp.float32) for w in IN_WIDTHS],
        compiler_params=pltpu.CompilerParams(dimension_semantics=("parallel",),
                                             vmem_limit_bytes=VMEM_LIMIT),
        name="in_proj",
    )(x, g.reshape(1, D_MODEL), w_bf16)


def _out_proj_kernel(x_ref, y0_ref, y1_ref, y2_ref, y3_ref, g_ref, w_ref, o_ref):
    acc = x_ref[...]
    for i, y_ref in enumerate((y0_ref, y1_ref, y2_ref, y3_ref)):
        n = (_rms(y_ref[...]) * g_ref[i:i + 1, :]).astype(jnp.bfloat16)
        acc = acc + jnp.dot(n, w_ref[i * W_GROUP:(i + 1) * W_GROUP, :],
                            preferred_element_type=jnp.float32)
    o_ref[...] = acc


def out_proj(x, ys, g, w_bf16):
    n = x.shape[0]
    row = lambda w: pl.BlockSpec((ROW_TILE, w), lambda i: (i, 0))
    return pl.pallas_call(
        _out_proj_kernel,
        grid=(n // ROW_TILE,),
        in_specs=[row(D_MODEL)] + [row(W_GROUP)] * N_MIXERS
                 + [pl.BlockSpec((N_MIXERS, W_GROUP), lambda i: (0, 0)),
                    pl.BlockSpec((D_MODEL, D_MODEL), lambda i: (0, 0))],
        out_specs=row(D_MODEL),
        out_shape=jax.ShapeDtypeStruct((n, D_MODEL), jnp.float32),
        compiler_params=pltpu.CompilerParams(dimension_semantics=("parallel",),
                                             vmem_limit_bytes=VMEM_LIMIT),
        name="out_proj",
    )(x, *ys, g, w_bf16)


def _mlp_kernel(x_ref, g_ref, w1_ref, w2_ref, o_ref, h_ref):
    j = pl.program_id(1)

    @pl.when(j == 0)
    def _():
        x = x_ref[...]
        h_ref[...] = (_rms(x) * g_ref[...]).astype(jnp.bfloat16)
        o_ref[...] = x

    a = jnp.dot(h_ref[...], w1_ref[...], preferred_element_type=jnp.float32)
    a = jnp.square(jnp.maximum(a, 0.0)).astype(jnp.bfloat16)
    o_ref[...] += jnp.dot(a, w2_ref[...], preferred_element_type=jnp.float32)


def mlp(x, g, w1_bf16, w2_bf16):
    n = x.shape[0]
    return pl.pallas_call(
        _mlp_kernel,
        grid=(n // ROW_TILE, D_FF // FF_TILE),
        in_specs=[pl.BlockSpec((ROW_TILE, D_MODEL), lambda i, j: (i, 0)),
                  pl.BlockSpec((1, D_MODEL), lambda i, j: (0, 0)),
                  pl.BlockSpec((D_MODEL, FF_TILE), lambda i, j: (0, j)),
                  pl.BlockSpec((FF_TILE, D_MODEL), lambda i, j: (j, 0))],
        out_specs=pl.BlockSpec((ROW_TILE, D_MODEL), lambda i, j: (i, 0)),
        out_shape=jax.ShapeDtypeStruct((n, D_MODEL), jnp.float32),
        scratch_shapes=[pltpu.VMEM((ROW_TILE, D_MODEL), jnp.bfloat16)],
        compiler_params=pltpu.CompilerParams(dimension_semantics=("parallel", "arbitrary"),
                                             vmem_limit_bytes=VMEM_LIMIT),
        name="mlp",
    )(x, g.reshape(1, D_MODEL), w1_bf16, w2_bf16)


def _head_sum(x):
    lane = lax.broadcasted_iota(jnp.int32, (W_GROUP, W_GROUP), 0) // RW_HEAD
    col = lax.broadcasted_iota(jnp.int32, (W_GROUP, W_GROUP), 1) // RW_HEAD
    ones = (lane == col).astype(jnp.float32)
    return jnp.dot(x, ones, precision=lax.Precision.HIGHEST, preferred_element_type=jnp.float32)


def _softplus(x):
    return jnp.maximum(x, 0.0) + jnp.log(1.0 + jnp.exp(-jnp.abs(x)))


def _shifted(z, prev_row, next_row):
    t = z.shape[0]
    row = lax.broadcasted_iota(jnp.int32, z.shape, 0)
    zp = jnp.where(row == 0, prev_row, pltpu.roll(z, 1, 0))
    zn = jnp.where(row == t - 1, next_row, pltpu.roll(z, t - 1, 0))
    return zp, zn


def _rwkv_prep_kernel(z_ref, zp_ref, zn_ref, mu_ref, w0_ref, w2_ref, a0_ref, a2_ref, g2_ref, kk_ref, ka_ref,
                      rk_ref, nkk_ref, r_ref, v_ref, dec0_ref, dec1_ref, kd0_ref, kd1_ref, b0_ref, b1_ref,
                      g_ref, bonus_ref):
    z = z_ref[...]
    zp, zn = _shifted(z, zp_ref[0], zn_ref[0])
    z = z + mu_ref[...] * (0.5 * (zp + zn) - z)
    r = z[:, 0:W_GROUP]
    k = z[:, W_GROUP:2 * W_GROUP]
    v = z[:, 2 * W_GROUP:3 * W_GROUP]
    wd = z[:, RW_SPLITS[2]:RW_SPLITS[3]]
    ad = z[:, RW_SPLITS[3]:RW_SPLITS[4]]
    gd = z[:, RW_SPLITS[4]:RW_IN]
    bf = jnp.bfloat16
    g_ref[...] = jnp.dot(jax.nn.sigmoid(gd).astype(bf), g2_ref[...], preferred_element_type=jnp.float32)
    kk = k * kk_ref[...]
    kk = kk / jnp.maximum(jnp.sqrt(_head_sum(kk * kk)), 1e-12)
    nkk_ref[...] = -kk
    r_ref[...] = r
    v_ref[...] = v
    bonus_ref[...] = _head_sum(r * k * rk_ref[...]) * v
    tw = jnp.tanh(wd).astype(bf)
    adb = ad.astype(bf)
    for d, (dec_ref, kd_ref, b_ref) in enumerate(((dec0_ref, kd0_ref, b0_ref), (dec1_ref, kd1_ref, b1_ref))):
        w = w0_ref[d:d + 1, :] + jnp.dot(tw, w2_ref[d], preferred_element_type=jnp.float32)
        w = -_softplus(-w) - 0.5
        dec_ref[...] = jnp.exp(-jnp.exp(w))
        a = jax.nn.sigmoid(a0_ref[d:d + 1, :] + jnp.dot(adb, a2_ref[d], preferred_element_type=jnp.float32))
        kd_ref[...] = k * (1.0 + (a - 1.0) * ka_ref[...])
        b_ref[...] = kk * a


def _halo_rows(z, n_seq, tile):
    n, c = z.shape
    zt = z.reshape(n_seq, n // n_seq // tile, tile, c)
    zero = jnp.zeros((n_seq, 1, c), z.dtype)
    prev = jnp.concatenate([zero, zt[:, :-1, -1]], axis=1).reshape(n // tile, 1, c)
    nxt = jnp.concatenate([zt[:, 1:, 0], zero], axis=1).reshape(n // tile, 1, c)
    return prev, nxt


def rwkv_prep(z, n_seq, mu, w0, w2, a0, a2, g2, k_k, k_a, r_k):
    n = z.shape[0]
    prev, nxt = _halo_rows(z, n_seq, ROW_TILE)
    bf = jnp.bfloat16
    full = lambda *s: pl.BlockSpec(s, lambda i: (0,) * len(s))
    row = pl.BlockSpec((ROW_TILE, W_GROUP), lambda i: (i, 0))
    halo = pl.BlockSpec((1, 1, RW_IN), lambda i: (i, 0, 0))
    vec = lambda x: x.reshape(1, -1)
    return pl.pallas_call(
        _rwkv_prep_kernel,
        grid=(n // ROW_TILE,),
        in_specs=[pl.BlockSpec((ROW_TILE, RW_IN), lambda i: (i, 0)), halo, halo, full(1, RW_IN),
                  full(2, W_GROUP), full(2, RW_DECAY_RANK, W_GROUP), full(2, W_GROUP),
                  full(2, RW_A_RANK, W_GROUP), full(RW_G_RANK, W_GROUP), full(1, W_GROUP), full(1, W_GROUP),
                  full(1, W_GROUP)],
        out_specs=[row] * 11,
        out_shape=[jax.ShapeDtypeStruct((n, W_GROUP), jnp.float32)] * 11,
        compiler_params=pltpu.CompilerParams(dimension_semantics=("parallel",),
                                             vmem_limit_bytes=VMEM_LIMIT),
        name="rwkv_prep",
    )(z, prev, nxt, vec(mu), w0, w2.astype(bf), a0, a2.astype(bf), g2.astype(bf), vec(k_k), vec(k_a), vec(r_k))


def _rwkv_scan_kernel(af_ref, wf_ref, bf_ref, kf_ref, rf_ref, vf_ref, ab_ref, wb_ref, bb_ref, kb_ref, rb_ref,
                      vb_ref, yf_ref, yb_ref, s_ref):
    @pl.when(pl.program_id(0) == 0)
    def _():
        s_ref[...] = jnp.zeros_like(s_ref)

    dirs = ((af_ref, wf_ref, bf_ref, kf_ref, rf_ref, vf_ref, yf_ref),
            (ab_ref, wb_ref, bb_ref, kb_ref, rb_ref, vb_ref, yb_ref))

    def step(i, carry):
        for j, (a_ref, w_ref, b_ref, k_ref, r_ref, v_ref, y_ref) in enumerate(dirs):
            t = i if j == 0 else SCAN_T - 1 - i
            vt = v_ref[t]
            sa = s_ref[j, 0] * a_ref[t, 0:1, :]
            for k in range(1, RW_HEAD):
                sa = sa + s_ref[j, k] * a_ref[t, k:k + 1, :]
            y = None
            for k in range(RW_HEAD):
                s = s_ref[j, k] * w_ref[t, k:k + 1, :] + sa * b_ref[t, k:k + 1, :] + vt * k_ref[t, k:k + 1, :]
                s_ref[j, k] = s
                yk = s * r_ref[t, k:k + 1, :]
                y = yk if y is None else y + yk
            y_ref[t] = y
        return carry

    lax.fori_loop(0, SCAN_T, step, 0)


def rwkv_scan(a, r, v, w_f, b_f, k_f, w_b, b_b, k_b):
    L = a.shape[0]
    nblk = L // SCAN_T
    fwd = pl.BlockSpec((SCAN_T, RW_HEAD, V7X_LANES), lambda i: (i, 0, 0))
    bwd = pl.BlockSpec((SCAN_T, RW_HEAD, V7X_LANES), lambda i: (nblk - 1 - i, 0, 0))
    out = jax.ShapeDtypeStruct((L, RW_HEAD, V7X_LANES), jnp.float32)
    return pl.pallas_call(
        _rwkv_scan_kernel,
        grid=(nblk,),
        in_specs=[fwd] * 6 + [bwd] * 6,
        out_specs=[fwd, bwd],
        out_shape=[out, out],
        scratch_shapes=[pltpu.VMEM((2, RW_HEAD, RW_HEAD, V7X_LANES), jnp.float32)],
        compiler_params=pltpu.CompilerParams(dimension_semantics=("arbitrary",),
                                             vmem_limit_bytes=VMEM_LIMIT),
        name="rwkv_scan",
    )(a, w_f, b_f, k_f, r, v, a, w_b, b_b, k_b, r, v)


def _rwkv_post_kernel(yf_ref, yb_ref, bonus_ref, g_ref, lw_ref, lb_ref, o_ref):
    y = yf_ref[...] + yb_ref[...]
    mean = _head_sum(y) * (1.0 / RW_HEAD)
    c = y - mean
    var = _head_sum(c * c) * (1.0 / RW_HEAD)
    y = c * lax.rsqrt(var + RW_LN_EPS) * lw_ref[...] + lb_ref[...]
    o_ref[...] = (y + bonus_ref[...]) * g_ref[...]


def rwkv_post(y_f, y_b, bonus, g, ln_w, ln_b):
    n = y_f.shape[0]
    row = pl.BlockSpec((ROW_TILE, W_GROUP), lambda i: (i, 0))
    vec = pl.BlockSpec((1, W_GROUP), lambda i: (0, 0))
    return pl.pallas_call(
        _rwkv_post_kernel,
        grid=(n // ROW_TILE,),
        in_specs=[row, row, row, row, vec, vec],
        out_specs=row,
        out_shape=jax.ShapeDtypeStruct((n, W_GROUP), jnp.float32),
        compiler_params=pltpu.CompilerParams(dimension_semantics=("parallel",),
                                             vmem_limit_bytes=VMEM_LIMIT),
        name="rwkv_post",
    )(y_f, y_b, bonus, g, ln_w.reshape(1, W_GROUP), ln_b.reshape(1, W_GROUP))


def rwkv_mixer(z, n_seq, mu, w0, w2, a0, a2, g2, k_k, k_a, r_k, ln_w, ln_b):
    n = z.shape[0]
    L = n // n_seq
    n_chain = n_seq * RW_HEADS
    assert n_chain <= V7X_LANES
    nkk, r, v, dec0, dec1, kd0, kd1, b0, b1, g, bonus = rwkv_prep(z, n_seq, mu, w0, w2, a0, a2, g2, k_k, k_a, r_k)

    def chains(x):
        x = x.reshape(n_seq, L, RW_HEADS, RW_HEAD).transpose(1, 3, 0, 2).reshape(L, RW_HEAD, n_chain)
        return jnp.pad(x, ((0, 0), (0, 0), (0, V7X_LANES - n_chain)))

    def tokens(y):
        return y[..., :n_chain].reshape(L, RW_HEAD, n_seq, RW_HEADS).transpose(2, 0, 3, 1).reshape(n, W_GROUP)

    y_f, y_b = rwkv_scan(chains(nkk), chains(r), chains(v), chains(dec0), chains(b0), chains(kd0),
                         chains(dec1), chains(b1), chains(kd1))
    return rwkv_post(tokens(y_f), tokens(y_b), bonus, g, ln_w, ln_b)


S5_NSTATE = S5_GROUPS * S5_STATE
S5_T = 512


def _cmul(ar, ai, br, bi):
    return ar * br - ai * bi, ar * bi + ai * br


def _s5_operators(lam_re, lam_im, log_dt, b_re, b_im, c_re, c_im):
    dt = jnp.exp(log_dt)[..., None]
    mag = jnp.exp(lam_re * dt)
    ab_re = mag * jnp.cos(lam_im * dt)
    ab_im = mag * jnp.sin(lam_im * dt)
    den = lam_re * lam_re + lam_im * lam_im
    n_re = ab_re - 1.0
    f_re = (n_re * lam_re + ab_im * lam_im) / den
    f_im = (ab_im * lam_re - n_re * lam_im) / den
    bb_re, bb_im = _cmul(f_re[..., None], f_im[..., None], b_re, b_im)
    eye = jnp.eye(S5_GROUPS, dtype=jnp.float32)

    def in_map(bb):
        return jnp.einsum('gh,dhnc->dgchn', eye, bb).reshape(2, W_GROUP, S5_NSTATE)

    def out_map(c):
        return jnp.einsum('hg,dgcn->dhngc', eye, c).reshape(2, S5_NSTATE, W_GROUP)

    bmat = jnp.concatenate([in_map(bb_re), in_map(bb_im)], axis=-1)
    cmat = jnp.concatenate([out_map(c_re), -out_map(c_im)], axis=1)
    lam = jnp.stack([ab_re.reshape(2, S5_NSTATE), ab_im.reshape(2, S5_NSTATE)], axis=1)
    return lam, bmat.astype(jnp.bfloat16), cmat.astype(jnp.bfloat16)


def _s5_scan_kernel(u_ref, lam_ref, bmat_ref, cmat_ref, y_ref, st_ref, carry_ref):
    d = pl.program_id(1)

    @pl.when(pl.program_id(2) == 0)
    def _():
        carry_ref[...] = jnp.zeros_like(carry_ref)

    st_ref[...] = jnp.dot(u_ref[...].astype(jnp.bfloat16), bmat_ref[0], preferred_element_type=jnp.float32)
    lam_r = lam_ref[0, 0:1, :]
    lam_i = lam_ref[0, 1:2, :]
    re = slice(0, S5_NSTATE)
    im = slice(S5_NSTATE, 2 * S5_NSTATE)

    def step(i, carry):
        sr, si = carry
        t = i + d * (S5_T - 1 - 2 * i)
        nr = lam_r * sr - lam_i * si + st_ref[pl.ds(t, 1), re]
        ni = lam_r * si + lam_i * sr + st_ref[pl.ds(t, 1), im]
        st_ref[pl.ds(t, 1), re] = nr
        st_ref[pl.ds(t, 1), im] = ni
        return nr, ni

    sr, si = lax.fori_loop(0, S5_T, step, (carry_ref[0:1, re], carry_ref[0:1, im]), unroll=4)
    carry_ref[0:1, re] = sr
    carry_ref[0:1, im] = si
    y_ref[0] = jnp.dot(st_ref[...].astype(jnp.bfloat16), cmat_ref[0], preferred_element_type=jnp.float32)


def s5_scan(z, lam, bmat, cmat, n_seq):
    n = z.shape[0]
    nch = n // n_seq // S5_T

    def row_block(b, d, c):
        return b * nch + c + d * (nch - 1 - 2 * c)

    return pl.pallas_call(
        _s5_scan_kernel,
        grid=(n_seq, 2, nch),
        in_specs=[pl.BlockSpec((S5_T, W_GROUP), lambda b, d, c: (row_block(b, d, c), 0)),
                  pl.BlockSpec((1, 2, S5_NSTATE), lambda b, d, c: (d, 0, 0)),
                  pl.BlockSpec((1, W_GROUP, 2 * S5_NSTATE), lambda b, d, c: (d, 0, 0)),
                  pl.BlockSpec((1, 2 * S5_NSTATE, W_GROUP), lambda b, d, c: (d, 0, 0))],
        out_specs=pl.BlockSpec((1, S5_T, W_GROUP), lambda b, d, c: (d, row_block(b, d, c), 0)),
        out_shape=jax.ShapeDtypeStruct((2, n, W_GROUP), jnp.float32),
        scratch_shapes=[pltpu.VMEM((S5_T, 2 * S5_NSTATE), jnp.float32),
                        pltpu.VMEM((8, 2 * S5_NSTATE), jnp.float32)],
        compiler_params=pltpu.CompilerParams(dimension_semantics=("arbitrary", "arbitrary", "arbitrary"),
                                             vmem_limit_bytes=VMEM_LIMIT),
        name="s5_scan",
    )(z, lam, bmat, cmat)


def _s5_finish_kernel(y_ref, z_ref, d_ref, w_ref, b_ref, o_ref):
    y = y_ref[0] + y_ref[1] + d_ref[...] * z_ref[...]
    g = jax.nn.gelu(y)
    gate = jnp.dot(g.astype(jnp.bfloat16), w_ref[...], preferred_element_type=jnp.float32) + b_ref[...]
    o_ref[...] = g * jax.nn.sigmoid(gate)


def s5_finish(y2, z, d, glu_w_bf16, glu_b):
    n = z.shape[0]
    vec = pl.BlockSpec((1, W_GROUP), lambda i: (0, 0))
    return pl.pallas_call(
        _s5_finish_kernel,
        grid=(n // ROW_TILE,),
        in_specs=[pl.BlockSpec((2, ROW_TILE, W_GROUP), lambda i: (0, i, 0)),
                  pl.BlockSpec((ROW_TILE, W_GROUP), lambda i: (i, 0)),
                  vec, pl.BlockSpec((W_GROUP, W_GROUP), lambda i: (0, 0)), vec],
        out_specs=pl.BlockSpec((ROW_TILE, W_GROUP), lambda i: (i, 0)),
        out_shape=jax.ShapeDtypeStruct((n, W_GROUP), jnp.float32),
        compiler_params=pltpu.CompilerParams(dimension_semantics=("parallel",),
                                             vmem_limit_bytes=VMEM_LIMIT),
        name="s5_finish",
    )(y2, z, d.reshape(1, W_GROUP), glu_w_bf16, glu_b.reshape(1, W_GROUP))


def s5_mixer(z, n_seq, lam_re, lam_im, log_dt, b_re, b_im, c_re, c_im, d, glu_w, glu_b):
    lam, bmat, cmat = _s5_operators(lam_re, lam_im, log_dt, b_re, b_im, c_re, c_im)
    y2 = s5_scan(z, lam, bmat, cmat, n_seq)
    return s5_finish(y2, z, d, glu_w.astype(jnp.bfloat16), glu_b)


HY_L = 4096
HY_N = 2 * HY_L
HY_N1 = 64
HY_N2 = 128
HY_K1 = HY_N1 // 2 + 1
HY_KB = 11
HY_KP = 24
HY_C = V7X_LANES


def _hyena_dft_tables():
    n1 = np.arange(HY_N1 // 2)
    k1 = np.arange(HY_K1)
    ang1 = 2.0 * np.pi * np.outer(k1, n1) / HY_N1
    weight = np.where((k1 == 0) | (k1 == HY_N1 // 2), 1.0, 2.0) / HY_N
    nblk = HY_K1 // HY_KB
    fa = np.zeros((nblk, HY_KP, HY_N1 // 2), np.float32)
    fi = np.zeros((nblk, HY_N1 // 2, HY_KP), np.float32)
    for b in range(nblk):
        sl = slice(b * HY_KB, (b + 1) * HY_KB)
        fa[b, :HY_KB] = np.cos(ang1[sl])
        fa[b, HY_KB:2 * HY_KB] = -np.sin(ang1[sl])
        fi[b, :, :HY_KB] = (np.cos(ang1[sl]) * weight[sl, None]).T
        fi[b, :, HY_KB:2 * HY_KB] = (-np.sin(ang1[sl]) * weight[sl, None]).T
    n2 = np.arange(HY_N2)
    k = k1[:, None] + HY_N1 * np.arange(HY_N2)[None, :]
    ang = 2.0 * np.pi * (k[:, :, None] * n2[None, None, :] % HY_N) / HY_N
    c, s = np.cos(ang), np.sin(ang)
    g = np.concatenate([np.concatenate([c, s], axis=2), np.concatenate([-s, c], axis=2)], axis=1)
    gi = np.transpose(g, (0, 2, 1))
    bf = jnp.bfloat16
    return (jnp.asarray(fa, bf), jnp.asarray(fi, bf), jnp.asarray(g, bf), jnp.asarray(gi, bf))


def _hyena_filter_spectra(w1, b1, freq, w2, b2, w3, log_rate):
    L = HY_L
    t = jnp.arange(L, dtype=jnp.float32) / L
    ang = 2.0 * math.pi * t[:, None] * jnp.arange(1, HY_BANDS + 1, dtype=jnp.float32)
    feats = jnp.concatenate([t[:, None], jnp.sin(ang), jnp.cos(ang)], axis=-1)
    h = jnp.sin(freq[0] * (feats @ w1 + b1))
    h = jnp.sin(freq[1] * (h @ w2 + b2))
    h = (h @ w3).reshape(L, 2, HY_ORDER, W_GROUP)
    h = h * jnp.exp(-jnp.exp(log_rate)[None] * t[:, None, None, None])
    fwd, bwd = h[:, 0], h[:, 1]
    k = jnp.concatenate([fwd, jnp.zeros_like(fwd[:1]), bwd[:0:-1]], axis=0)
    k = k / jnp.sum(jnp.abs(k), axis=0, keepdims=True)
    kf = jnp.fft.fft(k, axis=0).reshape(HY_N2, HY_N1, HY_ORDER, W_GROUP)[:, :HY_K1]
    kf = jnp.transpose(kf, (2, 1, 0, 3))
    return jnp.stack([jnp.real(kf), jnp.imag(kf)], axis=2).astype(jnp.float32)


def _hy_pre_kernel(z_ref, zp_ref, zn_ref, w_ref, b_ref, v_ref, x1_ref, x2_ref):
    z = z_ref[...]
    zp, zn = _shifted(z, zp_ref[0], zn_ref[0])
    z = w_ref[0:1, :] * zp + w_ref[1:2, :] * z + w_ref[2:3, :] * zn + b_ref[...]
    v_ref[...] = z[:, 0:W_GROUP]
    x1_ref[...] = z[:, W_GROUP:2 * W_GROUP]
    x2_ref[...] = z[:, 2 * W_GROUP:3 * W_GROUP]


def hy_pre(z, n_seq, conv_w, conv_b):
    n = z.shape[0]
    prev, nxt = _halo_rows(z, n_seq, ROW_TILE)
    row = pl.BlockSpec((ROW_TILE, W_GROUP), lambda i: (i, 0))
    halo = pl.BlockSpec((1, 1, HY_IN), lambda i: (i, 0, 0))
    return pl.pallas_call(
        _hy_pre_kernel,
        grid=(n // ROW_TILE,),
        in_specs=[pl.BlockSpec((ROW_TILE, HY_IN), lambda i: (i, 0)), halo, halo,
                  pl.BlockSpec((3, HY_IN), lambda i: (0, 0)), pl.BlockSpec((1, HY_IN), lambda i: (0, 0))],
        out_specs=[row] * 3,
        out_shape=[jax.ShapeDtypeStruct((n, W_GROUP), jnp.float32)] * 3,
        compiler_params=pltpu.CompilerParams(dimension_semantics=("parallel",),
                                             vmem_limit_bytes=VMEM_LIMIT),
        name="hy_pre",
    )(z, prev, nxt, conv_w, conv_b.reshape(1, HY_IN))


def _hy_conv_kernel(u_ref, fa_ref, fi_ref, g_ref, gi_ref, kf_ref, o_ref, as_ref, bs_ref):
    kb = pl.program_id(2)
    bf = jnp.bfloat16
    fa = fa_ref[0]
    fi = fi_ref[0]
    for n2 in range(HY_N2):
        rows = u_ref[pl.ds(n2, HY_N1 // 2, stride=HY_N2), :].astype(bf)
        as_ref[n2] = jnp.dot(fa, rows, preferred_element_type=jnp.float32)
    bs_ref[:, 2 * HY_KB:, :] = jnp.zeros((HY_N2, HY_KP - 2 * HY_KB, HY_C), jnp.float32)
    for j in range(HY_KB):
        a = jnp.concatenate([as_ref[:, j, :], as_ref[:, HY_KB + j, :]], axis=0).astype(bf)
        x = jnp.dot(g_ref[j], a, preferred_element_type=jnp.float32)
        xr, xi = x[:HY_N2], x[HY_N2:]
        kr, ki = kf_ref[0, j, 0], kf_ref[0, j, 1]
        y = jnp.concatenate([xr * kr - xi * ki, xr * ki + xi * kr], axis=0).astype(bf)
        b = jnp.dot(gi_ref[j], y, preferred_element_type=jnp.float32)
        bs_ref[:, j, :] = b[:HY_N2]
        bs_ref[:, HY_KB + j, :] = b[HY_N2:]
    @pl.when(kb == 0)
    def _():
        o_ref[...] = jnp.zeros_like(o_ref)

    for n2 in range(HY_N2):
        part = jnp.dot(fi, bs_ref[n2].astype(bf), preferred_element_type=jnp.float32)
        dst = pl.ds(n2, HY_N1 // 2, stride=HY_N2)
        o_ref[dst, :] = o_ref[dst, :] + part


def hy_conv(u, n_seq, kf, tables):
    n = u.shape[0]
    assert n // n_seq == HY_L
    fa, fi, g, gi = tables
    nblk = HY_K1 // HY_KB
    seq = pl.BlockSpec((HY_L, HY_C), lambda b, c, k: (b, c))
    return pl.pallas_call(
        _hy_conv_kernel,
        grid=(n_seq, W_GROUP // HY_C, nblk),
        in_specs=[seq,
                  pl.BlockSpec((1, HY_KP, HY_N1 // 2), lambda b, c, k: (k, 0, 0)),
                  pl.BlockSpec((1, HY_N1 // 2, HY_KP), lambda b, c, k: (k, 0, 0)),
                  pl.BlockSpec((HY_KB, 2 * HY_N2, 2 * HY_N2), lambda b, c, k: (k, 0, 0)),
                  pl.BlockSpec((HY_KB, 2 * HY_N2, 2 * HY_N2), lambda b, c, k: (k, 0, 0)),
                  pl.BlockSpec((1, HY_KB, 2, HY_N2, HY_C), lambda b, c, k: (0, k, 0, 0, c))],
        out_specs=seq,
        out_shape=jax.ShapeDtypeStruct((n, W_GROUP), jnp.float32),
        scratch_shapes=[pltpu.VMEM((HY_N2, HY_KP, HY_C), jnp.float32),
                        pltpu.VMEM((HY_N2, HY_KP, HY_C), jnp.float32)],
        compiler_params=pltpu.CompilerParams(dimension_semantics=("arbitrary", "arbitrary", "arbitrary"),
                                             vmem_limit_bytes=VMEM_LIMIT),
        name="hy_conv",
    )(u, fa, fi, g, gi, kf)


def _hy_gate_kernel(c_ref, u_ref, x_ref, d_ref, o_ref):
    o_ref[...] = x_ref[...] * (c_ref[...] + d_ref[...] * u_ref[...])


def hy_gate(c, u, gate, d):
    n = u.shape[0]
    row = pl.BlockSpec((ROW_TILE, W_GROUP), lambda i: (i, 0))
    return pl.pallas_call(
        _hy_gate_kernel,
        grid=(n // ROW_TILE,),
        in_specs=[row, row, row, pl.BlockSpec((1, W_GROUP), lambda i: (0, 0))],
        out_specs=row,
        out_shape=jax.ShapeDtypeStruct((n, W_GROUP), jnp.float32),
        compiler_params=pltpu.CompilerParams(dimension_semantics=("parallel",),
                                             vmem_limit_bytes=VMEM_LIMIT),
        name="hy_gate",
    )(c, u, gate, d.reshape(1, W_GROUP))


def hyena_mixer(z, n_seq, conv_w, conv_b, f_w1, f_b1, f_freq, f_w2, f_b2, f_w3, log_rate, d):
    v, x1, x2 = hy_pre(z, n_seq, conv_w, conv_b)
    kf = _hyena_filter_spectra(f_w1, f_b1, f_freq, f_w2, f_b2, f_w3, log_rate)
    tables = _hyena_dft_tables()
    u = v
    for o, gate in enumerate((x1, x2)):
        u = hy_gate(hy_conv(u, n_seq, kf[o:o + 1], tables), u, gate, d[o])
    return u


NA_BAND = 8
NA_KROWS = 2 * NA_BAND
NA_KBLK = 4


def _na_bias_tables(rel_bias, rows):
    hi = lax.Precision.HIGHEST
    qc = np.arange(GRID_W)
    ws = np.clip(qc - NA_WIN_C // 2, 0, GRID_W - NA_WIN_C)
    col_ok = (qc[None, :] >= ws[:, None]) & (qc[None, :] < ws[:, None] + NA_WIN_C)
    dc = np.clip(qc[None, :] - qc[:, None] + NA_WIN_C - 1, 0, 2 * NA_WIN_C - 2)
    dc_sel = (dc[..., None] == np.arange(2 * NA_WIN_C - 1)).astype(np.float32)
    col_bias = jnp.einsum('qkc,hdc->hdqk', dc_sel, rel_bias, precision=hi)
    tabs = []
    for band in (0, 1, rows // NA_BAND - 1):
        kb = int(np.clip(NA_BAND * band - NA_WIN_R // 2, 0, rows - NA_KROWS))
        r = NA_BAND * band + np.arange(NA_BAND)
        rs = np.clip(r - NA_WIN_R // 2, 0, rows - NA_WIN_R)
        krow = kb + np.arange(NA_KROWS)
        row_ok = (krow[None, :] >= rs[:, None]) & (krow[None, :] < rs[:, None] + NA_WIN_R)
        dr = np.clip(krow[None, :] - r[:, None] + NA_WIN_R - 1, 0, 2 * NA_WIN_R - 2)
        dr_sel = (dr[..., None] == np.arange(2 * NA_WIN_R - 1)).astype(np.float32)
        bias = jnp.einsum('jwd,hdqk->hjqwk', dr_sel, col_bias, precision=hi)
        ok = row_ok[:, None, :, None] & col_ok[None, :, None, :]
        tab = jnp.where(jnp.asarray(ok)[None], bias, NEG_INF)
        tabs.append(tab.reshape(NA_HEADS, NA_BAND * GRID_W, NA_KROWS * GRID_W))
    return jnp.stack(tabs)


def _na_kernel(q_ref, k0_ref, k1_ref, k2_ref, k3_ref, v0_ref, v1_ref, v2_ref, v3_ref, qg_ref, kg_ref,
               tab_ref, o_ref):
    q = q_ref[...]
    k = jnp.concatenate([k0_ref[...], k1_ref[...], k2_ref[...], k3_ref[...]], axis=0)
    v = jnp.concatenate([v0_ref[...], v1_ref[...], v2_ref[...], v3_ref[...]], axis=0)
    for h in range(NA_HEADS):
        sl = slice(h * NA_HEAD, (h + 1) * NA_HEAD)
        qh = (_rms(q[:, sl]) * qg_ref[...] * (NA_HEAD ** -0.5)).astype(jnp.bfloat16)
        kh = (_rms(k[:, sl]) * kg_ref[...]).astype(jnp.bfloat16)
        s = lax.dot_general(qh, kh, (((1,), (1,)), ((), ())), preferred_element_type=jnp.float32)
        s = s + tab_ref[0, h]
        p = jnp.exp(s - jnp.max(s, axis=-1, keepdims=True))
        den = jnp.sum(p, axis=-1, keepdims=True)
        o = jnp.dot(p.astype(jnp.bfloat16), v[:, sl].astype(jnp.bfloat16), preferred_element_type=jnp.float32)
        o_ref[:, sl] = o / den


def na_mixer(q, k, v, n_seq, q_g, k_g, rel_bias):
    n = q.shape[0]
    rows = n // n_seq // GRID_W
    n_band = rows // NA_BAND
    tab = _na_bias_tables(rel_bias, rows)
    qtok = NA_BAND * GRID_W
    ktok = NA_KBLK * GRID_W
    kblk_per_seq = rows // NA_KBLK

    def kv_spec(j):
        def index(i, b):
            first = jnp.clip(NA_BAND // NA_KBLK * i - 1, 0, kblk_per_seq - NA_KROWS // NA_KBLK)
            return (b * kblk_per_seq + first + j, 0)
        return pl.BlockSpec((ktok, W_GROUP), index)

    def tab_index(i, b):
        return (jnp.where(i == 0, 0, jnp.where(i == n_band - 1, 2, 1)), 0, 0, 0)

    qspec = pl.BlockSpec((qtok, W_GROUP), lambda i, b: (b * n_band + i, 0))
    gspec = pl.BlockSpec((1, NA_HEAD), lambda i, b: (0, 0))
    kvs = [kv_spec(j) for j in range(NA_KROWS // NA_KBLK)]
    return pl.pallas_call(
        _na_kernel,
        grid=(n_band, n_seq),
        in_specs=[qspec] + kvs + kvs + [gspec, gspec,
                  pl.BlockSpec((1, NA_HEADS, qtok, NA_KROWS * GRID_W), tab_index)],
        out_specs=qspec,
        out_shape=jax.ShapeDtypeStruct((n, W_GROUP), jnp.float32),
        compiler_params=pltpu.CompilerParams(dimension_semantics=("arbitrary", "arbitrary"),
                                             vmem_limit_bytes=VMEM_LIMIT),
        name="na_attn",
    )(q, k, k, k, k, v, v, v, v, q_g.reshape(1, NA_HEAD), k_g.reshape(1, NA_HEAD), tab)


def _trunk(x, p):
    Bn, L, _ = x.shape
    n = Bn * L
    x = x.reshape(n, D_MODEL)
    bf = jnp.bfloat16
    for l in range(DEPTH):
        z_s5, z_hy, z_rw, z_q, z_k, z_v = in_proj(x, p['ln1_g'][l], p['w_in'][l].astype(bf))
        y_s5 = s5_mixer(z_s5, Bn, p['s5_lam_re'][l], p['s5_lam_im'][l], p['s5_log_dt'][l],
                        p['s5_b_re'][l], p['s5_b_im'][l], p['s5_c_re'][l], p['s5_c_im'][l], p['s5_d'][l],
                        p['s5_glu_w'][l], p['s5_glu_b'][l])
        y_hy = hyena_mixer(z_hy, Bn, p['hy_conv_w'][l], p['hy_conv_b'][l], p['hy_f_w1'][l],
                           p['hy_f_b1'][l], p['hy_f_freq'][l], p['hy_f_w2'][l], p['hy_f_b2'][l],
                           p['hy_f_w3'][l], p['hy_log_rate'][l], p['hy_d'][l])
        y_rw = rwkv_mixer(z_rw, Bn, p['rw_mu'][l], p['rw_w0'][l], p['rw_w2'][l],
                          p['rw_a0'][l], p['rw_a2'][l], p['rw_g2'][l], p['rw_k_k'][l], p['rw_k_a'][l],
                          p['rw_r_k'][l], p['rw_ln_w'][l], p['rw_ln_b'][l])
        y_na = na_mixer(z_q, z_k, z_v, Bn, p['na_q_g'][l], p['na_k_g'][l], p['na_rel_bias'][l])
        ys = [y_s5, y_hy, y_rw, y_na]
        x = out_proj(x, ys, p['grp_g'][l], p['w_out'][l].astype(bf))
        x = mlp(x, p['ln2_g'][l], p['w_mlp1'][l].astype(bf), p['w_mlp2'][l].astype(bf))
    return x.reshape(Bn, L, D_MODEL)


def kernel(x_prompt, x_sample, ln1_g, w_in, s5_lam_re, s5_lam_im, s5_log_dt, s5_b_re, s5_b_im,
           s5_c_re, s5_c_im, s5_d, s5_glu_w, s5_glu_b, hy_conv_w, hy_conv_b, hy_f_w1, hy_f_b1,
           hy_f_freq, hy_f_w2, hy_f_b2, hy_f_w3, hy_log_rate, hy_d, rw_mu, rw_w0, rw_w2, rw_a0,
           rw_a2, rw_g2, rw_k_k, rw_k_a, rw_r_k, rw_ln_w, rw_ln_b, na_q_g, na_k_g, na_rel_bias,
           grp_g, w_out, ln2_g, w_mlp1, w_mlp2):
    p = dict(ln1_g=ln1_g, w_in=w_in, s5_lam_re=s5_lam_re, s5_lam_im=s5_lam_im, s5_log_dt=s5_log_dt,
             s5_b_re=s5_b_re, s5_b_im=s5_b_im, s5_c_re=s5_c_re, s5_c_im=s5_c_im, s5_d=s5_d,
             s5_glu_w=s5_glu_w, s5_glu_b=s5_glu_b, hy_conv_w=hy_conv_w, hy_conv_b=hy_conv_b,
             hy_f_w1=hy_f_w1, hy_f_b1=hy_f_b1, hy_f_freq=hy_f_freq, hy_f_w2=hy_f_w2, hy_f_b2=hy_f_b2,
             hy_f_w3=hy_f_w3, hy_log_rate=hy_log_rate, hy_d=hy_d, rw_mu=rw_mu, rw_w0=rw_w0, rw_w2=rw_w2,
             rw_a0=rw_a0, rw_a2=rw_a2, rw_g2=rw_g2, rw_k_k=rw_k_k, rw_k_a=rw_k_a, rw_r_k=rw_r_k,
             rw_ln_w=rw_ln_w, rw_ln_b=rw_ln_b, na_q_g=na_q_g, na_k_g=na_k_g, na_rel_bias=na_rel_bias,
             grp_g=grp_g, w_out=w_out, ln2_g=ln2_g, w_mlp1=w_mlp1, w_mlp2=w_mlp2)
    nb = x_prompt.shape[0]
    y = _trunk(jnp.concatenate([x_prompt, x_sample], axis=0), p)
    return (y[:nb], y[nb:])
```

```python
import math

import jax
import jax.numpy as jnp
import numpy as np
from jax import lax
from jax.experimental import pallas as pl
from jax.experimental.pallas import tpu as pltpu

D_MODEL = 1024
DEPTH = 4
GRID_W = 64
W_GROUP = 256
N_MIXERS = 4
D_FF = 4 * D_MODEL
NORM_EPS = 1e-6

S5_CH = 16
S5_GROUPS = W_GROUP // S5_CH
S5_STATE = 64
S5_IN = W_GROUP

HY_ORDER = 2
HY_BANDS = 8
HY_IN = (HY_ORDER + 1) * W_GROUP

RW_HEAD = 64
RW_HEADS = W_GROUP // RW_HEAD
RW_DECAY_RANK = 64
RW_A_RANK = 64
RW_G_RANK = 128
RW_LN_EPS = 64e-5
RW_IN = 3 * W_GROUP + RW_DECAY_RANK + RW_A_RANK + RW_G_RANK
RW_SPLITS = (W_GROUP, 2 * W_GROUP, 3 * W_GROUP, 3 * W_GROUP + RW_DECAY_RANK,
             3 * W_GROUP + RW_DECAY_RANK + RW_A_RANK)

NA_HEAD = 64
NA_HEADS = W_GROUP // NA_HEAD
NA_WIN_R = 8
NA_WIN_C = 16
NEG_INF = -1e30

D_IN = S5_IN + HY_IN + RW_IN + 3 * W_GROUP

V7X_LANES = 128
VMEM_LIMIT = 48 * 1024 * 1024

ROW_TILE = 512
FF_TILE = 1024
SCAN_T = 16


def _rms(x):
    return x * lax.rsqrt(jnp.mean(x * x, axis=-1, keepdims=True) + NORM_EPS)


IN_WIDTHS = (S5_IN, HY_IN, RW_IN, W_GROUP, W_GROUP, W_GROUP)


def _in_proj_kernel(x_ref, g_ref, w_ref, *o_refs):
    h = (_rms(x_ref[...]) * g_ref[...]).astype(jnp.bfloat16)
    lo = 0
    for o_ref, width in zip(o_refs, IN_WIDTHS):
        o_ref[...] = jnp.dot(h, w_ref[:, lo:lo + width], preferred_element_type=jnp.float32)
        lo += width


def in_proj(x, g, w_bf16):
    n = x.shape[0]
    return pl.pallas_call(
        _in_proj_kernel,
        grid=(n // ROW_TILE,),
        in_specs=[pl.BlockSpec((ROW_TILE, D_MODEL), lambda i: (i, 0)),
                  pl.BlockSpec((1, D_MODEL), lambda i: (0, 0)),
                  pl.BlockSpec((D_MODEL, D_IN), lambda i: (0, 0))],
        out_specs=[pl.BlockSpec((ROW_TILE, w), lambda i: (i, 0)) for w in IN_WIDTHS],
        out_shape=[jax.ShapeDtypeStruct((n, w), jnp.float32) for w in IN_WIDTHS],
        compiler_params=pltpu.CompilerParams(dimension_semantics=("parallel",),
                                             vmem_limit_bytes=VMEM_LIMIT),
        name="in_proj",
    )(x, g.reshape(1, D_MODEL), w_bf16)


def _out_proj_kernel(x_ref, y0_ref, y1_ref, y2_ref, y3_ref, g_ref, w_ref, o_ref):
    acc = x_ref[...]
    for i, y_ref in enumerate((y0_ref, y1_ref, y2_ref, y3_ref)):
        n = (_rms(y_ref[...]) * g_ref[i:i + 1, :]).astype(jnp.bfloat16)
        acc = acc + jnp.dot(n, w_ref[i * W_GROUP:(i + 1) * W_GROUP, :],
                            preferred_element_type=jnp.float32)
    o_ref[...] = acc


def out_proj(x, ys, g, w_bf16):
    n = x.shape[0]
    row = lambda w: pl.BlockSpec((ROW_TILE, w), lambda i: (i, 0))
    return pl.pallas_call(
        _out_proj_kernel,
        grid=(n // ROW_TILE,),
        in_specs=[row(D_MODEL)] + [row(W_GROUP)] * N_MIXERS
                 + [pl.BlockSpec((N_MIXERS, W_GROUP), lambda i: (0, 0)),
                    pl.BlockSpec((D_MODEL, D_MODEL), lambda i: (0, 0))],
        out_specs=row(D_MODEL),
        out_shape=jax.ShapeDtypeStruct((n, D_MODEL), jnp.float32),
        compiler_params=pltpu.CompilerParams(dimension_semantics=("parallel",),
                                             vmem_limit_bytes=VMEM_LIMIT),
        name="out_proj",
    )(x, *ys, g, w_bf16)


def _mlp_kernel(x_ref, g_ref, w1_ref, w2_ref, o_ref, h_ref):
    j = pl.program_id(1)

    @pl.when(j == 0)
    def _():
        x = x_ref[...]
        h_ref[...] = (_rms(x) * g_ref[...]).astype(jnp.bfloat16)
        o_ref[...] = x

    a = jnp.dot(h_ref[...], w1_ref[...], preferred_element_type=jnp.float32)
    a = jnp.square(jnp.maximum(a, 0.0)).astype(jnp.bfloat16)
    o_ref[...] += jnp.dot(a, w2_ref[...], preferred_element_type=jnp.float32)


def mlp(x, g, w1_bf16, w2_bf16):
    n = x.shape[0]
    return pl.pallas_call(
        _mlp_kernel,
        grid=(n // ROW_TILE, D_FF // FF_TILE),
        in_specs=[pl.BlockSpec((ROW_TILE, D_MODEL), lambda i, j: (i, 0)),
                  pl.BlockSpec((1, D_MODEL), lambda i, j: (0, 0)),
                  pl.BlockSpec((D_MODEL, FF_TILE), lambda i, j: (0, j)),
                  pl.BlockSpec((FF_TILE, D_MODEL), lambda i, j: (j, 0))],
        out_specs=pl.BlockSpec((ROW_TILE, D_MODEL), lambda i, j: (i, 0)),
        out_shape=jax.ShapeDtypeStruct((n, D_MODEL), jnp.float32),
        scratch_shapes=[pltpu.VMEM((ROW_TILE, D_MODEL), jnp.bfloat16)],
        compiler_params=pltpu.CompilerParams(dimension_semantics=("parallel", "arbitrary"),
                                             vmem_limit_bytes=VMEM_LIMIT),
        name="mlp",
    )(x, g.reshape(1, D_MODEL), w1_bf16, w2_bf16)


def _head_sum(x):
    lane = lax.broadcasted_iota(jnp.int32, (W_GROUP, W_GROUP), 0) // RW_HEAD
    col = lax.broadcasted_iota(jnp.int32, (W_GROUP, W_GROUP), 1) // RW_HEAD
    ones = (lane == col).astype(jnp.float32)
    return jnp.dot(x, ones, precision=lax.Precision.HIGHEST, preferred_element_type=jnp.float32)


def _softplus(x):
    return jnp.maximum(x, 0.0) + jnp.log(1.0 + jnp.exp(-jnp.abs(x)))


def _shifted(z, prev_row, next_row):
    t = z.shape[0]
    row = lax.broadcasted_iota(jnp.int32, z.shape, 0)
    zp = jnp.where(row == 0, prev_row, pltpu.roll(z, 1, 0))
    zn = jnp.where(row == t - 1, next_row, pltpu.roll(z, t - 1, 0))
    return zp, zn


def _store_heads(o_ref, x):
    for h in range(RW_HEADS):
        o_ref[h] = x[:, h * RW_HEAD:(h + 1) * RW_HEAD]


def _rwkv_prep_kernel(z_ref, zp_ref, zn_ref, mu_ref, w0_ref, w2_ref, a0_ref, a2_ref, g2_ref, kk_ref, ka_ref,
                      rk_ref, nkk_ref, r_ref, v_ref, dec0_ref, dec1_ref, kd0_ref, kd1_ref, b0_ref, b1_ref,
                      g_ref, bonus_ref):
    z = z_ref[...]
    zp, zn = _shifted(z, zp_ref[0], zn_ref[0])
    z = z + mu_ref[...] * (0.5 * (zp + zn) - z)
    r = z[:, 0:W_GROUP]
    k = z[:, W_GROUP:2 * W_GROUP]
    v = z[:, 2 * W_GROUP:3 * W_GROUP]
    wd = z[:, RW_SPLITS[2]:RW_SPLITS[3]]
    ad = z[:, RW_SPLITS[3]:RW_SPLITS[4]]
    gd = z[:, RW_SPLITS[4]:RW_IN]
    bf = jnp.bfloat16
    g_ref[...] = jnp.dot(jax.nn.sigmoid(gd).astype(bf), g2_ref[...], preferred_element_type=jnp.float32)
    kk = k * kk_ref[...]
    kk = kk / jnp.maximum(jnp.sqrt(_head_sum(kk * kk)), 1e-12)
    _store_heads(nkk_ref, -kk)
    _store_heads(r_ref, r)
    _store_heads(v_ref, v)
    bonus_ref[...] = _head_sum(r * k * rk_ref[...]) * v
    tw = jnp.tanh(wd).astype(bf)
    adb = ad.astype(bf)
    for d, (dec_ref, kd_ref, b_ref) in enumerate(((dec0_ref, kd0_ref, b0_ref), (dec1_ref, kd1_ref, b1_ref))):
        w = w0_ref[d:d + 1, :] + jnp.dot(tw, w2_ref[d], preferred_element_type=jnp.float32)
        w = -_softplus(-w) - 0.5
        _store_heads(dec_ref, jnp.exp(-jnp.exp(w)))
        a = jax.nn.sigmoid(a0_ref[d:d + 1, :] + jnp.dot(adb, a2_ref[d], preferred_element_type=jnp.float32))
        _store_heads(kd_ref, k * (1.0 + (a - 1.0) * ka_ref[...]))
        _store_heads(b_ref, kk * a)


def _halo_rows(z, n_seq, tile):
    n, c = z.shape
    zt = z.reshape(n_seq, n // n_seq // tile, tile, c)
    zero = jnp.zeros((n_seq, 1, c), z.dtype)
    prev = jnp.concatenate([zero, zt[:, :-1, -1]], axis=1).reshape(n // tile, 1, c)
    nxt = jnp.concatenate([zt[:, 1:, 0], zero], axis=1).reshape(n // tile, 1, c)
    return prev, nxt


def rwkv_prep(z, n_seq, mu, w0, w2, a0, a2, g2, k_k, k_a, r_k):
    n = z.shape[0]
    prev, nxt = _halo_rows(z, n_seq, ROW_TILE)
    bf = jnp.bfloat16
    full = lambda *s: pl.BlockSpec(s, lambda i: (0,) * len(s))
    row = pl.BlockSpec((ROW_TILE, W_GROUP), lambda i: (i, 0))
    heads = pl.BlockSpec((RW_HEADS, ROW_TILE, RW_HEAD), lambda i: (0, i, 0))
    halo = pl.BlockSpec((1, 1, RW_IN), lambda i: (i, 0, 0))
    vec = lambda x: x.reshape(1, -1)
    return pl.pallas_call(
        _rwkv_prep_kernel,
        grid=(n // ROW_TILE,),
        in_specs=[pl.BlockSpec((ROW_TILE, RW_IN), lambda i: (i, 0)), halo, halo, full(1, RW_IN),
                  full(2, W_GROUP), full(2, RW_DECAY_RANK, W_GROUP), full(2, W_GROUP),
                  full(2, RW_A_RANK, W_GROUP), full(RW_G_RANK, W_GROUP), full(1, W_GROUP), full(1, W_GROUP),
                  full(1, W_GROUP)],
        out_specs=[heads] * 9 + [row] * 2,
        out_shape=[jax.ShapeDtypeStruct((RW_HEADS, n, RW_HEAD), jnp.float32)] * 9
                  + [jax.ShapeDtypeStruct((n, W_GROUP), jnp.float32)] * 2,
        compiler_params=pltpu.CompilerParams(dimension_semantics=("parallel",),
                                             vmem_limit_bytes=VMEM_LIMIT),
        name="rwkv_prep",
    )(z, prev, nxt, vec(mu), w0, w2.astype(bf), a0, a2.astype(bf), g2.astype(bf), vec(k_k), vec(k_a), vec(r_k))


def _rwkv_scan_kernel(af_ref, wf_ref, bf_ref, kf_ref, rf_ref, vf_ref, ab_ref, wb_ref, bb_ref, kb_ref, rb_ref,
                      vb_ref, yf_ref, yb_ref, s_ref):
    @pl.when(pl.program_id(0) == 0)
    def _():
        s_ref[...] = jnp.zeros_like(s_ref)

    dirs = ((af_ref, wf_ref, bf_ref, kf_ref, rf_ref, vf_ref, yf_ref),
            (ab_ref, wb_ref, bb_ref, kb_ref, rb_ref, vb_ref, yb_ref))

    def step(i, carry):
        for j, (a_ref, w_ref, b_ref, k_ref, r_ref, v_ref, y_ref) in enumerate(dirs):
            t = i if j == 0 else SCAN_T - 1 - i
            vt = v_ref[t]
            sa = s_ref[j, 0] * a_ref[t, 0:1, :]
            for k in range(1, RW_HEAD):
                sa = sa + s_ref[j, k] * a_ref[t, k:k + 1, :]
            y = None
            for k in range(RW_HEAD):
                s = s_ref[j, k] * w_ref[t, k:k + 1, :] + sa * b_ref[t, k:k + 1, :] + vt * k_ref[t, k:k + 1, :]
                s_ref[j, k] = s
                yk = s * r_ref[t, k:k + 1, :]
                y = yk if y is None else y + yk
            y_ref[t] = y
        return carry

    lax.fori_loop(0, SCAN_T, step, 0)


def rwkv_scan(a, r, v, w_f, b_f, k_f, w_b, b_b, k_b):
    L, _, nc = a.shape
    nblk = L // SCAN_T
    fwd = pl.BlockSpec((SCAN_T, RW_HEAD, nc), lambda i: (i, 0, 0))
    bwd = pl.BlockSpec((SCAN_T, RW_HEAD, nc), lambda i: (nblk - 1 - i, 0, 0))
    out = jax.ShapeDtypeStruct((L, RW_HEAD, nc), jnp.float32)
    return pl.pallas_call(
        _rwkv_scan_kernel,
        grid=(nblk,),
        in_specs=[fwd] * 6 + [bwd] * 6,
        out_specs=[fwd, bwd],
        out_shape=[out, out],
        scratch_shapes=[pltpu.VMEM((2, RW_HEAD, RW_HEAD, nc), jnp.float32)],
        compiler_params=pltpu.CompilerParams(dimension_semantics=("arbitrary",),
                                             vmem_limit_bytes=VMEM_LIMIT),
        name="rwkv_scan",
    )(a, w_f, b_f, k_f, r, v, a, w_b, b_b, k_b, r, v)


def _rwkv_post_kernel(yf_ref, yb_ref, bonus_ref, g_ref, lw_ref, lb_ref, o_ref):
    y = jnp.concatenate([yf_ref[h] + yb_ref[h] for h in range(RW_HEADS)], axis=-1)
    mean = _head_sum(y) * (1.0 / RW_HEAD)
    c = y - mean
    var = _head_sum(c * c) * (1.0 / RW_HEAD)
    y = c * lax.rsqrt(var + RW_LN_EPS) * lw_ref[...] + lb_ref[...]
    o_ref[...] = (y + bonus_ref[...]) * g_ref[...]


def rwkv_post(y_f, y_b, bonus, g, ln_w, ln_b):
    n = bonus.shape[0]
    row = pl.BlockSpec((ROW_TILE, W_GROUP), lambda i: (i, 0))
    heads = pl.BlockSpec((RW_HEADS, ROW_TILE, RW_HEAD), lambda i: (0, i, 0))
    vec = pl.BlockSpec((1, W_GROUP), lambda i: (0, 0))
    return pl.pallas_call(
        _rwkv_post_kernel,
        grid=(n // ROW_TILE,),
        in_specs=[heads, heads, row, row, vec, vec],
        out_specs=row,
        out_shape=jax.ShapeDtypeStruct((n, W_GROUP), jnp.float32),
        compiler_params=pltpu.CompilerParams(dimension_semantics=("parallel",),
                                             vmem_limit_bytes=VMEM_LIMIT),
        name="rwkv_post",
    )(y_f, y_b, bonus, g, ln_w.reshape(1, W_GROUP), ln_b.reshape(1, W_GROUP))


def rwkv_mixer(z, n_seq, mu, w0, w2, a0, a2, g2, k_k, k_a, r_k, ln_w, ln_b):
    n = z.shape[0]
    L = n // n_seq
    n_chain = n_seq * RW_HEADS
    assert n_chain <= V7X_LANES
    nkk, r, v, dec0, dec1, kd0, kd1, b0, b1, g, bonus = rwkv_prep(z, n_seq, mu, w0, w2, a0, a2, g2, k_k, k_a, r_k)

    def chains(x):
        return x.reshape(RW_HEADS, n_seq, L, RW_HEAD).transpose(2, 3, 0, 1).reshape(L, RW_HEAD, n_chain)

    def tokens(y):
        return y.reshape(L, RW_HEAD, RW_HEADS, n_seq).transpose(2, 3, 0, 1).reshape(RW_HEADS, n, RW_HEAD)

    y_f, y_b = rwkv_scan(chains(nkk), chains(r), chains(v), chains(dec0), chains(b0), chains(kd0),
                         chains(dec1), chains(b1), chains(kd1))
    return rwkv_post(tokens(y_f), tokens(y_b), bonus, g, ln_w, ln_b)


S5_NSTATE = S5_GROUPS * S5_STATE
S5_T = 512


def _cmul(ar, ai, br, bi):
    return ar * br - ai * bi, ar * bi + ai * br


def _s5_operators(lam_re, lam_im, log_dt, b_re, b_im, c_re, c_im):
    dt = jnp.exp(log_dt)[..., None]
    mag = jnp.exp(lam_re * dt)
    ab_re = mag * jnp.cos(lam_im * dt)
    ab_im = mag * jnp.sin(lam_im * dt)
    den = lam_re * lam_re + lam_im * lam_im
    n_re = ab_re - 1.0
    f_re = (n_re * lam_re + ab_im * lam_im) / den
    f_im = (ab_im * lam_re - n_re * lam_im) / den
    bb_re, bb_im = _cmul(f_re[..., None], f_im[..., None], b_re, b_im)
    eye = jnp.eye(S5_GROUPS, dtype=jnp.float32)

    def in_map(bb):
        return jnp.einsum('gh,dhnc->dgchn', eye, bb).reshape(2, W_GROUP, S5_NSTATE)

    def out_map(c):
        return jnp.einsum('hg,dgcn->dhngc', eye, c).reshape(2, S5_NSTATE, W_GROUP)

    bmat = jnp.concatenate([in_map(bb_re), in_map(bb_im)], axis=-1)
    cmat = jnp.concatenate([out_map(c_re), -out_map(c_im)], axis=1)
    lam = jnp.stack([ab_re.reshape(2, S5_NSTATE), ab_im.reshape(2, S5_NSTATE)], axis=1)
    return lam, bmat.astype(jnp.bfloat16), cmat.astype(jnp.bfloat16)


def _s5_scan_kernel(u_ref, lam_ref, bmat_ref, cmat_ref, y_ref, st_ref, carry_ref):
    d = pl.program_id(1)

    @pl.when(pl.program_id(2) == 0)
    def _():
        carry_ref[...] = jnp.zeros_like(carry_ref)

    st_ref[...] = jnp.dot(u_ref[...].astype(jnp.bfloat16), bmat_ref[0], preferred_element_type=jnp.float32)
    lam_r = lam_ref[0, 0:1, :]
    lam_i = lam_ref[0, 1:2, :]
    re = slice(0, S5_NSTATE)
    im = slice(S5_NSTATE, 2 * S5_NSTATE)

    def step(i, carry):
        sr, si = carry
        t = i + d * (S5_T - 1 - 2 * i)
        nr = lam_r * sr - lam_i * si + st_ref[pl.ds(t, 1), re]
        ni = lam_r * si + lam_i * sr + st_ref[pl.ds(t, 1), im]
        st_ref[pl.ds(t, 1), re] = nr
        st_ref[pl.ds(t, 1), im] = ni
        return nr, ni

    sr, si = lax.fori_loop(0, S5_T, step, (carry_ref[0:1, re], carry_ref[0:1, im]), unroll=4)
    carry_ref[0:1, re] = sr
    carry_ref[0:1, im] = si
    y_ref[0] = jnp.dot(st_ref[...].astype(jnp.bfloat16), cmat_ref[0], preferred_element_type=jnp.float32)


def s5_scan(z, lam, bmat, cmat, n_seq):
    n = z.shape[0]
    nch = n // n_seq // S5_T

    def row_block(b, d, c):
        return b * nch + c + d * (nch - 1 - 2 * c)

    return pl.pallas_call(
        _s5_scan_kernel,
        grid=(n_seq, 2, nch),
        in_specs=[pl.BlockSpec((S5_T, W_GROUP), lambda b, d, c: (row_block(b, d, c), 0)),
                  pl.BlockSpec((1, 2, S5_NSTATE), lambda b, d, c: (d, 0, 0)),
                  pl.BlockSpec((1, W_GROUP, 2 * S5_NSTATE), lambda b, d, c: (d, 0, 0)),
                  pl.BlockSpec((1, 2 * S5_NSTATE, W_GROUP), lambda b, d, c: (d, 0, 0))],
        out_specs=pl.BlockSpec((1, S5_T, W_GROUP), lambda b, d, c: (d, row_block(b, d, c), 0)),
        out_shape=jax.ShapeDtypeStruct((2, n, W_GROUP), jnp.float32),
        scratch_shapes=[pltpu.VMEM((S5_T, 2 * S5_NSTATE), jnp.float32),
                        pltpu.VMEM((8, 2 * S5_NSTATE), jnp.float32)],
        compiler_params=pltpu.CompilerParams(dimension_semantics=("arbitrary", "arbitrary", "arbitrary"),
                                             vmem_limit_bytes=VMEM_LIMIT),
        name="s5_scan",
    )(z, lam, bmat, cmat)


def _s5_finish_kernel(y_ref, z_ref, d_ref, w_ref, b_ref, o_ref):
    y = y_ref[0] + y_ref[1] + d_ref[...] * z_ref[...]
    g = jax.nn.gelu(y)
    gate = jnp.dot(g.astype(jnp.bfloat16), w_ref[...], preferred_element_type=jnp.float32) + b_ref[...]
    o_ref[...] = g * jax.nn.sigmoid(gate)


def s5_finish(y2, z, d, glu_w_bf16, glu_b):
    n = z.shape[0]
    vec = pl.BlockSpec((1, W_GROUP), lambda i: (0, 0))
    return pl.pallas_call(
        _s5_finish_kernel,
        grid=(n // ROW_TILE,),
        in_specs=[pl.BlockSpec((2, ROW_TILE, W_GROUP), lambda i: (0, i, 0)),
                  pl.BlockSpec((ROW_TILE, W_GROUP), lambda i: (i, 0)),
                  vec, pl.BlockSpec((W_GROUP, W_GROUP), lambda i: (0, 0)), vec],
        out_specs=pl.BlockSpec((ROW_TILE, W_GROUP), lambda i: (i, 0)),
        out_shape=jax.ShapeDtypeStruct((n, W_GROUP), jnp.float32),
        compiler_params=pltpu.CompilerParams(dimension_semantics=("parallel",),
                                             vmem_limit_bytes=VMEM_LIMIT),
        name="s5_finish",
    )(y2, z, d.reshape(1, W_GROUP), glu_w_bf16, glu_b.reshape(1, W_GROUP))


def s5_mixer(z, n_seq, lam_re, lam_im, log_dt, b_re, b_im, c_re, c_im, d, glu_w, glu_b):
    lam, bmat, cmat = _s5_operators(lam_re, lam_im, log_dt, b_re, b_im, c_re, c_im)
    y2 = s5_scan(z, lam, bmat, cmat, n_seq)
    return s5_finish(y2, z, d, glu_w.astype(jnp.bfloat16), glu_b)


HY_L = 4096
HY_N = 2 * HY_L
HY_N1 = 64
HY_N2 = 128
HY_K1 = HY_N1 // 2 + 1
HY_KB = 11
HY_KP = 24
HY_C = V7X_LANES


def _hyena_dft_tables():
    n1 = np.arange(HY_N1 // 2)
    k1 = np.arange(HY_K1)
    ang1 = 2.0 * np.pi * np.outer(k1, n1) / HY_N1
    weight = np.where((k1 == 0) | (k1 == HY_N1 // 2), 1.0, 2.0) / HY_N
    nblk = HY_K1 // HY_KB
    fa = np.zeros((nblk, HY_KP, HY_N1 // 2), np.float32)
    fi = np.zeros((nblk, HY_N1 // 2, HY_KP), np.float32)
    for b in range(nblk):
        sl = slice(b * HY_KB, (b + 1) * HY_KB)
        fa[b, :HY_KB] = np.cos(ang1[sl])
        fa[b, HY_KB:2 * HY_KB] = -np.sin(ang1[sl])
        fi[b, :, :HY_KB] = (np.cos(ang1[sl]) * weight[sl, None]).T
        fi[b, :, HY_KB:2 * HY_KB] = (-np.sin(ang1[sl]) * weight[sl, None]).T
    n2 = np.arange(HY_N2)
    k = k1[:, None] + HY_N1 * np.arange(HY_N2)[None, :]
    ang = 2.0 * np.pi * (k[:, :, None] * n2[None, None, :] % HY_N) / HY_N
    c, s = np.cos(ang), np.sin(ang)
    g = np.concatenate([np.concatenate([c, s], axis=2), np.concatenate([-s, c], axis=2)], axis=1)
    gi = np.transpose(g, (0, 2, 1))
    bf = jnp.bfloat16
    return (jnp.asarray(fa, bf), jnp.asarray(fi, bf), jnp.asarray(g, bf), jnp.asarray(gi, bf))


def _hyena_filter_spectra(w1, b1, freq, w2, b2, w3, log_rate):
    L = HY_L
    t = jnp.arange(L, dtype=jnp.float32) / L
    ang = 2.0 * math.pi * t[:, None] * jnp.arange(1, HY_BANDS + 1, dtype=jnp.float32)
    feats = jnp.concatenate([t[:, None], jnp.sin(ang), jnp.cos(ang)], axis=-1)
    h = jnp.sin(freq[0] * (feats @ w1 + b1))
    h = jnp.sin(freq[1] * (h @ w2 + b2))
    h = (h @ w3).reshape(L, 2, HY_ORDER, W_GROUP)
    h = h * jnp.exp(-jnp.exp(log_rate)[None] * t[:, None, None, None])
    fwd, bwd = h[:, 0], h[:, 1]
    k = jnp.concatenate([fwd, jnp.zeros_like(fwd[:1]), bwd[:0:-1]], axis=0)
    k = k / jnp.sum(jnp.abs(k), axis=0, keepdims=True)
    kf = jnp.fft.fft(k, axis=0).reshape(HY_N2, HY_N1, HY_ORDER, W_GROUP)[:, :HY_K1]
    kf = jnp.transpose(kf, (2, 1, 0, 3))
    return jnp.stack([jnp.real(kf), jnp.imag(kf)], axis=2).astype(jnp.float32)


def _hy_pre_kernel(z_ref, zp_ref, zn_ref, w_ref, b_ref, v_ref, x1_ref, x2_ref):
    z = z_ref[...]
    zp, zn = _shifted(z, zp_ref[0], zn_ref[0])
    z = w_ref[0:1, :] * zp + w_ref[1:2, :] * z + w_ref[2:3, :] * zn + b_ref[...]
    v_ref[...] = z[:, 0:W_GROUP]
    x1_ref[...] = z[:, W_GROUP:2 * W_GROUP]
    x2_ref[...] = z[:, 2 * W_GROUP:3 * W_GROUP]


def hy_pre(z, n_seq, conv_w, conv_b):
    n = z.shape[0]
    prev, nxt = _halo_rows(z, n_seq, ROW_TILE)
    row = pl.BlockSpec((ROW_TILE, W_GROUP), lambda i: (i, 0))
    halo = pl.BlockSpec((1, 1, HY_IN), lambda i: (i, 0, 0))
    return pl.pallas_call(
        _hy_pre_kernel,
        grid=(n // ROW_TILE,),
        in_specs=[pl.BlockSpec((ROW_TILE, HY_IN), lambda i: (i, 0)), halo, halo,
                  pl.BlockSpec((3, HY_IN), lambda i: (0, 0)), pl.BlockSpec((1, HY_IN), lambda i: (0, 0))],
        out_specs=[row] * 3,
        out_shape=[jax.ShapeDtypeStruct((n, W_GROUP), jnp.float32)] * 3,
        compiler_params=pltpu.CompilerParams(dimension_semantics=("parallel",),
                                             vmem_limit_bytes=VMEM_LIMIT),
        name="hy_pre",
    )(z, prev, nxt, conv_w, conv_b.reshape(1, HY_IN))


def _hy_conv_kernel(u_ref, fa_ref, fi_ref, g_ref, gi_ref, kf_ref, o_ref, as_ref, bs_ref):
    kb = pl.program_id(2)
    bf = jnp.bfloat16
    fa = fa_ref[0]
    fi = fi_ref[0]
    for n2 in range(HY_N2):
        rows = u_ref[pl.ds(n2, HY_N1 // 2, stride=HY_N2), :].astype(bf)
        as_ref[n2] = jnp.dot(fa, rows, preferred_element_type=jnp.float32)
    bs_ref[:, 2 * HY_KB:, :] = jnp.zeros((HY_N2, HY_KP - 2 * HY_KB, HY_C), jnp.float32)
    for j in range(HY_KB):
        a = jnp.concatenate([as_ref[:, j, :], as_ref[:, HY_KB + j, :]], axis=0).astype(bf)
        x = jnp.dot(g_ref[j], a, preferred_element_type=jnp.float32)
        xr, xi = x[:HY_N2], x[HY_N2:]
        kr, ki = kf_ref[0, j, 0], kf_ref[0, j, 1]
        y = jnp.concatenate([xr * kr - xi * ki, xr * ki + xi * kr], axis=0).astype(bf)
        b = jnp.dot(gi_ref[j], y, preferred_element_type=jnp.float32)
        bs_ref[:, j, :] = b[:HY_N2]
        bs_ref[:, HY_KB + j, :] = b[HY_N2:]
    @pl.when(kb == 0)
    def _():
        o_ref[...] = jnp.zeros_like(o_ref)

    for n2 in range(HY_N2):
        part = jnp.dot(fi, bs_ref[n2].astype(bf), preferred_element_type=jnp.float32)
        dst = pl.ds(n2, HY_N1 // 2, stride=HY_N2)
        o_ref[dst, :] = o_ref[dst, :] + part


def hy_conv(u, n_seq, kf, tables):
    n = u.shape[0]
    assert n // n_seq == HY_L
    fa, fi, g, gi = tables
    nblk = HY_K1 // HY_KB
    seq = pl.BlockSpec((HY_L, HY_C), lambda b, c, k: (b, c))
    return pl.pallas_call(
        _hy_conv_kernel,
        grid=(n_seq, W_GROUP // HY_C, nblk),
        in_specs=[seq,
                  pl.BlockSpec((1, HY_KP, HY_N1 // 2), lambda b, c, k: (k, 0, 0)),
                  pl.BlockSpec((1, HY_N1 // 2, HY_KP), lambda b, c, k: (k, 0, 0)),
                  pl.BlockSpec((HY_KB, 2 * HY_N2, 2 * HY_N2), lambda b, c, k: (k, 0, 0)),
                  pl.BlockSpec((HY_KB, 2 * HY_N2, 2 * HY_N2), lambda b, c, k: (k, 0, 0)),
                  pl.BlockSpec((1, HY_KB, 2, HY_N2, HY_C), lambda b, c, k: (0, k, 0, 0, c))],
        out_specs=seq,
        out_shape=jax.ShapeDtypeStruct((n, W_GROUP), jnp.float32),
        scratch_shapes=[pltpu.VMEM((HY_N2, HY_KP, HY_C), jnp.float32),
                        pltpu.VMEM((HY_N2, HY_KP, HY_C), jnp.float32)],
        compiler_params=pltpu.CompilerParams(dimension_semantics=("arbitrary", "arbitrary", "arbitrary"),
                                             vmem_limit_bytes=VMEM_LIMIT),
        name="hy_conv",
    )(u, fa, fi, g, gi, kf)


def _hy_gate_kernel(c_ref, u_ref, x_ref, d_ref, o_ref):
    o_ref[...] = x_ref[...] * (c_ref[...] + d_ref[...] * u_ref[...])


def hy_gate(c, u, gate, d):
    n = u.shape[0]
    row = pl.BlockSpec((ROW_TILE, W_GROUP), lambda i: (i, 0))
    return pl.pallas_call(
        _hy_gate_kernel,
        grid=(n // ROW_TILE,),
        in_specs=[row, row, row, pl.BlockSpec((1, W_GROUP), lambda i: (0, 0))],
        out_specs=row,
        out_shape=jax.ShapeDtypeStruct((n, W_GROUP), jnp.float32),
        compiler_params=pltpu.CompilerParams(dimension_semantics=("parallel",),
                                             vmem_limit_bytes=VMEM_LIMIT),
        name="hy_gate",
    )(c, u, gate, d.reshape(1, W_GROUP))


def hyena_mixer(z, n_seq, conv_w, conv_b, f_w1, f_b1, f_freq, f_w2, f_b2, f_w3, log_rate, d):
    v, x1, x2 = hy_pre(z, n_seq, conv_w, conv_b)
    kf = _hyena_filter_spectra(f_w1, f_b1, f_freq, f_w2, f_b2, f_w3, log_rate)
    tables = _hyena_dft_tables()
    u = v
    for o, gate in enumerate((x1, x2)):
        u = hy_gate(hy_conv(u, n_seq, kf[o:o + 1], tables), u, gate, d[o])
    return u


NA_BAND = 8
NA_KROWS = 2 * NA_BAND
NA_KBLK = 4


def _na_bias_tables(rel_bias, rows):
    hi = lax.Precision.HIGHEST
    qc = np.arange(GRID_W)
    ws = np.clip(qc - NA_WIN_C // 2, 0, GRID_W - NA_WIN_C)
    col_ok = (qc[None, :] >= ws[:, None]) & (qc[None, :] < ws[:, None] + NA_WIN_C)
    dc = np.clip(qc[None, :] - qc[:, None] + NA_WIN_C - 1, 0, 2 * NA_WIN_C - 2)
    dc_sel = (dc[..., None] == np.arange(2 * NA_WIN_C - 1)).astype(np.float32)
    col_bias = jnp.einsum('qkc,hdc->hdqk', dc_sel, rel_bias, precision=hi)
    tabs = []
    for band in (0, 1, rows // NA_BAND - 1):
        kb = int(np.clip(NA_BAND * band - NA_WIN_R // 2, 0, rows - NA_KROWS))
        r = NA_BAND * band + np.arange(NA_BAND)
        rs = np.clip(r - NA_WIN_R // 2, 0, rows - NA_WIN_R)
        krow = kb + np.arange(NA_KROWS)
        row_ok = (krow[None, :] >= rs[:, None]) & (krow[None, :] < rs[:, None] + NA_WIN_R)
        dr = np.clip(krow[None, :] - r[:, None] + NA_WIN_R - 1, 0, 2 * NA_WIN_R - 2)
        dr_sel = (dr[..., None] == np.arange(2 * NA_WIN_R - 1)).astype(np.float32)
        bias = jnp.einsum('jwd,hdqk->hjqwk', dr_sel, col_bias, precision=hi)
        ok = row_ok[:, None, :, None] & col_ok[None, :, None, :]
        tab = jnp.where(jnp.asarray(ok)[None], bias, NEG_INF)
        tabs.append(tab.reshape(NA_HEADS, NA_BAND * GRID_W, NA_KROWS * GRID_W))
    return jnp.stack(tabs)


def _na_kernel(q_ref, k0_ref, k1_ref, k2_ref, k3_ref, v0_ref, v1_ref, v2_ref, v3_ref, qg_ref, kg_ref,
               tab_ref, o_ref):
    q = q_ref[...]
    k = jnp.concatenate([k0_ref[...], k1_ref[...], k2_ref[...], k3_ref[...]], axis=0)
    v = jnp.concatenate([v0_ref[...], v1_ref[...], v2_ref[...], v3_ref[...]], axis=0)
    for h in range(NA_HEADS):
        sl = slice(h * NA_HEAD, (h + 1) * NA_HEAD)
        qh = (_rms(q[:, sl]) * qg_ref[...] * (NA_HEAD ** -0.5)).astype(jnp.bfloat16)
        kh = (_rms(k[:, sl]) * kg_ref[...]).astype(jnp.bfloat16)
        s = lax.dot_general(qh, kh, (((1,), (1,)), ((), ())), preferred_element_type=jnp.float32)
        s = s + tab_ref[0, h]
        p = jnp.exp(s - jnp.max(s, axis=-1, keepdims=True))
        den = jnp.sum(p, axis=-1, keepdims=True)
        o = jnp.dot(p.astype(jnp.bfloat16), v[:, sl].astype(jnp.bfloat16), preferred_element_type=jnp.float32)
        o_ref[:, sl] = o / den


def na_mixer(q, k, v, n_seq, q_g, k_g, rel_bias):
    n = q.shape[0]
    rows = n // n_seq // GRID_W
    n_band = rows // NA_BAND
    tab = _na_bias_tables(rel_bias, rows)
    qtok = NA_BAND * GRID_W
    ktok = NA_KBLK * GRID_W
    kblk_per_seq = rows // NA_KBLK

    def kv_spec(j):
        def index(i, b):
            first = jnp.clip(NA_BAND // NA_KBLK * i - 1, 0, kblk_per_seq - NA_KROWS // NA_KBLK)
            return (b * kblk_per_seq + first + j, 0)
        return pl.BlockSpec((ktok, W_GROUP), index)

    def tab_index(i, b):
        return (jnp.where(i == 0, 0, jnp.where(i == n_band - 1, 2, 1)), 0, 0, 0)

    qspec = pl.BlockSpec((qtok, W_GROUP), lambda i, b: (b * n_band + i, 0))
    gspec = pl.BlockSpec((1, NA_HEAD), lambda i, b: (0, 0))
    kvs = [kv_spec(j) for j in range(NA_KROWS // NA_KBLK)]
    return pl.pallas_call(
        _na_kernel,
        grid=(n_band, n_seq),
        in_specs=[qspec] + kvs + kvs + [gspec, gspec,
                  pl.BlockSpec((1, NA_HEADS, qtok, NA_KROWS * GRID_W), tab_index)],
        out_specs=qspec,
        out_shape=jax.ShapeDtypeStruct((n, W_GROUP), jnp.float32),
        compiler_params=pltpu.CompilerParams(dimension_semantics=("arbitrary", "arbitrary"),
                                             vmem_limit_bytes=VMEM_LIMIT),
        name="na_attn",
    )(q, k, k, k, k, v, v, v, v, q_g.reshape(1, NA_HEAD), k_g.reshape(1, NA_HEAD), tab)


def _trunk(x, p):
    Bn, L, _ = x.shape
    n = Bn * L
    x = x.reshape(n, D_MODEL)
    bf = jnp.bfloat16
    for l in range(DEPTH):
        z_s5, z_hy, z_rw, z_q, z_k, z_v = in_proj(x, p['ln1_g'][l], p['w_in'][l].astype(bf))
        y_s5 = s5_mixer(z_s5, Bn, p['s5_lam_re'][l], p['s5_lam_im'][l], p['s5_log_dt'][l],
                        p['s5_b_re'][l], p['s5_b_im'][l], p['s5_c_re'][l], p['s5_c_im'][l], p['s5_d'][l],
                        p['s5_glu_w'][l], p['s5_glu_b'][l])
        y_hy = hyena_mixer(z_hy, Bn, p['hy_conv_w'][l], p['hy_conv_b'][l], p['hy_f_w1'][l],
                           p['hy_f_b1'][l], p['hy_f_freq'][l], p['hy_f_w2'][l], p['hy_f_b2'][l],
                           p['hy_f_w3'][l], p['hy_log_rate'][l], p['hy_d'][l])
        y_rw = rwkv_mixer(z_rw, Bn, p['rw_mu'][l], p['rw_w0'][l], p['rw_w2'][l],
                          p['rw_a0'][l], p['rw_a2'][l], p['rw_g2'][l], p['rw_k_k'][l], p['rw_k_a'][l],
                          p['rw_r_k'][l], p['rw_ln_w'][l], p['rw_ln_b'][l])
        y_na = na_mixer(z_q, z_k, z_v, Bn, p['na_q_g'][l], p['na_k_g'][l], p['na_rel_bias'][l])
        ys = [y_s5, y_hy, y_rw, y_na]
        x = out_proj(x, ys, p['grp_g'][l], p['w_out'][l].astype(bf))
        x = mlp(x, p['ln2_g'][l], p['w_mlp1'][l].astype(bf), p['w_mlp2'][l].astype(bf))
    return x.reshape(Bn, L, D_MODEL)


def kernel(x_prompt, x_sample, ln1_g, w_in, s5_lam_re, s5_lam_im, s5_log_dt, s5_b_re, s5_b_im,
           s5_c_re, s5_c_im, s5_d, s5_glu_w, s5_glu_b, hy_conv_w, hy_conv_b, hy_f_w1, hy_f_b1,
           hy_f_freq, hy_f_w2, hy_f_b2, hy_f_w3, hy_log_rate, hy_d, rw_mu, rw_w0, rw_w2, rw_a0,
           rw_a2, rw_g2, rw_k_k, rw_k_a, rw_r_k, rw_ln_w, rw_ln_b, na_q_g, na_k_g, na_rel_bias,
           grp_g, w_out, ln2_g, w_mlp1, w_mlp2):
    p = dict(ln1_g=ln1_g, w_in=w_in, s5_lam_re=s5_lam_re, s5_lam_im=s5_lam_im, s5_log_dt=s5_log_dt,
             s5_b_re=s5_b_re, s5_b_im=s5_b_im, s5_c_re=s5_c_re, s5_c_im=s5_c_im, s5_d=s5_d,
             s5_glu_w=s5_glu_w, s5_glu_b=s5_glu_b, hy_conv_w=hy_conv_w, hy_conv_b=hy_conv_b,
             hy_f_w1=hy_f_w1, hy_f_b1=hy_f_b1, hy_f_freq=hy_f_freq, hy_f_w2=hy_f_w2, hy_f_b2=hy_f_b2,
             hy_f_w3=hy_f_w3, hy_log_rate=hy_log_rate, hy_d=hy_d, rw_mu=rw_mu, rw_w0=rw_w0, rw_w2=rw_w2,
             rw_a0=rw_a0, rw_a2=rw_a2, rw_g2=rw_g2, rw_k_k=rw_k_k, rw_k_a=rw_k_a, rw_r_k=rw_r_k,
             rw_ln_w=rw_ln_w, rw_ln_b=rw_ln_b, na_q_g=na_q_g, na_k_g=na_k_g, na_rel_bias=na_rel_bias,
             grp_g=grp_g, w_out=w_out, ln2_g=ln2_g, w_mlp1=w_mlp1, w_mlp2=w_mlp2)
    nb = x_prompt.shape[0]
    y = _trunk(jnp.concatenate([x_prompt, x_sample], axis=0), p)
    return (y[:nb], y[nb:])
```

```python
import math

import jax
import jax.numpy as jnp
import numpy as np
from jax import lax
from jax.experimental import pallas as pl
from jax.experimental.pallas import tpu as pltpu

D_MODEL = 1024
DEPTH = 4
GRID_W = 64
W_GROUP = 256
N_MIXERS = 4
D_FF = 4 * D_MODEL
NORM_EPS = 1e-6

S5_CH = 16
S5_GROUPS = W_GROUP // S5_CH
S5_STATE = 64
S5_IN = W_GROUP

HY_ORDER = 2
HY_BANDS = 8
HY_IN = (HY_ORDER + 1) * W_GROUP

RW_HEAD = 64
RW_HEADS = W_GROUP // RW_HEAD
RW_DECAY_RANK = 64
RW_A_RANK = 64
RW_G_RANK = 128
RW_LN_EPS = 64e-5
RW_IN = 3 * W_GROUP + RW_DECAY_RANK + RW_A_RANK + RW_G_RANK
RW_SPLITS = (W_GROUP, 2 * W_GROUP, 3 * W_GROUP, 3 * W_GROUP + RW_DECAY_RANK,
             3 * W_GROUP + RW_DECAY_RANK + RW_A_RANK)

NA_HEAD = 64
NA_HEADS = W_GROUP // NA_HEAD
NA_WIN_R = 8
NA_WIN_C = 16
NEG_INF = -1e30

D_IN = S5_IN + HY_IN + RW_IN + 3 * W_GROUP

V7X_LANES = 128
VMEM_LIMIT = 48 * 1024 * 1024

ROW_TILE = 512
FF_TILE = 1024
SCAN_T = 16
SCAN_UNROLL = 32


def _rms(x):
    return x * lax.rsqrt(jnp.mean(x * x, axis=-1, keepdims=True) + NORM_EPS)


IN_WIDTHS = (S5_IN, HY_IN, RW_IN, W_GROUP, W_GROUP, W_GROUP)


def _in_proj_kernel(x_ref, g_ref, w_ref, *o_refs):
    h = (_rms(x_ref[...]) * g_ref[...]).astype(jnp.bfloat16)
    lo = 0
    for o_ref, width in zip(o_refs, IN_WIDTHS):
        o_ref[...] = jnp.dot(h, w_ref[:, lo:lo + width], preferred_element_type=jnp.float32)
        lo += width


def in_proj(x, g, w_bf16):
    n = x.shape[0]
    return pl.pallas_call(
        _in_proj_kernel,
        grid=(n // ROW_TILE,),
        in_specs=[pl.BlockSpec((ROW_TILE, D_MODEL), lambda i: (i, 0)),
                  pl.BlockSpec((1, D_MODEL), lambda i: (0, 0)),
                  pl.BlockSpec((D_MODEL, D_IN), lambda i: (0, 0))],
        out_specs=[pl.BlockSpec((ROW_TILE, w), lambda i: (i, 0)) for w in IN_WIDTHS],
        out_shape=[jax.ShapeDtypeStruct((n, w), jnp.float32) for w in IN_WIDTHS],
        compiler_params=pltpu.CompilerParams(dimension_semantics=("parallel",),
                                             vmem_limit_bytes=VMEM_LIMIT),
        name="in_proj",
    )(x, g.reshape(1, D_MODEL), w_bf16)


def _out_proj_kernel(x_ref, y0_ref, y1_ref, y2_ref, y3_ref, g_ref, w_ref, o_ref):
    acc = x_ref[...]
    for i, y_ref in enumerate((y0_ref, y1_ref, y2_ref, y3_ref)):
        n = (_rms(y_ref[...]) * g_ref[i:i + 1, :]).astype(jnp.bfloat16)
        acc = acc + jnp.dot(n, w_ref[i * W_GROUP:(i + 1) * W_GROUP, :],
                            preferred_element_type=jnp.float32)
    o_ref[...] = acc


def out_proj(x, ys, g, w_bf16):
    n = x.shape[0]
    row = lambda w: pl.BlockSpec((ROW_TILE, w), lambda i: (i, 0))
    return pl.pallas_call(
        _out_proj_kernel,
        grid=(n // ROW_TILE,),
        in_specs=[row(D_MODEL)] + [row(W_GROUP)] * N_MIXERS
                 + [pl.BlockSpec((N_MIXERS, W_GROUP), lambda i: (0, 0)),
                    pl.BlockSpec((D_MODEL, D_MODEL), lambda i: (0, 0))],
        out_specs=row(D_MODEL),
        out_shape=jax.ShapeDtypeStruct((n, D_MODEL), jnp.float32),
        compiler_params=pltpu.CompilerParams(dimension_semantics=("parallel",),
                                             vmem_limit_bytes=VMEM_LIMIT),
        name="out_proj",
    )(x, *ys, g, w_bf16)


def _mlp_kernel(x_ref, g_ref, w1_ref, w2_ref, o_ref, h_ref):
    j = pl.program_id(1)

    @pl.when(j == 0)
    def _():
        x = x_ref[...]
        h_ref[...] = (_rms(x) * g_ref[...]).astype(jnp.bfloat16)
        o_ref[...] = x

    a = jnp.dot(h_ref[...], w1_ref[...], preferred_element_type=jnp.float32)
    a = jnp.square(jnp.maximum(a, 0.0)).astype(jnp.bfloat16)
    o_ref[...] += jnp.dot(a, w2_ref[...], preferred_element_type=jnp.float32)


def mlp(x, g, w1_bf16, w2_bf16):
    n = x.shape[0]
    return pl.pallas_call(
        _mlp_kernel,
        grid=(n // ROW_TILE, D_FF // FF_TILE),
        in_specs=[pl.BlockSpec((ROW_TILE, D_MODEL), lambda i, j: (i, 0)),
                  pl.BlockSpec((1, D_MODEL), lambda i, j: (0, 0)),
                  pl.BlockSpec((D_MODEL, FF_TILE), lambda i, j: (0, j)),
                  pl.BlockSpec((FF_TILE, D_MODEL), lambda i, j: (j, 0))],
        out_specs=pl.BlockSpec((ROW_TILE, D_MODEL), lambda i, j: (i, 0)),
        out_shape=jax.ShapeDtypeStruct((n, D_MODEL), jnp.float32),
        scratch_shapes=[pltpu.VMEM((ROW_TILE, D_MODEL), jnp.bfloat16)],
        compiler_params=pltpu.CompilerParams(dimension_semantics=("parallel", "arbitrary"),
                                             vmem_limit_bytes=VMEM_LIMIT),
        name="mlp",
    )(x, g.reshape(1, D_MODEL), w1_bf16, w2_bf16)


def _head_sum(x):
    lane = lax.broadcasted_iota(jnp.int32, (W_GROUP, W_GROUP), 0) // RW_HEAD
    col = lax.broadcasted_iota(jnp.int32, (W_GROUP, W_GROUP), 1) // RW_HEAD
    ones = (lane == col).astype(jnp.float32)
    return jnp.dot(x, ones, precision=lax.Precision.HIGHEST, preferred_element_type=jnp.float32)


def _softplus(x):
    return jnp.maximum(x, 0.0) + jnp.log(1.0 + jnp.exp(-jnp.abs(x)))


def _shifted(z, prev_row, next_row):
    t = z.shape[0]
    row = lax.broadcasted_iota(jnp.int32, z.shape, 0)
    zp = jnp.where(row == 0, prev_row, pltpu.roll(z, 1, 0))
    zn = jnp.where(row == t - 1, next_row, pltpu.roll(z, t - 1, 0))
    return zp, zn


def _store_heads(o_ref, x):
    for h in range(RW_HEADS):
        o_ref[h] = x[:, h * RW_HEAD:(h + 1) * RW_HEAD]


def _rwkv_prep_kernel(z_ref, zp_ref, zn_ref, mu_ref, w0_ref, w2_ref, a0_ref, a2_ref, g2_ref, kk_ref, ka_ref,
                      rk_ref, nkk_ref, r_ref, v_ref, dec0_ref, dec1_ref, kd0_ref, kd1_ref, b0_ref, b1_ref,
                      g_ref, bonus_ref):
    z = z_ref[...]
    zp, zn = _shifted(z, zp_ref[0], zn_ref[0])
    z = z + mu_ref[...] * (0.5 * (zp + zn) - z)
    r = z[:, 0:W_GROUP]
    k = z[:, W_GROUP:2 * W_GROUP]
    v = z[:, 2 * W_GROUP:3 * W_GROUP]
    wd = z[:, RW_SPLITS[2]:RW_SPLITS[3]]
    ad = z[:, RW_SPLITS[3]:RW_SPLITS[4]]
    gd = z[:, RW_SPLITS[4]:RW_IN]
    bf = jnp.bfloat16
    g_ref[...] = jnp.dot(jax.nn.sigmoid(gd).astype(bf), g2_ref[...], preferred_element_type=jnp.float32)
    kk = k * kk_ref[...]
    kk = kk / jnp.maximum(jnp.sqrt(_head_sum(kk * kk)), 1e-12)
    _store_heads(nkk_ref, -kk)
    _store_heads(r_ref, r)
    _store_heads(v_ref, v)
    bonus_ref[...] = _head_sum(r * k * rk_ref[...]) * v
    tw = jnp.tanh(wd).astype(bf)
    adb = ad.astype(bf)
    for d, (dec_ref, kd_ref, b_ref) in enumerate(((dec0_ref, kd0_ref, b0_ref), (dec1_ref, kd1_ref, b1_ref))):
        w = w0_ref[d:d + 1, :] + jnp.dot(tw, w2_ref[d], preferred_element_type=jnp.float32)
        w = -_softplus(-w) - 0.5
        _store_heads(dec_ref, jnp.exp(-jnp.exp(w)))
        a = jax.nn.sigmoid(a0_ref[d:d + 1, :] + jnp.dot(adb, a2_ref[d], preferred_element_type=jnp.float32))
        _store_heads(kd_ref, k * (1.0 + (a - 1.0) * ka_ref[...]))
        _store_heads(b_ref, kk * a)


def _halo_rows(z, n_seq, tile):
    n, c = z.shape
    zt = z.reshape(n_seq, n // n_seq // tile, tile, c)
    zero = jnp.zeros((n_seq, 1, c), z.dtype)
    prev = jnp.concatenate([zero, zt[:, :-1, -1]], axis=1).reshape(n // tile, 1, c)
    nxt = jnp.concatenate([zt[:, 1:, 0], zero], axis=1).reshape(n // tile, 1, c)
    return prev, nxt


def rwkv_prep(z, n_seq, mu, w0, w2, a0, a2, g2, k_k, k_a, r_k):
    n = z.shape[0]
    prev, nxt = _halo_rows(z, n_seq, ROW_TILE)
    bf = jnp.bfloat16
    full = lambda *s: pl.BlockSpec(s, lambda i: (0,) * len(s))
    row = pl.BlockSpec((ROW_TILE, W_GROUP), lambda i: (i, 0))
    heads = pl.BlockSpec((RW_HEADS, ROW_TILE, RW_HEAD), lambda i: (0, i, 0))
    halo = pl.BlockSpec((1, 1, RW_IN), lambda i: (i, 0, 0))
    vec = lambda x: x.reshape(1, -1)
    return pl.pallas_call(
        _rwkv_prep_kernel,
        grid=(n // ROW_TILE,),
        in_specs=[pl.BlockSpec((ROW_TILE, RW_IN), lambda i: (i, 0)), halo, halo, full(1, RW_IN),
                  full(2, W_GROUP), full(2, RW_DECAY_RANK, W_GROUP), full(2, W_GROUP),
                  full(2, RW_A_RANK, W_GROUP), full(RW_G_RANK, W_GROUP), full(1, W_GROUP), full(1, W_GROUP),
                  full(1, W_GROUP)],
        out_specs=[heads] * 9 + [row] * 2,
        out_shape=[jax.ShapeDtypeStruct((RW_HEADS, n, RW_HEAD), jnp.float32)] * 9
                  + [jax.ShapeDtypeStruct((n, W_GROUP), jnp.float32)] * 2,
        compiler_params=pltpu.CompilerParams(dimension_semantics=("parallel",),
                                             vmem_limit_bytes=VMEM_LIMIT),
        name="rwkv_prep",
    )(z, prev, nxt, vec(mu), w0, w2.astype(bf), a0, a2.astype(bf), g2.astype(bf), vec(k_k), vec(k_a), vec(r_k))


def _rwkv_scan_kernel(af_ref, wf_ref, bf_ref, kf_ref, rf_ref, vf_ref, ab_ref, wb_ref, bb_ref, kb_ref, rb_ref,
                      vb_ref, yf_ref, yb_ref, s_ref):
    @pl.when(pl.program_id(0) == 0)
    def _():
        s_ref[...] = jnp.zeros_like(s_ref)

    dirs = ((af_ref, wf_ref, bf_ref, kf_ref, rf_ref, vf_ref, yf_ref),
            (ab_ref, wb_ref, bb_ref, kb_ref, rb_ref, vb_ref, yb_ref))

    def step(i, carry):
        for j, (a_ref, w_ref, b_ref, k_ref, r_ref, v_ref, y_ref) in enumerate(dirs):
            t = i if j == 0 else SCAN_T - 1 - i
            vt = v_ref[t]

            def reduce_a(k, sa):
                return sa + s_ref[j, k] * a_ref[t, pl.ds(k, 1), :]

            sa = lax.fori_loop(0, RW_HEAD, reduce_a, jnp.zeros_like(vt), unroll=SCAN_UNROLL)

            def update(k, y):
                s = (s_ref[j, k] * w_ref[t, pl.ds(k, 1), :] + sa * b_ref[t, pl.ds(k, 1), :]
                     + vt * k_ref[t, pl.ds(k, 1), :])
                s_ref[j, k] = s
                return y + s * r_ref[t, pl.ds(k, 1), :]

            y_ref[t] = lax.fori_loop(0, RW_HEAD, update, jnp.zeros_like(vt), unroll=SCAN_UNROLL)
        return carry

    lax.fori_loop(0, SCAN_T, step, 0)


def rwkv_scan(a, r, v, w_f, b_f, k_f, w_b, b_b, k_b):
    L, _, nc = a.shape
    nblk = L // SCAN_T
    fwd = pl.BlockSpec((SCAN_T, RW_HEAD, nc), lambda i: (i, 0, 0))
    bwd = pl.BlockSpec((SCAN_T, RW_HEAD, nc), lambda i: (nblk - 1 - i, 0, 0))
    out = jax.ShapeDtypeStruct((L, RW_HEAD, nc), jnp.float32)
    return pl.pallas_call(
        _rwkv_scan_kernel,
        grid=(nblk,),
        in_specs=[fwd] * 6 + [bwd] * 6,
        out_specs=[fwd, bwd],
        out_shape=[out, out],
        scratch_shapes=[pltpu.VMEM((2, RW_HEAD, RW_HEAD, nc), jnp.float32)],
        compiler_params=pltpu.CompilerParams(dimension_semantics=("arbitrary",),
                                             vmem_limit_bytes=VMEM_LIMIT),
        name="rwkv_scan",
    )(a, w_f, b_f, k_f, r, v, a, w_b, b_b, k_b, r, v)


def _rwkv_post_kernel(yf_ref, yb_ref, bonus_ref, g_ref, lw_ref, lb_ref, o_ref):
    y = jnp.concatenate([yf_ref[h] + yb_ref[h] for h in range(RW_HEADS)], axis=-1)
    mean = _head_sum(y) * (1.0 / RW_HEAD)
    c = y - mean
    var = _head_sum(c * c) * (1.0 / RW_HEAD)
    y = c * lax.rsqrt(var + RW_LN_EPS) * lw_ref[...] + lb_ref[...]
    o_ref[...] = (y + bonus_ref[...]) * g_ref[...]


def rwkv_post(y_f, y_b, bonus, g, ln_w, ln_b):
    n = bonus.shape[0]
    row = pl.BlockSpec((ROW_TILE, W_GROUP), lambda i: (i, 0))
    heads = pl.BlockSpec((RW_HEADS, ROW_TILE, RW_HEAD), lambda i: (0, i, 0))
    vec = pl.BlockSpec((1, W_GROUP), lambda i: (0, 0))
    return pl.pallas_call(
        _rwkv_post_kernel,
        grid=(n // ROW_TILE,),
        in_specs=[heads, heads, row, row, vec, vec],
        out_specs=row,
        out_shape=jax.ShapeDtypeStruct((n, W_GROUP), jnp.float32),
        compiler_params=pltpu.CompilerParams(dimension_semantics=("parallel",),
                                             vmem_limit_bytes=VMEM_LIMIT),
        name="rwkv_post",
    )(y_f, y_b, bonus, g, ln_w.reshape(1, W_GROUP), ln_b.reshape(1, W_GROUP))


def rwkv_mixer(z, n_seq, mu, w0, w2, a0, a2, g2, k_k, k_a, r_k, ln_w, ln_b):
    n = z.shape[0]
    L = n // n_seq
    n_chain = n_seq * RW_HEADS
    assert n_chain <= V7X_LANES
    nkk, r, v, dec0, dec1, kd0, kd1, b0, b1, g, bonus = rwkv_prep(z, n_seq, mu, w0, w2, a0, a2, g2, k_k, k_a, r_k)

    def chains(x):
        return x.reshape(RW_HEADS, n_seq, L, RW_HEAD).transpose(2, 3, 0, 1).reshape(L, RW_HEAD, n_chain)

    def tokens(y):
        return y.reshape(L, RW_HEAD, RW_HEADS, n_seq).transpose(2, 3, 0, 1).reshape(RW_HEADS, n, RW_HEAD)

    y_f, y_b = rwkv_scan(chains(nkk), chains(r), chains(v), chains(dec0), chains(b0), chains(kd0),
                         chains(dec1), chains(b1), chains(kd1))
    return rwkv_post(tokens(y_f), tokens(y_b), bonus, g, ln_w, ln_b)


S5_NSTATE = S5_GROUPS * S5_STATE
S5_T = 512


def _cmul(ar, ai, br, bi):
    return ar * br - ai * bi, ar * bi + ai * br


def _s5_operators(lam_re, lam_im, log_dt, b_re, b_im, c_re, c_im):
    dt = jnp.exp(log_dt)[..., None]
    mag = jnp.exp(lam_re * dt)
    ab_re = mag * jnp.cos(lam_im * dt)
    ab_im = mag * jnp.sin(lam_im * dt)
    den = lam_re * lam_re + lam_im * lam_im
    n_re = ab_re - 1.0
    f_re = (n_re * lam_re + ab_im * lam_im) / den
    f_im = (ab_im * lam_re - n_re * lam_im) / den
    bb_re, bb_im = _cmul(f_re[..., None], f_im[..., None], b_re, b_im)
    eye = jnp.eye(S5_GROUPS, dtype=jnp.float32)

    def in_map(bb):
        return jnp.einsum('gh,dhnc->dgchn', eye, bb).reshape(2, W_GROUP, S5_NSTATE)

    def out_map(c):
        return jnp.einsum('hg,dgcn->dhngc', eye, c).reshape(2, S5_NSTATE, W_GROUP)

    bmat = jnp.concatenate([in_map(bb_re), in_map(bb_im)], axis=-1)
    cmat = jnp.concatenate([out_map(c_re), -out_map(c_im)], axis=1)
    lam = jnp.stack([ab_re.reshape(2, S5_NSTATE), ab_im.reshape(2, S5_NSTATE)], axis=1)
    return lam, bmat.astype(jnp.bfloat16), cmat.astype(jnp.bfloat16)


def _s5_scan_kernel(u_ref, lam_ref, bmat_ref, cmat_ref, y_ref, st_ref, carry_ref):
    d = pl.program_id(1)

    @pl.when(pl.program_id(2) == 0)
    def _():
        carry_ref[...] = jnp.zeros_like(carry_ref)

    st_ref[...] = jnp.dot(u_ref[...].astype(jnp.bfloat16), bmat_ref[0], preferred_element_type=jnp.float32)
    lam_r = lam_ref[0, 0:1, :]
    lam_i = lam_ref[0, 1:2, :]
    re = slice(0, S5_NSTATE)
    im = slice(S5_NSTATE, 2 * S5_NSTATE)

    def step(i, carry):
        sr, si = carry
        t = i + d * (S5_T - 1 - 2 * i)
        nr = lam_r * sr - lam_i * si + st_ref[pl.ds(t, 1), re]
        ni = lam_r * si + lam_i * sr + st_ref[pl.ds(t, 1), im]
        st_ref[pl.ds(t, 1), re] = nr
        st_ref[pl.ds(t, 1), im] = ni
        return nr, ni

    sr, si = lax.fori_loop(0, S5_T, step, (carry_ref[0:1, re], carry_ref[0:1, im]), unroll=4)
    carry_ref[0:1, re] = sr
    carry_ref[0:1, im] = si
    y_ref[0] = jnp.dot(st_ref[...].astype(jnp.bfloat16), cmat_ref[0], preferred_element_type=jnp.float32)


def s5_scan(z, lam, bmat, cmat, n_seq):
    n = z.shape[0]
    nch = n // n_seq // S5_T

    def row_block(b, d, c):
        return b * nch + c + d * (nch - 1 - 2 * c)

    return pl.pallas_call(
        _s5_scan_kernel,
        grid=(n_seq, 2, nch),
        in_specs=[pl.BlockSpec((S5_T, W_GROUP), lambda b, d, c: (row_block(b, d, c), 0)),
                  pl.BlockSpec((1, 2, S5_NSTATE), lambda b, d, c: (d, 0, 0)),
                  pl.BlockSpec((1, W_GROUP, 2 * S5_NSTATE), lambda b, d, c: (d, 0, 0)),
                  pl.BlockSpec((1, 2 * S5_NSTATE, W_GROUP), lambda b, d, c: (d, 0, 0))],
        out_specs=pl.BlockSpec((1, S5_T, W_GROUP), lambda b, d, c: (d, row_block(b, d, c), 0)),
        out_shape=jax.ShapeDtypeStruct((2, n, W_GROUP), jnp.float32),
        scratch_shapes=[pltpu.VMEM((S5_T, 2 * S5_NSTATE), jnp.float32),
                        pltpu.VMEM((8, 2 * S5_NSTATE), jnp.float32)],
        compiler_params=pltpu.CompilerParams(dimension_semantics=("arbitrary", "arbitrary", "arbitrary"),
                                             vmem_limit_bytes=VMEM_LIMIT),
        name="s5_scan",
    )(z, lam, bmat, cmat)


def _s5_finish_kernel(y_ref, z_ref, d_ref, w_ref, b_ref, o_ref):
    y = y_ref[0] + y_ref[1] + d_ref[...] * z_ref[...]
    g = jax.nn.gelu(y)
    gate = jnp.dot(g.astype(jnp.bfloat16), w_ref[...], preferred_element_type=jnp.float32) + b_ref[...]
    o_ref[...] = g * jax.nn.sigmoid(gate)


def s5_finish(y2, z, d, glu_w_bf16, glu_b):
    n = z.shape[0]
    vec = pl.BlockSpec((1, W_GROUP), lambda i: (0, 0))
    return pl.pallas_call(
        _s5_finish_kernel,
        grid=(n // ROW_TILE,),
        in_specs=[pl.BlockSpec((2, ROW_TILE, W_GROUP), lambda i: (0, i, 0)),
                  pl.BlockSpec((ROW_TILE, W_GROUP), lambda i: (i, 0)),
                  vec, pl.BlockSpec((W_GROUP, W_GROUP), lambda i: (0, 0)), vec],
        out_specs=pl.BlockSpec((ROW_TILE, W_GROUP), lambda i: (i, 0)),
        out_shape=jax.ShapeDtypeStruct((n, W_GROUP), jnp.float32),
        compiler_params=pltpu.CompilerParams(dimension_semantics=("parallel",),
                                             vmem_limit_bytes=VMEM_LIMIT),
        name="s5_finish",
    )(y2, z, d.reshape(1, W_GROUP), glu_w_bf16, glu_b.reshape(1, W_GROUP))


def s5_mixer(z, n_seq, lam_re, lam_im, log_dt, b_re, b_im, c_re, c_im, d, glu_w, glu_b):
    lam, bmat, cmat = _s5_operators(lam_re, lam_im, log_dt, b_re, b_im, c_re, c_im)
    y2 = s5_scan(z, lam, bmat, cmat, n_seq)
    return s5_finish(y2, z, d, glu_w.astype(jnp.bfloat16), glu_b)


HY_L = 4096
HY_N = 2 * HY_L
HY_N1 = 64
HY_N2 = 128
HY_K1 = HY_N1 // 2 + 1
HY_KB = 11
HY_KP = 24
HY_C = V7X_LANES


def _hyena_dft_tables():
    n1 = np.arange(HY_N1 // 2)
    k1 = np.arange(HY_K1)
    ang1 = 2.0 * np.pi * np.outer(k1, n1) / HY_N1
    weight = np.where((k1 == 0) | (k1 == HY_N1 // 2), 1.0, 2.0) / HY_N
    nblk = HY_K1 // HY_KB
    fa = np.zeros((nblk, HY_KP, HY_N1 // 2), np.float32)
    fi = np.zeros((nblk, HY_N1 // 2, HY_KP), np.float32)
    for b in range(nblk):
        sl = slice(b * HY_KB, (b + 1) * HY_KB)
        fa[b, :HY_KB] = np.cos(ang1[sl])
        fa[b, HY_KB:2 * HY_KB] = -np.sin(ang1[sl])
        fi[b, :, :HY_KB] = (np.cos(ang1[sl]) * weight[sl, None]).T
        fi[b, :, HY_KB:2 * HY_KB] = (-np.sin(ang1[sl]) * weight[sl, None]).T
    n2 = np.arange(HY_N2)
    k = k1[:, None] + HY_N1 * np.arange(HY_N2)[None, :]
    ang = 2.0 * np.pi * (k[:, :, None] * n2[None, None, :] % HY_N) / HY_N
    c, s = np.cos(ang), np.sin(ang)
    g = np.concatenate([np.concatenate([c, s], axis=2), np.concatenate([-s, c], axis=2)], axis=1)
    gi = np.transpose(g, (0, 2, 1))
    bf = jnp.bfloat16
    return (jnp.asarray(fa, bf), jnp.asarray(fi, bf), jnp.asarray(g, bf), jnp.asarray(gi, bf))


def _hyena_filter_spectra(w1, b1, freq, w2, b2, w3, log_rate):
    L = HY_L
    t = jnp.arange(L, dtype=jnp.float32) / L
    ang = 2.0 * math.pi * t[:, None] * jnp.arange(1, HY_BANDS + 1, dtype=jnp.float32)
    feats = jnp.concatenate([t[:, None], jnp.sin(ang), jnp.cos(ang)], axis=-1)
    h = jnp.sin(freq[0] * (feats @ w1 + b1))
    h = jnp.sin(freq[1] * (h @ w2 + b2))
    h = (h @ w3).reshape(L, 2, HY_ORDER, W_GROUP)
    h = h * jnp.exp(-jnp.exp(log_rate)[None] * t[:, None, None, None])
    fwd, bwd = h[:, 0], h[:, 1]
    k = jnp.concatenate([fwd, jnp.zeros_like(fwd[:1]), bwd[:0:-1]], axis=0)
    k = k / jnp.sum(jnp.abs(k), axis=0, keepdims=True)
    kf = jnp.fft.fft(k, axis=0).reshape(HY_N2, HY_N1, HY_ORDER, W_GROUP)[:, :HY_K1]
    kf = jnp.transpose(kf, (2, 1, 0, 3))
    return jnp.stack([jnp.real(kf), jnp.imag(kf)], axis=2).astype(jnp.float32)


def _hy_pre_kernel(z_ref, zp_ref, zn_ref, w_ref, b_ref, v_ref, x1_ref, x2_ref):
    z = z_ref[...]
    zp, zn = _shifted(z, zp_ref[0], zn_ref[0])
    z = w_ref[0:1, :] * zp + w_ref[1:2, :] * z + w_ref[2:3, :] * zn + b_ref[...]
    v_ref[...] = z[:, 0:W_GROUP]
    x1_ref[...] = z[:, W_GROUP:2 * W_GROUP]
    x2_ref[...] = z[:, 2 * W_GROUP:3 * W_GROUP]


def hy_pre(z, n_seq, conv_w, conv_b):
    n = z.shape[0]
    prev, nxt = _halo_rows(z, n_seq, ROW_TILE)
    row = pl.BlockSpec((ROW_TILE, W_GROUP), lambda i: (i, 0))
    halo = pl.BlockSpec((1, 1, HY_IN), lambda i: (i, 0, 0))
    return pl.pallas_call(
        _hy_pre_kernel,
        grid=(n // ROW_TILE,),
        in_specs=[pl.BlockSpec((ROW_TILE, HY_IN), lambda i: (i, 0)), halo, halo,
                  pl.BlockSpec((3, HY_IN), lambda i: (0, 0)), pl.BlockSpec((1, HY_IN), lambda i: (0, 0))],
        out_specs=[row] * 3,
        out_shape=[jax.ShapeDtypeStruct((n, W_GROUP), jnp.float32)] * 3,
        compiler_params=pltpu.CompilerParams(dimension_semantics=("parallel",),
                                             vmem_limit_bytes=VMEM_LIMIT),
        name="hy_pre",
    )(z, prev, nxt, conv_w, conv_b.reshape(1, HY_IN))


def _hy_conv_kernel(u_ref, fa_ref, fi_ref, g_ref, gi_ref, kf_ref, o_ref, as_ref, bs_ref):
    kb = pl.program_id(2)
    bf = jnp.bfloat16
    fa = fa_ref[0]
    fi = fi_ref[0]
    for n2 in range(HY_N2):
        rows = u_ref[pl.ds(n2, HY_N1 // 2, stride=HY_N2), :].astype(bf)
        as_ref[n2] = jnp.dot(fa, rows, preferred_element_type=jnp.float32)
    bs_ref[:, 2 * HY_KB:, :] = jnp.zeros((HY_N2, HY_KP - 2 * HY_KB, HY_C), jnp.float32)
    for j in range(HY_KB):
        a = jnp.concatenate([as_ref[:, j, :], as_ref[:, HY_KB + j, :]], axis=0).astype(bf)
        x = jnp.dot(g_ref[j], a, preferred_element_type=jnp.float32)
        xr, xi = x[:HY_N2], x[HY_N2:]
        kr, ki = kf_ref[0, j, 0], kf_ref[0, j, 1]
        y = jnp.concatenate([xr * kr - xi * ki, xr * ki + xi * kr], axis=0).astype(bf)
        b = jnp.dot(gi_ref[j], y, preferred_element_type=jnp.float32)
        bs_ref[:, j, :] = b[:HY_N2]
        bs_ref[:, HY_KB + j, :] = b[HY_N2:]
    @pl.when(kb == 0)
    def _():
        o_ref[...] = jnp.zeros_like(o_ref)

    for n2 in range(HY_N2):
        part = jnp.dot(fi, bs_ref[n2].astype(bf), preferred_element_type=jnp.float32)
        dst = pl.ds(n2, HY_N1 // 2, stride=HY_N2)
        o_ref[dst, :] = o_ref[dst, :] + part


def hy_conv(u, n_seq, kf, tables):
    n = u.shape[0]
    assert n // n_seq == HY_L
    fa, fi, g, gi = tables
    nblk = HY_K1 // HY_KB
    seq = pl.BlockSpec((HY_L, HY_C), lambda b, c, k: (b, c))
    return pl.pallas_call(
        _hy_conv_kernel,
        grid=(n_seq, W_GROUP // HY_C, nblk),
        in_specs=[seq,
                  pl.BlockSpec((1, HY_KP, HY_N1 // 2), lambda b, c, k: (k, 0, 0)),
                  pl.BlockSpec((1, HY_N1 // 2, HY_KP), lambda b, c, k: (k, 0, 0)),
                  pl.BlockSpec((HY_KB, 2 * HY_N2, 2 * HY_N2), lambda b, c, k: (k, 0, 0)),
                  pl.BlockSpec((HY_KB, 2 * HY_N2, 2 * HY_N2), lambda b, c, k: (k, 0, 0)),
                  pl.BlockSpec((1, HY_KB, 2, HY_N2, HY_C), lambda b, c, k: (0, k, 0, 0, c))],
        out_specs=seq,
        out_shape=jax.ShapeDtypeStruct((n, W_GROUP), jnp.float32),
        scratch_shapes=[pltpu.VMEM((HY_N2, HY_KP, HY_C), jnp.float32),
                        pltpu.VMEM((HY_N2, HY_KP, HY_C), jnp.float32)],
        compiler_params=pltpu.CompilerParams(dimension_semantics=("arbitrary", "arbitrary", "arbitrary"),
                                             vmem_limit_bytes=VMEM_LIMIT),
        name="hy_conv",
    )(u, fa, fi, g, gi, kf)


def _hy_gate_kernel(c_ref, u_ref, x_ref, d_ref, o_ref):
    o_ref[...] = x_ref[...] * (c_ref[...] + d_ref[...] * u_ref[...])


def hy_gate(c, u, gate, d):
    n = u.shape[0]
    row = pl.BlockSpec((ROW_TILE, W_GROUP), lambda i: (i, 0))
    return pl.pallas_call(
        _hy_gate_kernel,
        grid=(n // ROW_TILE,),
        in_specs=[row, row, row, pl.BlockSpec((1, W_GROUP), lambda i: (0, 0))],
        out_specs=row,
        out_shape=jax.ShapeDtypeStruct((n, W_GROUP), jnp.float32),
        compiler_params=pltpu.CompilerParams(dimension_semantics=("parallel",),
                                             vmem_limit_bytes=VMEM_LIMIT),
        name="hy_gate",
    )(c, u, gate, d.reshape(1, W_GROUP))


def hyena_mixer(z, n_seq, conv_w, conv_b, f_w1, f_b1, f_freq, f_w2, f_b2, f_w3, log_rate, d):
    v, x1, x2 = hy_pre(z, n_seq, conv_w, conv_b)
    kf = _hyena_filter_spectra(f_w1, f_b1, f_freq, f_w2, f_b2, f_w3, log_rate)
    tables = _hyena_dft_tables()
    u = v
    for o, gate in enumerate((x1, x2)):
        u = hy_gate(hy_conv(u, n_seq, kf[o:o + 1], tables), u, gate, d[o])
    return u


NA_BAND = 8
NA_KROWS = 2 * NA_BAND
NA_KBLK = 4


def _na_bias_tables(rel_bias, rows):
    hi = lax.Precision.HIGHEST
    qc = np.arange(GRID_W)
    ws = np.clip(qc - NA_WIN_C // 2, 0, GRID_W - NA_WIN_C)
    col_ok = (qc[None, :] >= ws[:, None]) & (qc[None, :] < ws[:, None] + NA_WIN_C)
    dc = np.clip(qc[None, :] - qc[:, None] + NA_WIN_C - 1, 0, 2 * NA_WIN_C - 2)
    dc_sel = (dc[..., None] == np.arange(2 * NA_WIN_C - 1)).astype(np.float32)
    col_bias = jnp.einsum('qkc,hdc->hdqk', dc_sel, rel_bias, precision=hi)
    tabs = []
    for band in (0, 1, rows // NA_BAND - 1):
        kb = int(np.clip(NA_BAND * band - NA_WIN_R // 2, 0, rows - NA_KROWS))
        r = NA_BAND * band + np.arange(NA_BAND)
        rs = np.clip(r - NA_WIN_R // 2, 0, rows - NA_WIN_R)
        krow = kb + np.arange(NA_KROWS)
        row_ok = (krow[None, :] >= rs[:, None]) & (krow[None, :] < rs[:, None] + NA_WIN_R)
        dr = np.clip(krow[None, :] - r[:, None] + NA_WIN_R - 1, 0, 2 * NA_WIN_R - 2)
        dr_sel = (dr[..., None] == np.arange(2 * NA_WIN_R - 1)).astype(np.float32)
        bias = jnp.einsum('jwd,hdqk->hjqwk', dr_sel, col_bias, precision=hi)
        ok = row_ok[:, None, :, None] & col_ok[None, :, None, :]
        tab = jnp.where(jnp.asarray(ok)[None], bias, NEG_INF)
        tabs.append(tab.reshape(NA_HEADS, NA_BAND * GRID_W, NA_KROWS * GRID_W))
    return jnp.stack(tabs)


def _na_kernel(q_ref, k0_ref, k1_ref, k2_ref, k3_ref, v0_ref, v1_ref, v2_ref, v3_ref, qg_ref, kg_ref,
               tab_ref, o_ref):
    q = q_ref[...]
    k = jnp.concatenate([k0_ref[...], k1_ref[...], k2_ref[...], k3_ref[...]], axis=0)
    v = jnp.concatenate([v0_ref[...], v1_ref[...], v2_ref[...], v3_ref[...]], axis=0)
    for h in range(NA_HEADS):
        sl = slice(h * NA_HEAD, (h + 1) * NA_HEAD)
        qh = (_rms(q[:, sl]) * qg_ref[...] * (NA_HEAD ** -0.5)).astype(jnp.bfloat16)
        kh = (_rms(k[:, sl]) * kg_ref[...]).astype(jnp.bfloat16)
        s = lax.dot_general(qh, kh, (((1,), (1,)), ((), ())), preferred_element_type=jnp.float32)
        s = s + tab_ref[0, h]
        p = jnp.exp(s - jnp.max(s, axis=-1, keepdims=True))
        den = jnp.sum(p, axis=-1, keepdims=True)
        o = jnp.dot(p.astype(jnp.bfloat16), v[:, sl].astype(jnp.bfloat16), preferred_element_type=jnp.float32)
        o_ref[:, sl] = o / den


def na_mixer(q, k, v, n_seq, q_g, k_g, rel_bias):
    n = q.shape[0]
    rows = n // n_seq // GRID_W
    n_band = rows // NA_BAND
    tab = _na_bias_tables(rel_bias, rows)
    qtok = NA_BAND * GRID_W
    ktok = NA_KBLK * GRID_W
    kblk_per_seq = rows // NA_KBLK

    def kv_spec(j):
        def index(i, b):
            first = jnp.clip(NA_BAND // NA_KBLK * i - 1, 0, kblk_per_seq - NA_KROWS // NA_KBLK)
            return (b * kblk_per_seq + first + j, 0)
        return pl.BlockSpec((ktok, W_GROUP), index)

    def tab_index(i, b):
        return (jnp.where(i == 0, 0, jnp.where(i == n_band - 1, 2, 1)), 0, 0, 0)

    qspec = pl.BlockSpec((qtok, W_GROUP), lambda i, b: (b * n_band + i, 0))
    gspec = pl.BlockSpec((1, NA_HEAD), lambda i, b: (0, 0))
    kvs = [kv_spec(j) for j in range(NA_KROWS // NA_KBLK)]
    return pl.pallas_call(
        _na_kernel,
        grid=(n_band, n_seq),
        in_specs=[qspec] + kvs + kvs + [gspec, gspec,
                  pl.BlockSpec((1, NA_HEADS, qtok, NA_KROWS * GRID_W), tab_index)],
        out_specs=qspec,
        out_shape=jax.ShapeDtypeStruct((n, W_GROUP), jnp.float32),
        compiler_params=pltpu.CompilerParams(dimension_semantics=("arbitrary", "arbitrary"),
                                             vmem_limit_bytes=VMEM_LIMIT),
        name="na_attn",
    )(q, k, k, k, k, v, v, v, v, q_g.reshape(1, NA_HEAD), k_g.reshape(1, NA_HEAD), tab)


def _trunk(x, p):
    Bn, L, _ = x.shape
    n = Bn * L
    x = x.reshape(n, D_MODEL)
    bf = jnp.bfloat16
    for l in range(DEPTH):
        z_s5, z_hy, z_rw, z_q, z_k, z_v = in_proj(x, p['ln1_g'][l], p['w_in'][l].astype(bf))
        y_s5 = s5_mixer(z_s5, Bn, p['s5_lam_re'][l], p['s5_lam_im'][l], p['s5_log_dt'][l],
                        p['s5_b_re'][l], p['s5_b_im'][l], p['s5_c_re'][l], p['s5_c_im'][l], p['s5_d'][l],
                        p['s5_glu_w'][l], p['s5_glu_b'][l])
        y_hy = hyena_mixer(z_hy, Bn, p['hy_conv_w'][l], p['hy_conv_b'][l], p['hy_f_w1'][l],
                           p['hy_f_b1'][l], p['hy_f_freq'][l], p['hy_f_w2'][l], p['hy_f_b2'][l],
                           p['hy_f_w3'][l], p['hy_log_rate'][l], p['hy_d'][l])
        y_rw = rwkv_mixer(z_rw, Bn, p['rw_mu'][l], p['rw_w0'][l], p['rw_w2'][l],
                          p['rw_a0'][l], p['rw_a2'][l], p['rw_g2'][l], p['rw_k_k'][l], p['rw_k_a'][l],
                          p['rw_r_k'][l], p['rw_ln_w'][l], p['rw_ln_b'][l])
        y_na = na_mixer(z_q, z_k, z_v, Bn, p['na_q_g'][l], p['na_k_g'][l], p['na_rel_bias'][l])
        ys = [y_s5, y_hy, y_rw, y_na]
        x = out_proj(x, ys, p['grp_g'][l], p['w_out'][l].astype(bf))
        x = mlp(x, p['ln2_g'][l], p['w_mlp1'][l].astype(bf), p['w_mlp2'][l].astype(bf))
    return x.reshape(Bn, L, D_MODEL)


def kernel(x_prompt, x_sample, ln1_g, w_in, s5_lam_re, s5_lam_im, s5_log_dt, s5_b_re, s5_b_im,
           s5_c_re, s5_c_im, s5_d, s5_glu_w, s5_glu_b, hy_conv_w, hy_conv_b, hy_f_w1, hy_f_b1,
           hy_f_freq, hy_f_w2, hy_f_b2, hy_f_w3, hy_log_rate, hy_d, rw_mu, rw_w0, rw_w2, rw_a0,
           rw_a2, rw_g2, rw_k_k, rw_k_a, rw_r_k, rw_ln_w, rw_ln_b, na_q_g, na_k_g, na_rel_bias,
           grp_g, w_out, ln2_g, w_mlp1, w_mlp2):
    p = dict(ln1_g=ln1_g, w_in=w_in, s5_lam_re=s5_lam_re, s5_lam_im=s5_lam_im, s5_log_dt=s5_log_dt,
             s5_b_re=s5_b_re, s5_b_im=s5_b_im, s5_c_re=s5_c_re, s5_c_im=s5_c_im, s5_d=s5_d,
             s5_glu_w=s5_glu_w, s5_glu_b=s5_glu_b, hy_conv_w=hy_conv_w, hy_conv_b=hy_conv_b,
             hy_f_w1=hy_f_w1, hy_f_b1=hy_f_b1, hy_f_freq=hy_f_freq, hy_f_w2=hy_f_w2, hy_f_b2=hy_f_b2,
             hy_f_w3=hy_f_w3, hy_log_rate=hy_log_rate, hy_d=hy_d, rw_mu=rw_mu, rw_w0=rw_w0, rw_w2=rw_w2,
             rw_a0=rw_a0, rw_a2=rw_a2, rw_g2=rw_g2, rw_k_k=rw_k_k, rw_k_a=rw_k_a, rw_r_k=rw_r_k,
             rw_ln_w=rw_ln_w, rw_ln_b=rw_ln_b, na_q_g=na_q_g, na_k_g=na_k_g, na_rel_bias=na_rel_bias,
             grp_g=grp_g, w_out=w_out, ln2_g=ln2_g, w_mlp1=w_mlp1, w_mlp2=w_mlp2)
    nb = x_prompt.shape[0]
    y = _trunk(jnp.concatenate([x_prompt, x_sample], axis=0), p)
    return (y[:nb], y[nb:])
```

```python
import math

import jax
import jax.numpy as jnp
import numpy as np
from jax import lax
from jax.experimental import pallas as pl
from jax.experimental.pallas import tpu as pltpu

D_MODEL = 1024
DEPTH = 4
GRID_W = 64
W_GROUP = 256
N_MIXERS = 4
D_FF = 4 * D_MODEL
NORM_EPS = 1e-6

S5_CH = 16
S5_GROUPS = W_GROUP // S5_CH
S5_STATE = 64
S5_IN = W_GROUP

HY_ORDER = 2
HY_BANDS = 8
HY_IN = (HY_ORDER + 1) * W_GROUP

RW_HEAD = 64
RW_HEADS = W_GROUP // RW_HEAD
RW_DECAY_RANK = 64
RW_A_RANK = 64
RW_G_RANK = 128
RW_LN_EPS = 64e-5
RW_IN = 3 * W_GROUP + RW_DECAY_RANK + RW_A_RANK + RW_G_RANK
RW_SPLITS = (W_GROUP, 2 * W_GROUP, 3 * W_GROUP, 3 * W_GROUP + RW_DECAY_RANK,
             3 * W_GROUP + RW_DECAY_RANK + RW_A_RANK)

NA_HEAD = 64
NA_HEADS = W_GROUP // NA_HEAD
NA_WIN_R = 8
NA_WIN_C = 16
NEG_INF = -1e30

D_IN = S5_IN + HY_IN + RW_IN + 3 * W_GROUP

V7X_LANES = 128
VMEM_LIMIT = 48 * 1024 * 1024

ROW_TILE = 512
FF_TILE = 1024
SCAN_T = 16
SCAN_UNROLL = 32


def _rms(x):
    return x * lax.rsqrt(jnp.mean(x * x, axis=-1, keepdims=True) + NORM_EPS)


IN_WIDTHS = (S5_IN, HY_IN, RW_IN, W_GROUP, W_GROUP, W_GROUP)


def _in_proj_kernel(x_ref, g_ref, w_ref, *o_refs):
    h = (_rms(x_ref[...]) * g_ref[...]).astype(jnp.bfloat16)
    lo = 0
    for o_ref, width in zip(o_refs, IN_WIDTHS):
        o_ref[...] = jnp.dot(h, w_ref[:, lo:lo + width], preferred_element_type=jnp.float32)
        lo += width


def in_proj(x, g, w_bf16):
    n = x.shape[0]
    return pl.pallas_call(
        _in_proj_kernel,
        grid=(n // ROW_TILE,),
        in_specs=[pl.BlockSpec((ROW_TILE, D_MODEL), lambda i: (i, 0)),
                  pl.BlockSpec((1, D_MODEL), lambda i: (0, 0)),
                  pl.BlockSpec((D_MODEL, D_IN), lambda i: (0, 0))],
        out_specs=[pl.BlockSpec((ROW_TILE, w), lambda i: (i, 0)) for w in IN_WIDTHS],
        out_shape=[jax.ShapeDtypeStruct((n, w), jnp.float32) for w in IN_WIDTHS],
        compiler_params=pltpu.CompilerParams(dimension_semantics=("parallel",),
                                             vmem_limit_bytes=VMEM_LIMIT),
        name="in_proj",
    )(x, g.reshape(1, D_MODEL), w_bf16)


def _out_proj_kernel(x_ref, y0_ref, y1_ref, y2_ref, y3_ref, g_ref, w_ref, o_ref):
    acc = x_ref[...]
    for i, y_ref in enumerate((y0_ref, y1_ref, y2_ref, y3_ref)):
        n = (_rms(y_ref[...]) * g_ref[i:i + 1, :]).astype(jnp.bfloat16)
        acc = acc + jnp.dot(n, w_ref[i * W_GROUP:(i + 1) * W_GROUP, :],
                            preferred_element_type=jnp.float32)
    o_ref[...] = acc


def out_proj(x, ys, g, w_bf16):
    n = x.shape[0]
    row = lambda w: pl.BlockSpec((ROW_TILE, w), lambda i: (i, 0))
    return pl.pallas_call(
        _out_proj_kernel,
        grid=(n // ROW_TILE,),
        in_specs=[row(D_MODEL)] + [row(W_GROUP)] * N_MIXERS
                 + [pl.BlockSpec((N_MIXERS, W_GROUP), lambda i: (0, 0)),
                    pl.BlockSpec((D_MODEL, D_MODEL), lambda i: (0, 0))],
        out_specs=row(D_MODEL),
        out_shape=jax.ShapeDtypeStruct((n, D_MODEL), jnp.float32),
        compiler_params=pltpu.CompilerParams(dimension_semantics=("parallel",),
                                             vmem_limit_bytes=VMEM_LIMIT),
        name="out_proj",
    )(x, *ys, g, w_bf16)


def _mlp_kernel(x_ref, g_ref, w1_ref, w2_ref, o_ref, h_ref):
    j = pl.program_id(1)

    @pl.when(j == 0)
    def _():
        x = x_ref[...]
        h_ref[...] = (_rms(x) * g_ref[...]).astype(jnp.bfloat16)
        o_ref[...] = x

    a = jnp.dot(h_ref[...], w1_ref[...], preferred_element_type=jnp.float32)
    a = jnp.square(jnp.maximum(a, 0.0)).astype(jnp.bfloat16)
    o_ref[...] += jnp.dot(a, w2_ref[...], preferred_element_type=jnp.float32)


def mlp(x, g, w1_bf16, w2_bf16):
    n = x.shape[0]
    return pl.pallas_call(
        _mlp_kernel,
        grid=(n // ROW_TILE, D_FF // FF_TILE),
        in_specs=[pl.BlockSpec((ROW_TILE, D_MODEL), lambda i, j: (i, 0)),
                  pl.BlockSpec((1, D_MODEL), lambda i, j: (0, 0)),
                  pl.BlockSpec((D_MODEL, FF_TILE), lambda i, j: (0, j)),
                  pl.BlockSpec((FF_TILE, D_MODEL), lambda i, j: (j, 0))],
        out_specs=pl.BlockSpec((ROW_TILE, D_MODEL), lambda i, j: (i, 0)),
        out_shape=jax.ShapeDtypeStruct((n, D_MODEL), jnp.float32),
        scratch_shapes=[pltpu.VMEM((ROW_TILE, D_MODEL), jnp.bfloat16)],
        compiler_params=pltpu.CompilerParams(dimension_semantics=("parallel", "arbitrary"),
                                             vmem_limit_bytes=VMEM_LIMIT),
        name="mlp",
    )(x, g.reshape(1, D_MODEL), w1_bf16, w2_bf16)


def _head_sum(x):
    lane = lax.broadcasted_iota(jnp.int32, (W_GROUP, W_GROUP), 0) // RW_HEAD
    col = lax.broadcasted_iota(jnp.int32, (W_GROUP, W_GROUP), 1) // RW_HEAD
    ones = (lane == col).astype(jnp.float32)
    return jnp.dot(x, ones, precision=lax.Precision.HIGHEST, preferred_element_type=jnp.float32)


def _softplus(x):
    return jnp.maximum(x, 0.0) + jnp.log(1.0 + jnp.exp(-jnp.abs(x)))


def _shifted(z, prev_row, next_row):
    t = z.shape[0]
    row = lax.broadcasted_iota(jnp.int32, z.shape, 0)
    zp = jnp.where(row == 0, prev_row, pltpu.roll(z, 1, 0))
    zn = jnp.where(row == t - 1, next_row, pltpu.roll(z, t - 1, 0))
    return zp, zn


def _store_heads(o_ref, x):
    for h in range(RW_HEADS):
        o_ref[h] = x[:, h * RW_HEAD:(h + 1) * RW_HEAD]


def _rwkv_prep_kernel(z_ref, zp_ref, zn_ref, mu_ref, w0_ref, w2_ref, a0_ref, a2_ref, g2_ref, kk_ref, ka_ref,
                      rk_ref, nkk_ref, r_ref, v_ref, dec0_ref, dec1_ref, kd0_ref, kd1_ref, b0_ref, b1_ref,
                      g_ref, bonus_ref):
    z = z_ref[...]
    zp, zn = _shifted(z, zp_ref[0], zn_ref[0])
    z = z + mu_ref[...] * (0.5 * (zp + zn) - z)
    r = z[:, 0:W_GROUP]
    k = z[:, W_GROUP:2 * W_GROUP]
    v = z[:, 2 * W_GROUP:3 * W_GROUP]
    wd = z[:, RW_SPLITS[2]:RW_SPLITS[3]]
    ad = z[:, RW_SPLITS[3]:RW_SPLITS[4]]
    gd = z[:, RW_SPLITS[4]:RW_IN]
    bf = jnp.bfloat16
    g_ref[...] = jnp.dot(jax.nn.sigmoid(gd).astype(bf), g2_ref[...], preferred_element_type=jnp.float32)
    kk = k * kk_ref[...]
    kk = kk / jnp.maximum(jnp.sqrt(_head_sum(kk * kk)), 1e-12)
    _store_heads(nkk_ref, -kk)
    _store_heads(r_ref, r)
    _store_heads(v_ref, v)
    bonus_ref[...] = _head_sum(r * k * rk_ref[...]) * v
    tw = jnp.tanh(wd).astype(bf)
    adb = ad.astype(bf)
    for d, (dec_ref, kd_ref, b_ref) in enumerate(((dec0_ref, kd0_ref, b0_ref), (dec1_ref, kd1_ref, b1_ref))):
        w = w0_ref[d:d + 1, :] + jnp.dot(tw, w2_ref[d], preferred_element_type=jnp.float32)
        w = -_softplus(-w) - 0.5
        _store_heads(dec_ref, jnp.exp(-jnp.exp(w)))
        a = jax.nn.sigmoid(a0_ref[d:d + 1, :] + jnp.dot(adb, a2_ref[d], preferred_element_type=jnp.float32))
        _store_heads(kd_ref, k * (1.0 + (a - 1.0) * ka_ref[...]))
        _store_heads(b_ref, kk * a)


def _halo_rows(z, n_seq, tile):
    n, c = z.shape
    zt = z.reshape(n_seq, n // n_seq // tile, tile, c)
    zero = jnp.zeros((n_seq, 1, c), z.dtype)
    prev = jnp.concatenate([zero, zt[:, :-1, -1]], axis=1).reshape(n // tile, 1, c)
    nxt = jnp.concatenate([zt[:, 1:, 0], zero], axis=1).reshape(n // tile, 1, c)
    return prev, nxt


def rwkv_prep(z, n_seq, mu, w0, w2, a0, a2, g2, k_k, k_a, r_k):
    n = z.shape[0]
    prev, nxt = _halo_rows(z, n_seq, ROW_TILE)
    bf = jnp.bfloat16
    full = lambda *s: pl.BlockSpec(s, lambda i: (0,) * len(s))
    row = pl.BlockSpec((ROW_TILE, W_GROUP), lambda i: (i, 0))
    heads = pl.BlockSpec((RW_HEADS, ROW_TILE, RW_HEAD), lambda i: (0, i, 0))
    halo = pl.BlockSpec((1, 1, RW_IN), lambda i: (i, 0, 0))
    vec = lambda x: x.reshape(1, -1)
    return pl.pallas_call(
        _rwkv_prep_kernel,
        grid=(n // ROW_TILE,),
        in_specs=[pl.BlockSpec((ROW_TILE, RW_IN), lambda i: (i, 0)), halo, halo, full(1, RW_IN),
                  full(2, W_GROUP), full(2, RW_DECAY_RANK, W_GROUP), full(2, W_GROUP),
                  full(2, RW_A_RANK, W_GROUP), full(RW_G_RANK, W_GROUP), full(1, W_GROUP), full(1, W_GROUP),
                  full(1, W_GROUP)],
        out_specs=[heads] * 9 + [row] * 2,
        out_shape=[jax.ShapeDtypeStruct((RW_HEADS, n, RW_HEAD), jnp.float32)] * 9
                  + [jax.ShapeDtypeStruct((n, W_GROUP), jnp.float32)] * 2,
        compiler_params=pltpu.CompilerParams(dimension_semantics=("parallel",),
                                             vmem_limit_bytes=VMEM_LIMIT),
        name="rwkv_prep",
    )(z, prev, nxt, vec(mu), w0, w2.astype(bf), a0, a2.astype(bf), g2.astype(bf), vec(k_k), vec(k_a), vec(r_k))


RW_NA = 16
RW_NB = 4
RW_KLO = 4
RW_KHI = RW_HEAD // RW_KLO
RW_HALF = V7X_LANES // 2
RW_GRP = V7X_LANES // RW_KLO
assert RW_NA * RW_HEADS == RW_HALF and 2 * RW_NB * RW_HEADS == RW_GRP


def _rwkv_scan_kernel(*refs):
    ins, outs, scratch = refs[:24], refs[24:28], refs[28:]
    a_tile = [(ins[2 * i], ins[2 * i + 1]) for i in range(6)]
    b_tile = [(ins[12 + 2 * i], ins[13 + 2 * i]) for i in range(6)]
    ya_f, ya_b, yb_f, yb_b = outs
    sa_ref, sb_ref, opa_ref, opb_ref, va_ref, vb_ref = scratch

    @pl.when(pl.program_id(0) == 0)
    def _():
        sa_ref[...] = jnp.zeros_like(sa_ref)
        sb_ref[...] = jnp.zeros_like(sb_ref)

    def fwd_lanes(rows, group):
        return lax.broadcasted_iota(jnp.int32, (rows, V7X_LANES), 1) % group < group // 2

    for tile, op_ref, v_ref, rows, group in ((a_tile, opa_ref, va_ref, RW_HEAD, V7X_LANES),
                                             (b_tile, opb_ref, vb_ref, RW_KHI, RW_GRP)):
        for x, (f_ref, b_ref) in enumerate(tile):
            is_v = x == 5
            mask = fwd_lanes(RW_HEAD if is_v else rows, group)
            for tt in range(SCAN_T):
                mix = jnp.where(mask, f_ref[tt], b_ref[SCAN_T - 1 - tt])
                if is_v:
                    v_ref[tt] = mix
                else:
                    op_ref[x, tt] = mix

    def tile_step(s_ref, op_ref, vt, t, rows, k_lo_groups):
        unroll = min(rows, SCAN_UNROLL)

        def reduce_a(k, sa):
            return sa + s_ref[k] * op_ref[0, t, pl.ds(k, 1), :]

        sa = lax.fori_loop(0, rows, reduce_a, jnp.zeros_like(vt), unroll=unroll)
        if k_lo_groups:
            sa = sa + pltpu.roll(sa, 2 * RW_GRP, 1)
            sa = sa + pltpu.roll(sa, RW_GRP, 1)

        def update(k, y):
            s = (s_ref[k] * op_ref[1, t, pl.ds(k, 1), :] + sa * op_ref[2, t, pl.ds(k, 1), :]
                 + vt * op_ref[3, t, pl.ds(k, 1), :])
            s_ref[k] = s
            return y + s * op_ref[4, t, pl.ds(k, 1), :]

        return lax.fori_loop(0, rows, update, jnp.zeros_like(vt), unroll=unroll)

    def step(t, carry):
        y = tile_step(sb_ref, opb_ref, vb_ref[t], t, RW_KHI, True)
        yb_f[t] = y
        yb_b[SCAN_T - 1 - t] = y
        y = tile_step(sa_ref, opa_ref, va_ref[t], t, RW_HEAD, False)
        ya_f[t] = y
        ya_b[SCAN_T - 1 - t] = y
        return carry

    lax.fori_loop(0, SCAN_T, step, 0)


def rwkv_scan(a_ops, b_ops):
    L = a_ops[0].shape[0]
    nblk = L // SCAN_T

    def specs(x):
        blk = (SCAN_T,) + x.shape[1:]
        return [pl.BlockSpec(blk, lambda i: (i, 0, 0)), pl.BlockSpec(blk, lambda i: (nblk - 1 - i, 0, 0))]

    in_specs, args = [], []
    for x in list(a_ops) + list(b_ops):
        in_specs += specs(x)
        args += [x, x]
    out = jax.ShapeDtypeStruct((L, RW_HEAD, V7X_LANES), jnp.float32)
    return pl.pallas_call(
        _rwkv_scan_kernel,
        grid=(nblk,),
        in_specs=in_specs,
        out_specs=specs(out) + specs(out),
        out_shape=[out] * 4,
        scratch_shapes=[pltpu.VMEM((RW_HEAD, RW_HEAD, V7X_LANES), jnp.float32),
                        pltpu.VMEM((RW_KHI, RW_HEAD, V7X_LANES), jnp.float32),
                        pltpu.VMEM((5, SCAN_T, RW_HEAD, V7X_LANES), jnp.float32),
                        pltpu.VMEM((5, SCAN_T, RW_KHI, V7X_LANES), jnp.float32),
                        pltpu.VMEM((SCAN_T, RW_HEAD, V7X_LANES), jnp.float32),
                        pltpu.VMEM((SCAN_T, RW_HEAD, V7X_LANES), jnp.float32)],
        compiler_params=pltpu.CompilerParams(dimension_semantics=("arbitrary",),
                                             vmem_limit_bytes=VMEM_LIMIT),
        name="rwkv_scan",
    )(*args)


def _rwkv_post_kernel(yf_ref, yb_ref, bonus_ref, g_ref, lw_ref, lb_ref, o_ref):
    y = jnp.concatenate([yf_ref[h] + yb_ref[h] for h in range(RW_HEADS)], axis=-1)
    mean = _head_sum(y) * (1.0 / RW_HEAD)
    c = y - mean
    var = _head_sum(c * c) * (1.0 / RW_HEAD)
    y = c * lax.rsqrt(var + RW_LN_EPS) * lw_ref[...] + lb_ref[...]
    o_ref[...] = (y + bonus_ref[...]) * g_ref[...]


def rwkv_post(y_f, y_b, bonus, g, ln_w, ln_b):
    n = bonus.shape[0]
    row = pl.BlockSpec((ROW_TILE, W_GROUP), lambda i: (i, 0))
    heads = pl.BlockSpec((RW_HEADS, ROW_TILE, RW_HEAD), lambda i: (0, i, 0))
    vec = pl.BlockSpec((1, W_GROUP), lambda i: (0, 0))
    return pl.pallas_call(
        _rwkv_post_kernel,
        grid=(n // ROW_TILE,),
        in_specs=[heads, heads, row, row, vec, vec],
        out_specs=row,
        out_shape=jax.ShapeDtypeStruct((n, W_GROUP), jnp.float32),
        compiler_params=pltpu.CompilerParams(dimension_semantics=("parallel",),
                                             vmem_limit_bytes=VMEM_LIMIT),
        name="rwkv_post",
    )(y_f, y_b, bonus, g, ln_w.reshape(1, W_GROUP), ln_b.reshape(1, W_GROUP))


def rwkv_mixer(z, n_seq, mu, w0, w2, a0, a2, g2, k_k, k_a, r_k, ln_w, ln_b):
    n = z.shape[0]
    L = n // n_seq
    assert n_seq == RW_NB + RW_NA
    nkk, r, v, dec0, dec1, kd0, kd1, b0, b1, g, bonus = rwkv_prep(z, n_seq, mu, w0, w2, a0, a2, g2, k_k, k_a, r_k)

    def seqs(x):
        return x.reshape(RW_HEADS, n_seq, L, RW_HEAD)

    def tile_a(x0, x1):
        x = jnp.stack([seqs(x0)[:, RW_NB:], seqs(x1)[:, RW_NB:]])
        return x.transpose(3, 4, 0, 1, 2).reshape(L, RW_HEAD, V7X_LANES)

    def tile_b(x0, x1):
        x = jnp.stack([seqs(x0)[:, :RW_NB], seqs(x1)[:, :RW_NB]])
        return x.transpose(3, 4, 0, 1, 2).reshape(L, RW_KHI, V7X_LANES)

    def tile_b_v(x):
        x = seqs(x)[:, :RW_NB].transpose(2, 3, 0, 1).reshape(L, RW_HEAD, 1, RW_NB * RW_HEADS)
        return jnp.broadcast_to(x, (L, RW_HEAD, 2 * RW_KLO, RW_NB * RW_HEADS)).reshape(L, RW_HEAD, V7X_LANES)

    a_ops = (tile_a(nkk, nkk), tile_a(dec0, dec1), tile_a(b0, b1), tile_a(kd0, kd1), tile_a(r, r), tile_a(v, v))
    b_ops = (tile_b(nkk, nkk), tile_b(dec0, dec1), tile_b(b0, b1), tile_b(kd0, kd1), tile_b(r, r), tile_b_v(v))
    ya_f, ya_b, yb_f, yb_b = rwkv_scan(a_ops, b_ops)

    def tokens(ya, yb, d):
        ya = ya.reshape(L, RW_HEAD, 2, RW_HEADS, RW_NA)[:, :, d].transpose(2, 3, 0, 1)
        yb = yb.reshape(L, RW_HEAD, RW_KLO, 2, RW_HEADS, RW_NB)[:, :, :, d].sum(axis=2)
        yb = yb.transpose(2, 3, 0, 1)
        return jnp.concatenate([yb, ya], axis=1).reshape(RW_HEADS, n, RW_HEAD)

    return rwkv_post(tokens(ya_f, yb_f, 0), tokens(ya_b, yb_b, 1), bonus, g, ln_w, ln_b)


S5_NSTATE = S5_GROUPS * S5_STATE
S5_T = 512


def _cmul(ar, ai, br, bi):
    return ar * br - ai * bi, ar * bi + ai * br


def _s5_operators(lam_re, lam_im, log_dt, b_re, b_im, c_re, c_im):
    dt = jnp.exp(log_dt)[..., None]
    mag = jnp.exp(lam_re * dt)
    ab_re = mag * jnp.cos(lam_im * dt)
    ab_im = mag * jnp.sin(lam_im * dt)
    den = lam_re * lam_re + lam_im * lam_im
    n_re = ab_re - 1.0
    f_re = (n_re * lam_re + ab_im * lam_im) / den
    f_im = (ab_im * lam_re - n_re * lam_im) / den
    bb_re, bb_im = _cmul(f_re[..., None], f_im[..., None], b_re, b_im)
    eye = jnp.eye(S5_GROUPS, dtype=jnp.float32)

    def in_map(bb):
        return jnp.einsum('gh,dhnc->dgchn', eye, bb).reshape(2, W_GROUP, S5_NSTATE)

    def out_map(c):
        return jnp.einsum('hg,dgcn->dhngc', eye, c).reshape(2, S5_NSTATE, W_GROUP)

    bmat = jnp.concatenate([in_map(bb_re), in_map(bb_im)], axis=-1)
    cmat = jnp.concatenate([out_map(c_re), -out_map(c_im)], axis=1)
    lam = jnp.stack([ab_re.reshape(2, S5_NSTATE), ab_im.reshape(2, S5_NSTATE)], axis=1)
    return lam, bmat.astype(jnp.bfloat16), cmat.astype(jnp.bfloat16)


def _s5_scan_kernel(u_ref, lam_ref, bmat_ref, cmat_ref, y_ref, st_ref, carry_ref):
    d = pl.program_id(1)

    @pl.when(pl.program_id(2) == 0)
    def _():
        carry_ref[...] = jnp.zeros_like(carry_ref)

    st_ref[...] = jnp.dot(u_ref[...].astype(jnp.bfloat16), bmat_ref[0], preferred_element_type=jnp.float32)
    lam_r = lam_ref[0, 0:1, :]
    lam_i = lam_ref[0, 1:2, :]
    re = slice(0, S5_NSTATE)
    im = slice(S5_NSTATE, 2 * S5_NSTATE)

    def step(i, carry):
        sr, si = carry
        t = i + d * (S5_T - 1 - 2 * i)
        nr = lam_r * sr - lam_i * si + st_ref[pl.ds(t, 1), re]
        ni = lam_r * si + lam_i * sr + st_ref[pl.ds(t, 1), im]
        st_ref[pl.ds(t, 1), re] = nr
        st_ref[pl.ds(t, 1), im] = ni
        return nr, ni

    sr, si = lax.fori_loop(0, S5_T, step, (carry_ref[0:1, re], carry_ref[0:1, im]), unroll=4)
    carry_ref[0:1, re] = sr
    carry_ref[0:1, im] = si
    y_ref[0] = jnp.dot(st_ref[...].astype(jnp.bfloat16), cmat_ref[0], preferred_element_type=jnp.float32)


def s5_scan(z, lam, bmat, cmat, n_seq):
    n = z.shape[0]
    nch = n // n_seq // S5_T

    def row_block(b, d, c):
        return b * nch + c + d * (nch - 1 - 2 * c)

    return pl.pallas_call(
        _s5_scan_kernel,
        grid=(n_seq, 2, nch),
        in_specs=[pl.BlockSpec((S5_T, W_GROUP), lambda b, d, c: (row_block(b, d, c), 0)),
                  pl.BlockSpec((1, 2, S5_NSTATE), lambda b, d, c: (d, 0, 0)),
                  pl.BlockSpec((1, W_GROUP, 2 * S5_NSTATE), lambda b, d, c: (d, 0, 0)),
                  pl.BlockSpec((1, 2 * S5_NSTATE, W_GROUP), lambda b, d, c: (d, 0, 0))],
        out_specs=pl.BlockSpec((1, S5_T, W_GROUP), lambda b, d, c: (d, row_block(b, d, c), 0)),
        out_shape=jax.ShapeDtypeStruct((2, n, W_GROUP), jnp.float32),
        scratch_shapes=[pltpu.VMEM((S5_T, 2 * S5_NSTATE), jnp.float32),
                        pltpu.VMEM((8, 2 * S5_NSTATE), jnp.float32)],
        compiler_params=pltpu.CompilerParams(dimension_semantics=("arbitrary", "arbitrary", "arbitrary"),
                                             vmem_limit_bytes=VMEM_LIMIT),
        name="s5_scan",
    )(z, lam, bmat, cmat)


def _s5_finish_kernel(y_ref, z_ref, d_ref, w_ref, b_ref, o_ref):
    y = y_ref[0] + y_ref[1] + d_ref[...] * z_ref[...]
    g = jax.nn.gelu(y)
    gate = jnp.dot(g.astype(jnp.bfloat16), w_ref[...], preferred_element_type=jnp.float32) + b_ref[...]
    o_ref[...] = g * jax.nn.sigmoid(gate)


def s5_finish(y2, z, d, glu_w_bf16, glu_b):
    n = z.shape[0]
    vec = pl.BlockSpec((1, W_GROUP), lambda i: (0, 0))
    return pl.pallas_call(
        _s5_finish_kernel,
        grid=(n // ROW_TILE,),
        in_specs=[pl.BlockSpec((2, ROW_TILE, W_GROUP), lambda i: (0, i, 0)),
                  pl.BlockSpec((ROW_TILE, W_GROUP), lambda i: (i, 0)),
                  vec, pl.BlockSpec((W_GROUP, W_GROUP), lambda i: (0, 0)), vec],
        out_specs=pl.BlockSpec((ROW_TILE, W_GROUP), lambda i: (i, 0)),
        out_shape=jax.ShapeDtypeStruct((n, W_GROUP), jnp.float32),
        compiler_params=pltpu.CompilerParams(dimension_semantics=("parallel",),
                                             vmem_limit_bytes=VMEM_LIMIT),
        name="s5_finish",
    )(y2, z, d.reshape(1, W_GROUP), glu_w_bf16, glu_b.reshape(1, W_GROUP))


def s5_mixer(z, n_seq, lam_re, lam_im, log_dt, b_re, b_im, c_re, c_im, d, glu_w, glu_b):
    lam, bmat, cmat = _s5_operators(lam_re, lam_im, log_dt, b_re, b_im, c_re, c_im)
    y2 = s5_scan(z, lam, bmat, cmat, n_seq)
    return s5_finish(y2, z, d, glu_w.astype(jnp.bfloat16), glu_b)


HY_L = 4096
HY_N = 2 * HY_L
HY_N1 = 64
HY_N2 = 128
HY_K1 = HY_N1 // 2 + 1
HY_KB = 11
HY_KP = 24
HY_C = V7X_LANES


def _hyena_dft_tables():
    n1 = np.arange(HY_N1 // 2)
    k1 = np.arange(HY_K1)
    ang1 = 2.0 * np.pi * np.outer(k1, n1) / HY_N1
    weight = np.where((k1 == 0) | (k1 == HY_N1 // 2), 1.0, 2.0) / HY_N
    nblk = HY_K1 // HY_KB
    fa = np.zeros((nblk, HY_KP, HY_N1 // 2), np.float32)
    fi = np.zeros((nblk, HY_N1 // 2, HY_KP), np.float32)
    for b in range(nblk):
        sl = slice(b * HY_KB, (b + 1) * HY_KB)
        fa[b, :HY_KB] = np.cos(ang1[sl])
        fa[b, HY_KB:2 * HY_KB] = -np.sin(ang1[sl])
        fi[b, :, :HY_KB] = (np.cos(ang1[sl]) * weight[sl, None]).T
        fi[b, :, HY_KB:2 * HY_KB] = (-np.sin(ang1[sl]) * weight[sl, None]).T
    n2 = np.arange(HY_N2)
    k = k1[:, None] + HY_N1 * np.arange(HY_N2)[None, :]
    ang = 2.0 * np.pi * (k[:, :, None] * n2[None, None, :] % HY_N) / HY_N
    c, s = np.cos(ang), np.sin(ang)
    g = np.concatenate([np.concatenate([c, s], axis=2), np.concatenate([-s, c], axis=2)], axis=1)
    gi = np.transpose(g, (0, 2, 1))
    bf = jnp.bfloat16
    return (jnp.asarray(fa, bf), jnp.asarray(fi, bf), jnp.asarray(g, bf), jnp.asarray(gi, bf))


def _hyena_filter_spectra(w1, b1, freq, w2, b2, w3, log_rate):
    L = HY_L
    t = jnp.arange(L, dtype=jnp.float32) / L
    ang = 2.0 * math.pi * t[:, None] * jnp.arange(1, HY_BANDS + 1, dtype=jnp.float32)
    feats = jnp.concatenate([t[:, None], jnp.sin(ang), jnp.cos(ang)], axis=-1)
    h = jnp.sin(freq[0] * (feats @ w1 + b1))
    h = jnp.sin(freq[1] * (h @ w2 + b2))
    h = (h @ w3).reshape(L, 2, HY_ORDER, W_GROUP)
    h = h * jnp.exp(-jnp.exp(log_rate)[None] * t[:, None, None, None])
    fwd, bwd = h[:, 0], h[:, 1]
    k = jnp.concatenate([fwd, jnp.zeros_like(fwd[:1]), bwd[:0:-1]], axis=0)
    k = k / jnp.sum(jnp.abs(k), axis=0, keepdims=True)
    kf = jnp.fft.fft(k, axis=0).reshape(HY_N2, HY_N1, HY_ORDER, W_GROUP)[:, :HY_K1]
    kf = jnp.transpose(kf, (2, 1, 0, 3))
    return jnp.stack([jnp.real(kf), jnp.imag(kf)], axis=2).astype(jnp.float32)


def _hy_pre_kernel(z_ref, zp_ref, zn_ref, w_ref, b_ref, v_ref, x1_ref, x2_ref):
    z = z_ref[...]
    zp, zn = _shifted(z, zp_ref[0], zn_ref[0])
    z = w_ref[0:1, :] * zp + w_ref[1:2, :] * z + w_ref[2:3, :] * zn + b_ref[...]
    v_ref[...] = z[:, 0:W_GROUP]
    x1_ref[...] = z[:, W_GROUP:2 * W_GROUP]
    x2_ref[...] = z[:, 2 * W_GROUP:3 * W_GROUP]


def hy_pre(z, n_seq, conv_w, conv_b):
    n = z.shape[0]
    prev, nxt = _halo_rows(z, n_seq, ROW_TILE)
    row = pl.BlockSpec((ROW_TILE, W_GROUP), lambda i: (i, 0))
    halo = pl.BlockSpec((1, 1, HY_IN), lambda i: (i, 0, 0))
    return pl.pallas_call(
        _hy_pre_kernel,
        grid=(n // ROW_TILE,),
        in_specs=[pl.BlockSpec((ROW_TILE, HY_IN), lambda i: (i, 0)), halo, halo,
                  pl.BlockSpec((3, HY_IN), lambda i: (0, 0)), pl.BlockSpec((1, HY_IN), lambda i: (0, 0))],
        out_specs=[row] * 3,
        out_shape=[jax.ShapeDtypeStruct((n, W_GROUP), jnp.float32)] * 3,
        compiler_params=pltpu.CompilerParams(dimension_semantics=("parallel",),
                                             vmem_limit_bytes=VMEM_LIMIT),
        name="hy_pre",
    )(z, prev, nxt, conv_w, conv_b.reshape(1, HY_IN))


def _hy_conv_kernel(u_ref, fa_ref, fi_ref, g_ref, gi_ref, kf_ref, o_ref, as_ref, bs_ref):
    kb = pl.program_id(2)
    bf = jnp.bfloat16
    fa = fa_ref[0]
    fi = fi_ref[0]
    for n2 in range(HY_N2):
        rows = u_ref[pl.ds(n2, HY_N1 // 2, stride=HY_N2), :].astype(bf)
        as_ref[n2] = jnp.dot(fa, rows, preferred_element_type=jnp.float32)
    bs_ref[:, 2 * HY_KB:, :] = jnp.zeros((HY_N2, HY_KP - 2 * HY_KB, HY_C), jnp.float32)
    for j in range(HY_KB):
        a = jnp.concatenate([as_ref[:, j, :], as_ref[:, HY_KB + j, :]], axis=0).astype(bf)
        x = jnp.dot(g_ref[j], a, preferred_element_type=jnp.float32)
        xr, xi = x[:HY_N2], x[HY_N2:]
        kr, ki = kf_ref[0, j, 0], kf_ref[0, j, 1]
        y = jnp.concatenate([xr * kr - xi * ki, xr * ki + xi * kr], axis=0).astype(bf)
        b = jnp.dot(gi_ref[j], y, preferred_element_type=jnp.float32)
        bs_ref[:, j, :] = b[:HY_N2]
        bs_ref[:, HY_KB + j, :] = b[HY_N2:]
    @pl.when(kb == 0)
    def _():
        o_ref[...] = jnp.zeros_like(o_ref)

    for n2 in range(HY_N2):
        part = jnp.dot(fi, bs_ref[n2].astype(bf), preferred_element_type=jnp.float32)
        dst = pl.ds(n2, HY_N1 // 2, stride=HY_N2)
        o_ref[dst, :] = o_ref[dst, :] + part


def hy_conv(u, n_seq, kf, tables):
    n = u.shape[0]
    assert n // n_seq == HY_L
    fa, fi, g, gi = tables
    nblk = HY_K1 // HY_KB
    seq = pl.BlockSpec((HY_L, HY_C), lambda b, c, k: (b, c))
    return pl.pallas_call(
        _hy_conv_kernel,
        grid=(n_seq, W_GROUP // HY_C, nblk),
        in_specs=[seq,
                  pl.BlockSpec((1, HY_KP, HY_N1 // 2), lambda b, c, k: (k, 0, 0)),
                  pl.BlockSpec((1, HY_N1 // 2, HY_KP), lambda b, c, k: (k, 0, 0)),
                  pl.BlockSpec((HY_KB, 2 * HY_N2, 2 * HY_N2), lambda b, c, k: (k, 0, 0)),
                  pl.BlockSpec((HY_KB, 2 * HY_N2, 2 * HY_N2), lambda b, c, k: (k, 0, 0)),
                  pl.BlockSpec((1, HY_KB, 2, HY_N2, HY_C), lambda b, c, k: (0, k, 0, 0, c))],
        out_specs=seq,
        out_shape=jax.ShapeDtypeStruct((n, W_GROUP), jnp.float32),
        scratch_shapes=[pltpu.VMEM((HY_N2, HY_KP, HY_C), jnp.float32),
                        pltpu.VMEM((HY_N2, HY_KP, HY_C), jnp.float32)],
        compiler_params=pltpu.CompilerParams(dimension_semantics=("arbitrary", "arbitrary", "arbitrary"),
                                             vmem_limit_bytes=VMEM_LIMIT),
        name="hy_conv",
    )(u, fa, fi, g, gi, kf)


def _hy_gate_kernel(c_ref, u_ref, x_ref, d_ref, o_ref):
    o_ref[...] = x_ref[...] * (c_ref[...] + d_ref[...] * u_ref[...])


def hy_gate(c, u, gate, d):
    n = u.shape[0]
    row = pl.BlockSpec((ROW_TILE, W_GROUP), lambda i: (i, 0))
    return pl.pallas_call(
        _hy_gate_kernel,
        grid=(n // ROW_TILE,),
        in_specs=[row, row, row, pl.BlockSpec((1, W_GROUP), lambda i: (0, 0))],
        out_specs=row,
        out_shape=jax.ShapeDtypeStruct((n, W_GROUP), jnp.float32),
        compiler_params=pltpu.CompilerParams(dimension_semantics=("parallel",),
                                             vmem_limit_bytes=VMEM_LIMIT),
        name="hy_gate",
    )(c, u, gate, d.reshape(1, W_GROUP))


def hyena_mixer(z, n_seq, conv_w, conv_b, f_w1, f_b1, f_freq, f_w2, f_b2, f_w3, log_rate, d):
    v, x1, x2 = hy_pre(z, n_seq, conv_w, conv_b)
    kf = _hyena_filter_spectra(f_w1, f_b1, f_freq, f_w2, f_b2, f_w3, log_rate)
    tables = _hyena_dft_tables()
    u = v
    for o, gate in enumerate((x1, x2)):
        u = hy_gate(hy_conv(u, n_seq, kf[o:o + 1], tables), u, gate, d[o])
    return u


NA_BAND = 8
NA_KROWS = 2 * NA_BAND
NA_KBLK = 4


def _na_bias_tables(rel_bias, rows):
    hi = lax.Precision.HIGHEST
    qc = np.arange(GRID_W)
    ws = np.clip(qc - NA_WIN_C // 2, 0, GRID_W - NA_WIN_C)
    col_ok = (qc[None, :] >= ws[:, None]) & (qc[None, :] < ws[:, None] + NA_WIN_C)
    dc = np.clip(qc[None, :] - qc[:, None] + NA_WIN_C - 1, 0, 2 * NA_WIN_C - 2)
    dc_sel = (dc[..., None] == np.arange(2 * NA_WIN_C - 1)).astype(np.float32)
    col_bias = jnp.einsum('qkc,hdc->hdqk', dc_sel, rel_bias, precision=hi)
    tabs = []
    for band in (0, 1, rows // NA_BAND - 1):
        kb = int(np.clip(NA_BAND * band - NA_WIN_R // 2, 0, rows - NA_KROWS))
        r = NA_BAND * band + np.arange(NA_BAND)
        rs = np.clip(r - NA_WIN_R // 2, 0, rows - NA_WIN_R)
        krow = kb + np.arange(NA_KROWS)
        row_ok = (krow[None, :] >= rs[:, None]) & (krow[None, :] < rs[:, None] + NA_WIN_R)
        dr = np.clip(krow[None, :] - r[:, None] + NA_WIN_R - 1, 0, 2 * NA_WIN_R - 2)
        dr_sel = (dr[..., None] == np.arange(2 * NA_WIN_R - 1)).astype(np.float32)
        bias = jnp.einsum('jwd,hdqk->hjqwk', dr_sel, col_bias, precision=hi)
        ok = row_ok[:, None, :, None] & col_ok[None, :, None, :]
        tab = jnp.where(jnp.asarray(ok)[None], bias, NEG_INF)
        tabs.append(tab.reshape(NA_HEADS, NA_BAND * GRID_W, NA_KROWS * GRID_W))
    return jnp.stack(tabs)


def _na_kernel(q_ref, k0_ref, k1_ref, k2_ref, k3_ref, v0_ref, v1_ref, v2_ref, v3_ref, qg_ref, kg_ref,
               tab_ref, o_ref):
    q = q_ref[...]
    k = jnp.concatenate([k0_ref[...], k1_ref[...], k2_ref[...], k3_ref[...]], axis=0)
    v = jnp.concatenate([v0_ref[...], v1_ref[...], v2_ref[...], v3_ref[...]], axis=0)
    for h in range(NA_HEADS):
        sl = slice(h * NA_HEAD, (h + 1) * NA_HEAD)
        qh = (_rms(q[:, sl]) * qg_ref[...] * (NA_HEAD ** -0.5)).astype(jnp.bfloat16)
        kh = (_rms(k[:, sl]) * kg_ref[...]).astype(jnp.bfloat16)
        s = lax.dot_general(qh, kh, (((1,), (1,)), ((), ())), preferred_element_type=jnp.float32)
        s = s + tab_ref[0, h]
        p = jnp.exp(s - jnp.max(s, axis=-1, keepdims=True))
        den = jnp.sum(p, axis=-1, keepdims=True)
        o = jnp.dot(p.astype(jnp.bfloat16), v[:, sl].astype(jnp.bfloat16), preferred_element_type=jnp.float32)
        o_ref[:, sl] = o / den


def na_mixer(q, k, v, n_seq, q_g, k_g, rel_bias):
    n = q.shape[0]
    rows = n // n_seq // GRID_W
    n_band = rows // NA_BAND
    tab = _na_bias_tables(rel_bias, rows)
    qtok = NA_BAND * GRID_W
    ktok = NA_KBLK * GRID_W
    kblk_per_seq = rows // NA_KBLK

    def kv_spec(j):
        def index(i, b):
            first = jnp.clip(NA_BAND // NA_KBLK * i - 1, 0, kblk_per_seq - NA_KROWS // NA_KBLK)
            return (b * kblk_per_seq + first + j, 0)
        return pl.BlockSpec((ktok, W_GROUP), index)

    def tab_index(i, b):
        return (jnp.where(i == 0, 0, jnp.where(i == n_band - 1, 2, 1)), 0, 0, 0)

    qspec = pl.BlockSpec((qtok, W_GROUP), lambda i, b: (b * n_band + i, 0))
    gspec = pl.BlockSpec((1, NA_HEAD), lambda i, b: (0, 0))
    kvs = [kv_spec(j) for j in range(NA_KROWS // NA_KBLK)]
    return pl.pallas_call(
        _na_kernel,
        grid=(n_band, n_seq),
        in_specs=[qspec] + kvs + kvs + [gspec, gspec,
                  pl.BlockSpec((1, NA_HEADS, qtok, NA_KROWS * GRID_W), tab_index)],
        out_specs=qspec,
        out_shape=jax.ShapeDtypeStruct((n, W_GROUP), jnp.float32),
        compiler_params=pltpu.CompilerParams(dimension_semantics=("arbitrary", "arbitrary"),
                                             vmem_limit_bytes=VMEM_LIMIT),
        name="na_attn",
    )(q, k, k, k, k, v, v, v, v, q_g.reshape(1, NA_HEAD), k_g.reshape(1, NA_HEAD), tab)


def _trunk(x, p):
    Bn, L, _ = x.shape
    n = Bn * L
    x = x.reshape(n, D_MODEL)
    bf = jnp.bfloat16
    for l in range(DEPTH):
        z_s5, z_hy, z_rw, z_q, z_k, z_v = in_proj(x, p['ln1_g'][l], p['w_in'][l].astype(bf))
        y_s5 = s5_mixer(z_s5, Bn, p['s5_lam_re'][l], p['s5_lam_im'][l], p['s5_log_dt'][l],
                        p['s5_b_re'][l], p['s5_b_im'][l], p['s5_c_re'][l], p['s5_c_im'][l], p['s5_d'][l],
                        p['s5_glu_w'][l], p['s5_glu_b'][l])
        y_hy = hyena_mixer(z_hy, Bn, p['hy_conv_w'][l], p['hy_conv_b'][l], p['hy_f_w1'][l],
                           p['hy_f_b1'][l], p['hy_f_freq'][l], p['hy_f_w2'][l], p['hy_f_b2'][l],
                           p['hy_f_w3'][l], p['hy_log_rate'][l], p['hy_d'][l])
        y_rw = rwkv_mixer(z_rw, Bn, p['rw_mu'][l], p['rw_w0'][l], p['rw_w2'][l],
                          p['rw_a0'][l], p['rw_a2'][l], p['rw_g2'][l], p['rw_k_k'][l], p['rw_k_a'][l],
                          p['rw_r_k'][l], p['rw_ln_w'][l], p['rw_ln_b'][l])
        y_na = na_mixer(z_q, z_k, z_v, Bn, p['na_q_g'][l], p['na_k_g'][l], p['na_rel_bias'][l])
        ys = [y_s5, y_hy, y_rw, y_na]
        x = out_proj(x, ys, p['grp_g'][l], p['w_out'][l].astype(bf))
        x = mlp(x, p['ln2_g'][l], p['w_mlp1'][l].astype(bf), p['w_mlp2'][l].astype(bf))
    return x.reshape(Bn, L, D_MODEL)


def kernel(x_prompt, x_sample, ln1_g, w_in, s5_lam_re, s5_lam_im, s5_log_dt, s5_b_re, s5_b_im,
           s5_c_re, s5_c_im, s5_d, s5_glu_w, s5_glu_b, hy_conv_w, hy_conv_b, hy_f_w1, hy_f_b1,
           hy_f_freq, hy_f_w2, hy_f_b2, hy_f_w3, hy_log_rate, hy_d, rw_mu, rw_w0, rw_w2, rw_a0,
           rw_a2, rw_g2, rw_k_k, rw_k_a, rw_r_k, rw_ln_w, rw_ln_b, na_q_g, na_k_g, na_rel_bias,
           grp_g, w_out, ln2_g, w_mlp1, w_mlp2):
    p = dict(ln1_g=ln1_g, w_in=w_in, s5_lam_re=s5_lam_re, s5_lam_im=s5_lam_im, s5_log_dt=s5_log_dt,
             s5_b_re=s5_b_re, s5_b_im=s5_b_im, s5_c_re=s5_c_re, s5_c_im=s5_c_im, s5_d=s5_d,
             s5_glu_w=s5_glu_w, s5_glu_b=s5_glu_b, hy_conv_w=hy_conv_w, hy_conv_b=hy_conv_b,
             hy_f_w1=hy_f_w1, hy_f_b1=hy_f_b1, hy_f_freq=hy_f_freq, hy_f_w2=hy_f_w2, hy_f_b2=hy_f_b2,
             hy_f_w3=hy_f_w3, hy_log_rate=hy_log_rate, hy_d=hy_d, rw_mu=rw_mu, rw_w0=rw_w0, rw_w2=rw_w2,
             rw_a0=rw_a0, rw_a2=rw_a2, rw_g2=rw_g2, rw_k_k=rw_k_k, rw_k_a=rw_k_a, rw_r_k=rw_r_k,
             rw_ln_w=rw_ln_w, rw_ln_b=rw_ln_b, na_q_g=na_q_g, na_k_g=na_k_g, na_rel_bias=na_rel_bias,
             grp_g=grp_g, w_out=w_out, ln2_g=ln2_g, w_mlp1=w_mlp1, w_mlp2=w_mlp2)
    nb = x_prompt.shape[0]
    y = _trunk(jnp.concatenate([x_prompt, x_sample], axis=0), p)
    return (y[:nb], y[nb:])
```

```python
import math

import jax
import jax.numpy as jnp
import numpy as np
from jax import lax
from jax.experimental import pallas as pl
from jax.experimental.pallas import tpu as pltpu

D_MODEL = 1024
DEPTH = 4
GRID_W = 64
W_GROUP = 256
N_MIXERS = 4
D_FF = 4 * D_MODEL
NORM_EPS = 1e-6

S5_CH = 16
S5_GROUPS = W_GROUP // S5_CH
S5_STATE = 64
S5_IN = W_GROUP

HY_ORDER = 2
HY_BANDS = 8
HY_IN = (HY_ORDER + 1) * W_GROUP

RW_HEAD = 64
RW_HEADS = W_GROUP // RW_HEAD
RW_DECAY_RANK = 64
RW_A_RANK = 64
RW_G_RANK = 128
RW_LN_EPS = 64e-5
RW_IN = 3 * W_GROUP + RW_DECAY_RANK + RW_A_RANK + RW_G_RANK
RW_SPLITS = (W_GROUP, 2 * W_GROUP, 3 * W_GROUP, 3 * W_GROUP + RW_DECAY_RANK,
             3 * W_GROUP + RW_DECAY_RANK + RW_A_RANK)

NA_HEAD = 64
NA_HEADS = W_GROUP // NA_HEAD
NA_WIN_R = 8
NA_WIN_C = 16
NEG_INF = -1e30

D_IN = S5_IN + HY_IN + RW_IN + 3 * W_GROUP

V7X_LANES = 128
VMEM_LIMIT = 48 * 1024 * 1024

ROW_TILE = 512
FF_TILE = 1024
MLP_ROW_TILE = 1024
SCAN_T = 16
SCAN_UNROLL = 32


def _rms(x):
    return x * lax.rsqrt(jnp.mean(x * x, axis=-1, keepdims=True) + NORM_EPS)


IN_WIDTHS = (S5_IN, HY_IN, RW_IN, W_GROUP, W_GROUP, W_GROUP)


def _in_proj_kernel(x_ref, g_ref, w_ref, *o_refs):
    h = (_rms(x_ref[...]) * g_ref[...]).astype(jnp.bfloat16)
    lo = 0
    for o_ref, width in zip(o_refs, IN_WIDTHS):
        o_ref[...] = jnp.dot(h, w_ref[:, lo:lo + width], preferred_element_type=jnp.float32)
        lo += width


def in_proj(x, g, w_bf16):
    n = x.shape[0]
    return pl.pallas_call(
        _in_proj_kernel,
        grid=(n // ROW_TILE,),
        in_specs=[pl.BlockSpec((ROW_TILE, D_MODEL), lambda i: (i, 0)),
                  pl.BlockSpec((1, D_MODEL), lambda i: (0, 0)),
                  pl.BlockSpec((D_MODEL, D_IN), lambda i: (0, 0))],
        out_specs=[pl.BlockSpec((ROW_TILE, w), lambda i: (i, 0)) for w in IN_WIDTHS],
        out_shape=[jax.ShapeDtypeStruct((n, w), jnp.float32) for w in IN_WIDTHS],
        compiler_params=pltpu.CompilerParams(dimension_semantics=("parallel",),
                                             vmem_limit_bytes=VMEM_LIMIT),
        name="in_proj",
    )(x, g.reshape(1, D_MODEL), w_bf16)


def _out_proj_kernel(x_ref, y0_ref, y1_ref, y2_ref, y3_ref, g_ref, w_ref, o_ref):
    acc = x_ref[...]
    for i, y_ref in enumerate((y0_ref, y1_ref, y2_ref, y3_ref)):
        n = (_rms(y_ref[...]) * g_ref[i:i + 1, :]).astype(jnp.bfloat16)
        acc = acc + jnp.dot(n, w_ref[i * W_GROUP:(i + 1) * W_GROUP, :],
                            preferred_element_type=jnp.float32)
    o_ref[...] = acc


def out_proj(x, ys, g, w_bf16):
    n = x.shape[0]
    row = lambda w: pl.BlockSpec((ROW_TILE, w), lambda i: (i, 0))
    return pl.pallas_call(
        _out_proj_kernel,
        grid=(n // ROW_TILE,),
        in_specs=[row(D_MODEL)] + [row(W_GROUP)] * N_MIXERS
                 + [pl.BlockSpec((N_MIXERS, W_GROUP), lambda i: (0, 0)),
                    pl.BlockSpec((D_MODEL, D_MODEL), lambda i: (0, 0))],
        out_specs=row(D_MODEL),
        out_shape=jax.ShapeDtypeStruct((n, D_MODEL), jnp.float32),
        compiler_params=pltpu.CompilerParams(dimension_semantics=("parallel",),
                                             vmem_limit_bytes=VMEM_LIMIT),
        name="out_proj",
    )(x, *ys, g, w_bf16)


def _mlp_kernel(x_ref, g_ref, w1_ref, w2_ref, o_ref, h_ref):
    j = pl.program_id(1)

    @pl.when(j == 0)
    def _():
        x = x_ref[...]
        h_ref[...] = (_rms(x) * g_ref[...]).astype(jnp.bfloat16)
        o_ref[...] = x

    a = jnp.dot(h_ref[...], w1_ref[...], preferred_element_type=jnp.float32)
    a = jnp.square(jnp.maximum(a, 0.0)).astype(jnp.bfloat16)
    o_ref[...] += jnp.dot(a, w2_ref[...], preferred_element_type=jnp.float32)


def mlp(x, g, w1_bf16, w2_bf16):
    n = x.shape[0]
    return pl.pallas_call(
        _mlp_kernel,
        grid=(n // MLP_ROW_TILE, D_FF // FF_TILE),
        in_specs=[pl.BlockSpec((MLP_ROW_TILE, D_MODEL), lambda i, j: (i, 0)),
                  pl.BlockSpec((1, D_MODEL), lambda i, j: (0, 0)),
                  pl.BlockSpec((D_MODEL, FF_TILE), lambda i, j: (0, j)),
                  pl.BlockSpec((FF_TILE, D_MODEL), lambda i, j: (j, 0))],
        out_specs=pl.BlockSpec((MLP_ROW_TILE, D_MODEL), lambda i, j: (i, 0)),
        out_shape=jax.ShapeDtypeStruct((n, D_MODEL), jnp.float32),
        scratch_shapes=[pltpu.VMEM((MLP_ROW_TILE, D_MODEL), jnp.bfloat16)],
        compiler_params=pltpu.CompilerParams(dimension_semantics=("parallel", "arbitrary"),
                                             vmem_limit_bytes=VMEM_LIMIT),
        name="mlp",
    )(x, g.reshape(1, D_MODEL), w1_bf16, w2_bf16)


def _head_sum(x):
    lane = lax.broadcasted_iota(jnp.int32, (W_GROUP, W_GROUP), 0) // RW_HEAD
    col = lax.broadcasted_iota(jnp.int32, (W_GROUP, W_GROUP), 1) // RW_HEAD
    ones = (lane == col).astype(jnp.float32)
    return jnp.dot(x, ones, precision=lax.Precision.HIGHEST, preferred_element_type=jnp.float32)


def _softplus(x):
    return jnp.maximum(x, 0.0) + jnp.log(1.0 + jnp.exp(-jnp.abs(x)))


def _shifted(z, prev_row, next_row):
    t = z.shape[0]
    row = lax.broadcasted_iota(jnp.int32, z.shape, 0)
    zp = jnp.where(row == 0, prev_row, pltpu.roll(z, 1, 0))
    zn = jnp.where(row == t - 1, next_row, pltpu.roll(z, t - 1, 0))
    return zp, zn


def _store_heads(o_ref, x):
    for h in range(RW_HEADS):
        o_ref[h] = x[:, h * RW_HEAD:(h + 1) * RW_HEAD]


def _rwkv_prep_kernel(z_ref, zp_ref, zn_ref, mu_ref, w0_ref, w2_ref, a0_ref, a2_ref, g2_ref, kk_ref, ka_ref,
                      rk_ref, nkk_ref, r_ref, v_ref, dec0_ref, dec1_ref, kd0_ref, kd1_ref, b0_ref, b1_ref,
                      g_ref, bonus_ref):
    z = z_ref[...]
    zp, zn = _shifted(z, zp_ref[0], zn_ref[0])
    z = z + mu_ref[...] * (0.5 * (zp + zn) - z)
    r = z[:, 0:W_GROUP]
    k = z[:, W_GROUP:2 * W_GROUP]
    v = z[:, 2 * W_GROUP:3 * W_GROUP]
    wd = z[:, RW_SPLITS[2]:RW_SPLITS[3]]
    ad = z[:, RW_SPLITS[3]:RW_SPLITS[4]]
    gd = z[:, RW_SPLITS[4]:RW_IN]
    bf = jnp.bfloat16
    g_ref[...] = jnp.dot(jax.nn.sigmoid(gd).astype(bf), g2_ref[...], preferred_element_type=jnp.float32)
    kk = k * kk_ref[...]
    kk = kk / jnp.maximum(jnp.sqrt(_head_sum(kk * kk)), 1e-12)
    _store_heads(nkk_ref, -kk)
    _store_heads(r_ref, r)
    _store_heads(v_ref, v)
    bonus_ref[...] = _head_sum(r * k * rk_ref[...]) * v
    tw = jnp.tanh(wd).astype(bf)
    adb = ad.astype(bf)
    for d, (dec_ref, kd_ref, b_ref) in enumerate(((dec0_ref, kd0_ref, b0_ref), (dec1_ref, kd1_ref, b1_ref))):
        w = w0_ref[d:d + 1, :] + jnp.dot(tw, w2_ref[d], preferred_element_type=jnp.float32)
        w = -_softplus(-w) - 0.5
        _store_heads(dec_ref, jnp.exp(-jnp.exp(w)))
        a = jax.nn.sigmoid(a0_ref[d:d + 1, :] + jnp.dot(adb, a2_ref[d], preferred_element_type=jnp.float32))
        _store_heads(kd_ref, k * (1.0 + (a - 1.0) * ka_ref[...]))
        _store_heads(b_ref, kk * a)


def _halo_rows(z, n_seq, tile):
    n, c = z.shape
    zt = z.reshape(n_seq, n // n_seq // tile, tile, c)
    zero = jnp.zeros((n_seq, 1, c), z.dtype)
    prev = jnp.concatenate([zero, zt[:, :-1, -1]], axis=1).reshape(n // tile, 1, c)
    nxt = jnp.concatenate([zt[:, 1:, 0], zero], axis=1).reshape(n // tile, 1, c)
    return prev, nxt


def rwkv_prep(z, n_seq, mu, w0, w2, a0, a2, g2, k_k, k_a, r_k):
    n = z.shape[0]
    prev, nxt = _halo_rows(z, n_seq, ROW_TILE)
    bf = jnp.bfloat16
    full = lambda *s: pl.BlockSpec(s, lambda i: (0,) * len(s))
    row = pl.BlockSpec((ROW_TILE, W_GROUP), lambda i: (i, 0))
    heads = pl.BlockSpec((RW_HEADS, ROW_TILE, RW_HEAD), lambda i: (0, i, 0))
    halo = pl.BlockSpec((1, 1, RW_IN), lambda i: (i, 0, 0))
    vec = lambda x: x.reshape(1, -1)
    return pl.pallas_call(
        _rwkv_prep_kernel,
        grid=(n // ROW_TILE,),
        in_specs=[pl.BlockSpec((ROW_TILE, RW_IN), lambda i: (i, 0)), halo, halo, full(1, RW_IN),
                  full(2, W_GROUP), full(2, RW_DECAY_RANK, W_GROUP), full(2, W_GROUP),
                  full(2, RW_A_RANK, W_GROUP), full(RW_G_RANK, W_GROUP), full(1, W_GROUP), full(1, W_GROUP),
                  full(1, W_GROUP)],
        out_specs=[heads] * 9 + [row] * 2,
        out_shape=[jax.ShapeDtypeStruct((RW_HEADS, n, RW_HEAD), jnp.float32)] * 9
                  + [jax.ShapeDtypeStruct((n, W_GROUP), jnp.float32)] * 2,
        compiler_params=pltpu.CompilerParams(dimension_semantics=("parallel",),
                                             vmem_limit_bytes=VMEM_LIMIT),
        name="rwkv_prep",
    )(z, prev, nxt, vec(mu), w0, w2.astype(bf), a0, a2.astype(bf), g2.astype(bf), vec(k_k), vec(k_a), vec(r_k))


def _rwkv_scan_kernel(af_ref, wf_ref, bf_ref, kf_ref, rf_ref, vf_ref, ab_ref, wb_ref, bb_ref, kb_ref, rb_ref,
                      vb_ref, yf_ref, yb_ref, s_ref):
    @pl.when(pl.program_id(0) == 0)
    def _():
        s_ref[...] = jnp.zeros_like(s_ref)

    dirs = ((af_ref, wf_ref, bf_ref, kf_ref, rf_ref, vf_ref, yf_ref),
            (ab_ref, wb_ref, bb_ref, kb_ref, rb_ref, vb_ref, yb_ref))

    def step(i, carry):
        for j, (a_ref, w_ref, b_ref, k_ref, r_ref, v_ref, y_ref) in enumerate(dirs):
            t = i if j == 0 else SCAN_T - 1 - i
            vt = v_ref[t]

            def reduce_a(k, sa):
                return sa + s_ref[j, k] * a_ref[t, pl.ds(k, 1), :]

            sa = lax.fori_loop(0, RW_HEAD, reduce_a, jnp.zeros_like(vt), unroll=SCAN_UNROLL)

            def update(k, y):
                s = (s_ref[j, k] * w_ref[t, pl.ds(k, 1), :] + sa * b_ref[t, pl.ds(k, 1), :]
                     + vt * k_ref[t, pl.ds(k, 1), :])
                s_ref[j, k] = s
                return y + s * r_ref[t, pl.ds(k, 1), :]

            y_ref[t] = lax.fori_loop(0, RW_HEAD, update, jnp.zeros_like(vt), unroll=SCAN_UNROLL)
        return carry

    lax.fori_loop(0, SCAN_T, step, 0)


def rwkv_scan(a, r, v, w_f, b_f, k_f, w_b, b_b, k_b):
    L, _, nc = a.shape
    nblk = L // SCAN_T
    fwd = pl.BlockSpec((SCAN_T, RW_HEAD, nc), lambda i: (i, 0, 0))
    bwd = pl.BlockSpec((SCAN_T, RW_HEAD, nc), lambda i: (nblk - 1 - i, 0, 0))
    out = jax.ShapeDtypeStruct((L, RW_HEAD, nc), jnp.float32)
    return pl.pallas_call(
        _rwkv_scan_kernel,
        grid=(nblk,),
        in_specs=[fwd] * 6 + [bwd] * 6,
        out_specs=[fwd, bwd],
        out_shape=[out, out],
        scratch_shapes=[pltpu.VMEM((2, RW_HEAD, RW_HEAD, nc), jnp.float32)],
        compiler_params=pltpu.CompilerParams(dimension_semantics=("arbitrary",),
                                             vmem_limit_bytes=VMEM_LIMIT),
        name="rwkv_scan",
    )(a, w_f, b_f, k_f, r, v, a, w_b, b_b, k_b, r, v)


def _rwkv_post_kernel(yf_ref, yb_ref, bonus_ref, g_ref, lw_ref, lb_ref, o_ref):
    y = jnp.concatenate([yf_ref[h] + yb_ref[h] for h in range(RW_HEADS)], axis=-1)
    mean = _head_sum(y) * (1.0 / RW_HEAD)
    c = y - mean
    var = _head_sum(c * c) * (1.0 / RW_HEAD)
    y = c * lax.rsqrt(var + RW_LN_EPS) * lw_ref[...] + lb_ref[...]
    o_ref[...] = (y + bonus_ref[...]) * g_ref[...]


def rwkv_post(y_f, y_b, bonus, g, ln_w, ln_b):
    n = bonus.shape[0]
    row = pl.BlockSpec((ROW_TILE, W_GROUP), lambda i: (i, 0))
    heads = pl.BlockSpec((RW_HEADS, ROW_TILE, RW_HEAD), lambda i: (0, i, 0))
    vec = pl.BlockSpec((1, W_GROUP), lambda i: (0, 0))
    return pl.pallas_call(
        _rwkv_post_kernel,
        grid=(n // ROW_TILE,),
        in_specs=[heads, heads, row, row, vec, vec],
        out_specs=row,
        out_shape=jax.ShapeDtypeStruct((n, W_GROUP), jnp.float32),
        compiler_params=pltpu.CompilerParams(dimension_semantics=("parallel",),
                                             vmem_limit_bytes=VMEM_LIMIT),
        name="rwkv_post",
    )(y_f, y_b, bonus, g, ln_w.reshape(1, W_GROUP), ln_b.reshape(1, W_GROUP))


def rwkv_mixer(z, n_seq, mu, w0, w2, a0, a2, g2, k_k, k_a, r_k, ln_w, ln_b):
    n = z.shape[0]
    L = n // n_seq
    n_chain = n_seq * RW_HEADS
    assert n_chain <= V7X_LANES
    nkk, r, v, dec0, dec1, kd0, kd1, b0, b1, g, bonus = rwkv_prep(z, n_seq, mu, w0, w2, a0, a2, g2, k_k, k_a, r_k)

    def chains(x):
        return x.reshape(RW_HEADS, n_seq, L, RW_HEAD).transpose(2, 3, 0, 1).reshape(L, RW_HEAD, n_chain)

    def tokens(y):
        return y.reshape(L, RW_HEAD, RW_HEADS, n_seq).transpose(2, 3, 0, 1).reshape(RW_HEADS, n, RW_HEAD)

    y_f, y_b = rwkv_scan(chains(nkk), chains(r), chains(v), chains(dec0), chains(b0), chains(kd0),
                         chains(dec1), chains(b1), chains(kd1))
    return rwkv_post(tokens(y_f), tokens(y_b), bonus, g, ln_w, ln_b)


S5_NSTATE = S5_GROUPS * S5_STATE
S5_T = 512


def _cmul(ar, ai, br, bi):
    return ar * br - ai * bi, ar * bi + ai * br


def _s5_operators(lam_re, lam_im, log_dt, b_re, b_im, c_re, c_im):
    dt = jnp.exp(log_dt)[..., None]
    mag = jnp.exp(lam_re * dt)
    ab_re = mag * jnp.cos(lam_im * dt)
    ab_im = mag * jnp.sin(lam_im * dt)
    den = lam_re * lam_re + lam_im * lam_im
    n_re = ab_re - 1.0
    f_re = (n_re * lam_re + ab_im * lam_im) / den
    f_im = (ab_im * lam_re - n_re * lam_im) / den
    bb_re, bb_im = _cmul(f_re[..., None], f_im[..., None], b_re, b_im)
    eye = jnp.eye(S5_GROUPS, dtype=jnp.float32)

    def in_map(bb):
        return jnp.einsum('gh,dhnc->dgchn', eye, bb).reshape(2, W_GROUP, S5_NSTATE)

    def out_map(c):
        return jnp.einsum('hg,dgcn->dhngc', eye, c).reshape(2, S5_NSTATE, W_GROUP)

    bmat = jnp.concatenate([in_map(bb_re), in_map(bb_im)], axis=-1)
    cmat = jnp.concatenate([out_map(c_re), -out_map(c_im)], axis=1)
    lam = jnp.stack([ab_re.reshape(2, S5_NSTATE), ab_im.reshape(2, S5_NSTATE)], axis=1)
    return lam, bmat.astype(jnp.bfloat16), cmat.astype(jnp.bfloat16)


def _s5_scan_kernel(u_ref, lam_ref, bmat_ref, cmat_ref, y_ref, st_ref, carry_ref):
    d = pl.program_id(1)

    @pl.when(pl.program_id(2) == 0)
    def _():
        carry_ref[...] = jnp.zeros_like(carry_ref)

    st_ref[...] = jnp.dot(u_ref[...].astype(jnp.bfloat16), bmat_ref[0], preferred_element_type=jnp.float32)
    lam_r = lam_ref[0, 0:1, :]
    lam_i = lam_ref[0, 1:2, :]
    re = slice(0, S5_NSTATE)
    im = slice(S5_NSTATE, 2 * S5_NSTATE)

    def step(i, carry):
        sr, si = carry
        t = i + d * (S5_T - 1 - 2 * i)
        nr = lam_r * sr - lam_i * si + st_ref[pl.ds(t, 1), re]
        ni = lam_r * si + lam_i * sr + st_ref[pl.ds(t, 1), im]
        st_ref[pl.ds(t, 1), re] = nr
        st_ref[pl.ds(t, 1), im] = ni
        return nr, ni

    sr, si = lax.fori_loop(0, S5_T, step, (carry_ref[0:1, re], carry_ref[0:1, im]), unroll=4)
    carry_ref[0:1, re] = sr
    carry_ref[0:1, im] = si
    y_ref[0] = jnp.dot(st_ref[...].astype(jnp.bfloat16), cmat_ref[0], preferred_element_type=jnp.float32)


def s5_scan(z, lam, bmat, cmat, n_seq):
    n = z.shape[0]
    nch = n // n_seq // S5_T

    def row_block(b, d, c):
        return b * nch + c + d * (nch - 1 - 2 * c)

    return pl.pallas_call(
        _s5_scan_kernel,
        grid=(n_seq, 2, nch),
        in_specs=[pl.BlockSpec((S5_T, W_GROUP), lambda b, d, c: (row_block(b, d, c), 0)),
                  pl.BlockSpec((1, 2, S5_NSTATE), lambda b, d, c: (d, 0, 0)),
                  pl.BlockSpec((1, W_GROUP, 2 * S5_NSTATE), lambda b, d, c: (d, 0, 0)),
                  pl.BlockSpec((1, 2 * S5_NSTATE, W_GROUP), lambda b, d, c: (d, 0, 0))],
        out_specs=pl.BlockSpec((1, S5_T, W_GROUP), lambda b, d, c: (d, row_block(b, d, c), 0)),
        out_shape=jax.ShapeDtypeStruct((2, n, W_GROUP), jnp.float32),
        scratch_shapes=[pltpu.VMEM((S5_T, 2 * S5_NSTATE), jnp.float32),
                        pltpu.VMEM((8, 2 * S5_NSTATE), jnp.float32)],
        compiler_params=pltpu.CompilerParams(dimension_semantics=("arbitrary", "arbitrary", "arbitrary"),
                                             vmem_limit_bytes=VMEM_LIMIT),
        name="s5_scan",
    )(z, lam, bmat, cmat)


def _s5_finish_kernel(y_ref, z_ref, d_ref, w_ref, b_ref, o_ref):
    y = y_ref[0] + y_ref[1] + d_ref[...] * z_ref[...]
    g = jax.nn.gelu(y)
    gate = jnp.dot(g.astype(jnp.bfloat16), w_ref[...], preferred_element_type=jnp.float32) + b_ref[...]
    o_ref[...] = g * jax.nn.sigmoid(gate)


def s5_finish(y2, z, d, glu_w_bf16, glu_b):
    n = z.shape[0]
    vec = pl.BlockSpec((1, W_GROUP), lambda i: (0, 0))
    return pl.pallas_call(
        _s5_finish_kernel,
        grid=(n // ROW_TILE,),
        in_specs=[pl.BlockSpec((2, ROW_TILE, W_GROUP), lambda i: (0, i, 0)),
                  pl.BlockSpec((ROW_TILE, W_GROUP), lambda i: (i, 0)),
                  vec, pl.BlockSpec((W_GROUP, W_GROUP), lambda i: (0, 0)), vec],
        out_specs=pl.BlockSpec((ROW_TILE, W_GROUP), lambda i: (i, 0)),
        out_shape=jax.ShapeDtypeStruct((n, W_GROUP), jnp.float32),
        compiler_params=pltpu.CompilerParams(dimension_semantics=("parallel",),
                                             vmem_limit_bytes=VMEM_LIMIT),
        name="s5_finish",
    )(y2, z, d.reshape(1, W_GROUP), glu_w_bf16, glu_b.reshape(1, W_GROUP))


def s5_mixer(z, n_seq, lam_re, lam_im, log_dt, b_re, b_im, c_re, c_im, d, glu_w, glu_b):
    lam, bmat, cmat = _s5_operators(lam_re, lam_im, log_dt, b_re, b_im, c_re, c_im)
    y2 = s5_scan(z, lam, bmat, cmat, n_seq)
    return s5_finish(y2, z, d, glu_w.astype(jnp.bfloat16), glu_b)


HY_L = 4096
HY_N = 2 * HY_L
HY_N1 = 64
HY_N2 = 128
HY_K1 = HY_N1 // 2 + 1
HY_KB = 11
HY_KP = 24
HY_C = V7X_LANES


def _hyena_dft_tables():
    n1 = np.arange(HY_N1 // 2)
    k1 = np.arange(HY_K1)
    ang1 = 2.0 * np.pi * np.outer(k1, n1) / HY_N1
    weight = np.where((k1 == 0) | (k1 == HY_N1 // 2), 1.0, 2.0) / HY_N
    nblk = HY_K1 // HY_KB
    fa = np.zeros((nblk, HY_KP, HY_N1 // 2), np.float32)
    fi = np.zeros((nblk, HY_N1 // 2, HY_KP), np.float32)
    for b in range(nblk):
        sl = slice(b * HY_KB, (b + 1) * HY_KB)
        fa[b, :HY_KB] = np.cos(ang1[sl])
        fa[b, HY_KB:2 * HY_KB] = -np.sin(ang1[sl])
        fi[b, :, :HY_KB] = (np.cos(ang1[sl]) * weight[sl, None]).T
        fi[b, :, HY_KB:2 * HY_KB] = (-np.sin(ang1[sl]) * weight[sl, None]).T
    n2 = np.arange(HY_N2)
    k = k1[:, None] + HY_N1 * np.arange(HY_N2)[None, :]
    ang = 2.0 * np.pi * (k[:, :, None] * n2[None, None, :] % HY_N) / HY_N
    c, s = np.cos(ang), np.sin(ang)
    g = np.concatenate([np.concatenate([c, s], axis=2), np.concatenate([-s, c], axis=2)], axis=1)
    gi = np.transpose(g, (0, 2, 1))
    bf = jnp.bfloat16
    return (jnp.asarray(fa, bf), jnp.asarray(fi, bf), jnp.asarray(g, bf), jnp.asarray(gi, bf))


def _hyena_filter_spectra(w1, b1, freq, w2, b2, w3, log_rate):
    L = HY_L
    t = jnp.arange(L, dtype=jnp.float32) / L
    ang = 2.0 * math.pi * t[:, None] * jnp.arange(1, HY_BANDS + 1, dtype=jnp.float32)
    feats = jnp.concatenate([t[:, None], jnp.sin(ang), jnp.cos(ang)], axis=-1)
    h = jnp.sin(freq[0] * (feats @ w1 + b1))
    h = jnp.sin(freq[1] * (h @ w2 + b2))
    h = (h @ w3).reshape(L, 2, HY_ORDER, W_GROUP)
    h = h * jnp.exp(-jnp.exp(log_rate)[None] * t[:, None, None, None])
    fwd, bwd = h[:, 0], h[:, 1]
    k = jnp.concatenate([fwd, jnp.zeros_like(fwd[:1]), bwd[:0:-1]], axis=0)
    k = k / jnp.sum(jnp.abs(k), axis=0, keepdims=True)
    kf = jnp.fft.fft(k, axis=0).reshape(HY_N2, HY_N1, HY_ORDER, W_GROUP)[:, :HY_K1]
    kf = jnp.transpose(kf, (2, 1, 0, 3))
    return jnp.stack([jnp.real(kf), jnp.imag(kf)], axis=2).astype(jnp.float32)


def _hy_pre_kernel(z_ref, zp_ref, zn_ref, w_ref, b_ref, v_ref, x1_ref, x2_ref):
    z = z_ref[...]
    zp, zn = _shifted(z, zp_ref[0], zn_ref[0])
    z = w_ref[0:1, :] * zp + w_ref[1:2, :] * z + w_ref[2:3, :] * zn + b_ref[...]
    v_ref[...] = z[:, 0:W_GROUP]
    x1_ref[...] = z[:, W_GROUP:2 * W_GROUP]
    x2_ref[...] = z[:, 2 * W_GROUP:3 * W_GROUP]


def hy_pre(z, n_seq, conv_w, conv_b):
    n = z.shape[0]
    prev, nxt = _halo_rows(z, n_seq, ROW_TILE)
    row = pl.BlockSpec((ROW_TILE, W_GROUP), lambda i: (i, 0))
    halo = pl.BlockSpec((1, 1, HY_IN), lambda i: (i, 0, 0))
    return pl.pallas_call(
        _hy_pre_kernel,
        grid=(n // ROW_TILE,),
        in_specs=[pl.BlockSpec((ROW_TILE, HY_IN), lambda i: (i, 0)), halo, halo,
                  pl.BlockSpec((3, HY_IN), lambda i: (0, 0)), pl.BlockSpec((1, HY_IN), lambda i: (0, 0))],
        out_specs=[row] * 3,
        out_shape=[jax.ShapeDtypeStruct((n, W_GROUP), jnp.float32)] * 3,
        compiler_params=pltpu.CompilerParams(dimension_semantics=("parallel",),
                                             vmem_limit_bytes=VMEM_LIMIT),
        name="hy_pre",
    )(z, prev, nxt, conv_w, conv_b.reshape(1, HY_IN))


def _hy_conv_kernel(u_ref, fa_ref, fi_ref, g_ref, gi_ref, kf_ref, o_ref, as_ref, bs_ref):
    kb = pl.program_id(2)
    nblk = HY_K1 // HY_KB
    bf = jnp.bfloat16

    @pl.when(kb == 0)
    def _():
        fa = fa_ref[...].reshape(nblk * HY_KP, HY_N1 // 2)
        for n2 in range(HY_N2):
            rows = u_ref[pl.ds(n2, HY_N1 // 2, stride=HY_N2), :].astype(bf)
            a_all = jnp.dot(fa, rows, preferred_element_type=jnp.float32)
            for b in range(nblk):
                as_ref[b, n2] = a_all[b * HY_KP:(b + 1) * HY_KP]

    bs_ref[kb, :, 2 * HY_KB:, :] = jnp.zeros((HY_N2, HY_KP - 2 * HY_KB, HY_C), jnp.float32)
    a_blk = as_ref.at[kb]
    b_blk = bs_ref.at[kb]
    for j in range(HY_KB):
        a = jnp.concatenate([a_blk[:, j, :], a_blk[:, HY_KB + j, :]], axis=0).astype(bf)
        x = jnp.dot(g_ref[j], a, preferred_element_type=jnp.float32)
        xr, xi = x[:HY_N2], x[HY_N2:]
        kr, ki = kf_ref[0, j, 0], kf_ref[0, j, 1]
        y = jnp.concatenate([xr * kr - xi * ki, xr * ki + xi * kr], axis=0).astype(bf)
        b = jnp.dot(gi_ref[j], y, preferred_element_type=jnp.float32)
        b_blk[:, j, :] = b[:HY_N2]
        b_blk[:, HY_KB + j, :] = b[HY_N2:]

    @pl.when(kb == nblk - 1)
    def _():
        for n2 in range(HY_N2):
            part = jnp.dot(fi_ref[0], bs_ref[0, n2].astype(bf), preferred_element_type=jnp.float32)
            for b in range(1, nblk):
                part = part + jnp.dot(fi_ref[b], bs_ref[b, n2].astype(bf), preferred_element_type=jnp.float32)
            o_ref[pl.ds(n2, HY_N1 // 2, stride=HY_N2), :] = part


def hy_conv(u, n_seq, kf, tables):
    n = u.shape[0]
    assert n // n_seq == HY_L
    fa, fi, g, gi = tables
    nblk = HY_K1 // HY_KB
    seq = pl.BlockSpec((HY_L, HY_C), lambda b, c, k: (b, c))
    return pl.pallas_call(
        _hy_conv_kernel,
        grid=(n_seq, W_GROUP // HY_C, nblk),
        in_specs=[seq,
                  pl.BlockSpec((nblk, HY_KP, HY_N1 // 2), lambda b, c, k: (0, 0, 0)),
                  pl.BlockSpec((nblk, HY_N1 // 2, HY_KP), lambda b, c, k: (0, 0, 0)),
                  pl.BlockSpec((HY_KB, 2 * HY_N2, 2 * HY_N2), lambda b, c, k: (k, 0, 0)),
                  pl.BlockSpec((HY_KB, 2 * HY_N2, 2 * HY_N2), lambda b, c, k: (k, 0, 0)),
                  pl.BlockSpec((1, HY_KB, 2, HY_N2, HY_C), lambda b, c, k: (0, k, 0, 0, c))],
        out_specs=seq,
        out_shape=jax.ShapeDtypeStruct((n, W_GROUP), jnp.float32),
        scratch_shapes=[pltpu.VMEM((nblk, HY_N2, HY_KP, HY_C), jnp.float32),
                        pltpu.VMEM((nblk, HY_N2, HY_KP, HY_C), jnp.float32)],
        compiler_params=pltpu.CompilerParams(dimension_semantics=("arbitrary", "arbitrary", "arbitrary"),
                                             vmem_limit_bytes=VMEM_LIMIT),
        name="hy_conv",
    )(u, fa, fi, g, gi, kf)


def _hy_gate_kernel(c_ref, u_ref, x_ref, d_ref, o_ref):
    o_ref[...] = x_ref[...] * (c_ref[...] + d_ref[...] * u_ref[...])


def hy_gate(c, u, gate, d):
    n = u.shape[0]
    row = pl.BlockSpec((ROW_TILE, W_GROUP), lambda i: (i, 0))
    return pl.pallas_call(
        _hy_gate_kernel,
        grid=(n // ROW_TILE,),
        in_specs=[row, row, row, pl.BlockSpec((1, W_GROUP), lambda i: (0, 0))],
        out_specs=row,
        out_shape=jax.ShapeDtypeStruct((n, W_GROUP), jnp.float32),
        compiler_params=pltpu.CompilerParams(dimension_semantics=("parallel",),
                                             vmem_limit_bytes=VMEM_LIMIT),
        name="hy_gate",
    )(c, u, gate, d.reshape(1, W_GROUP))


def hyena_mixer(z, n_seq, conv_w, conv_b, f_w1, f_b1, f_freq, f_w2, f_b2, f_w3, log_rate, d):
    v, x1, x2 = hy_pre(z, n_seq, conv_w, conv_b)
    kf = _hyena_filter_spectra(f_w1, f_b1, f_freq, f_w2, f_b2, f_w3, log_rate)
    tables = _hyena_dft_tables()
    u = v
    for o, gate in enumerate((x1, x2)):
        u = hy_gate(hy_conv(u, n_seq, kf[o:o + 1], tables), u, gate, d[o])
    return u


NA_BAND = 8
NA_KROWS = 2 * NA_BAND
NA_KBLK = 4


def _na_bias_tables(rel_bias, rows):
    hi = lax.Precision.HIGHEST
    qc = np.arange(GRID_W)
    ws = np.clip(qc - NA_WIN_C // 2, 0, GRID_W - NA_WIN_C)
    col_ok = (qc[None, :] >= ws[:, None]) & (qc[None, :] < ws[:, None] + NA_WIN_C)
    dc = np.clip(qc[None, :] - qc[:, None] + NA_WIN_C - 1, 0, 2 * NA_WIN_C - 2)
    dc_sel = (dc[..., None] == np.arange(2 * NA_WIN_C - 1)).astype(np.float32)
    col_bias = jnp.einsum('qkc,hdc->hdqk', dc_sel, rel_bias, precision=hi)
    tabs = []
    for band in (0, 1, rows // NA_BAND - 1):
        kb = int(np.clip(NA_BAND * band - NA_WIN_R // 2, 0, rows - NA_KROWS))
        r = NA_BAND * band + np.arange(NA_BAND)
        rs = np.clip(r - NA_WIN_R // 2, 0, rows - NA_WIN_R)
        krow = kb + np.arange(NA_KROWS)
        row_ok = (krow[None, :] >= rs[:, None]) & (krow[None, :] < rs[:, None] + NA_WIN_R)
        dr = np.clip(krow[None, :] - r[:, None] + NA_WIN_R - 1, 0, 2 * NA_WIN_R - 2)
        dr_sel = (dr[..., None] == np.arange(2 * NA_WIN_R - 1)).astype(np.float32)
        bias = jnp.einsum('jwd,hdqk->hjqwk', dr_sel, col_bias, precision=hi)
        ok = row_ok[:, None, :, None] & col_ok[None, :, None, :]
        tab = jnp.where(jnp.asarray(ok)[None], bias, NEG_INF)
        tabs.append(tab.reshape(NA_HEADS, NA_BAND * GRID_W, NA_KROWS * GRID_W))
    return jnp.stack(tabs)


def _na_kernel(q_ref, k0_ref, k1_ref, k2_ref, k3_ref, v0_ref, v1_ref, v2_ref, v3_ref, qg_ref, kg_ref,
               tab_ref, o_ref):
    q = q_ref[...]
    k = jnp.concatenate([k0_ref[...], k1_ref[...], k2_ref[...], k3_ref[...]], axis=0)
    v = jnp.concatenate([v0_ref[...], v1_ref[...], v2_ref[...], v3_ref[...]], axis=0)
    for h in range(NA_HEADS):
        sl = slice(h * NA_HEAD, (h + 1) * NA_HEAD)
        qh = (_rms(q[:, sl]) * qg_ref[...] * (NA_HEAD ** -0.5)).astype(jnp.bfloat16)
        kh = (_rms(k[:, sl]) * kg_ref[...]).astype(jnp.bfloat16)
        s = lax.dot_general(qh, kh, (((1,), (1,)), ((), ())), preferred_element_type=jnp.float32)
        s = s + tab_ref[0, h]
        p = jnp.exp(s - jnp.max(s, axis=-1, keepdims=True))
        den = jnp.sum(p, axis=-1, keepdims=True)
        o = jnp.dot(p.astype(jnp.bfloat16), v[:, sl].astype(jnp.bfloat16), preferred_element_type=jnp.float32)
        o_ref[:, sl] = o / den


def na_mixer(q, k, v, n_seq, q_g, k_g, rel_bias):
    n = q.shape[0]
    rows = n // n_seq // GRID_W
    n_band = rows // NA_BAND
    tab = _na_bias_tables(rel_bias, rows)
    qtok = NA_BAND * GRID_W
    ktok = NA_KBLK * GRID_W
    kblk_per_seq = rows // NA_KBLK

    def kv_spec(j):
        def index(i, b):
            first = jnp.clip(NA_BAND // NA_KBLK * i - 1, 0, kblk_per_seq - NA_KROWS // NA_KBLK)
            return (b * kblk_per_seq + first + j, 0)
        return pl.BlockSpec((ktok, W_GROUP), index)

    def tab_index(i, b):
        return (jnp.where(i == 0, 0, jnp.where(i == n_band - 1, 2, 1)), 0, 0, 0)

    qspec = pl.BlockSpec((qtok, W_GROUP), lambda i, b: (b * n_band + i, 0))
    gspec = pl.BlockSpec((1, NA_HEAD), lambda i, b: (0, 0))
    kvs = [kv_spec(j) for j in range(NA_KROWS // NA_KBLK)]
    return pl.pallas_call(
        _na_kernel,
        grid=(n_band, n_seq),
        in_specs=[qspec] + kvs + kvs + [gspec, gspec,
                  pl.BlockSpec((1, NA_HEADS, qtok, NA_KROWS * GRID_W), tab_index)],
        out_specs=qspec,
        out_shape=jax.ShapeDtypeStruct((n, W_GROUP), jnp.float32),
        compiler_params=pltpu.CompilerParams(dimension_semantics=("arbitrary", "arbitrary"),
                                             vmem_limit_bytes=VMEM_LIMIT),
        name="na_attn",
    )(q, k, k, k, k, v, v, v, v, q_g.reshape(1, NA_HEAD), k_g.reshape(1, NA_HEAD), tab)


def _trunk(x, p):
    Bn, L, _ = x.shape
    n = Bn * L
    x = x.reshape(n, D_MODEL)
    bf = jnp.bfloat16
    for l in range(DEPTH):
        z_s5, z_hy, z_rw, z_q, z_k, z_v = in_proj(x, p['ln1_g'][l], p['w_in'][l].astype(bf))
        y_s5 = s5_mixer(z_s5, Bn, p['s5_lam_re'][l], p['s5_lam_im'][l], p['s5_log_dt'][l],
                        p['s5_b_re'][l], p['s5_b_im'][l], p['s5_c_re'][l], p['s5_c_im'][l], p['s5_d'][l],
                        p['s5_glu_w'][l], p['s5_glu_b'][l])
        y_hy = hyena_mixer(z_hy, Bn, p['hy_conv_w'][l], p['hy_conv_b'][l], p['hy_f_w1'][l],
                           p['hy_f_b1'][l], p['hy_f_freq'][l], p['hy_f_w2'][l], p['hy_f_b2'][l],
                           p['hy_f_w3'][l], p['hy_log_rate'][l], p['hy_d'][l])
        y_rw = rwkv_mixer(z_rw, Bn, p['rw_mu'][l], p['rw_w0'][l], p['rw_w2'][l],
                          p['rw_a0'][l], p['rw_a2'][l], p['rw_g2'][l], p['rw_k_k'][l], p['rw_k_a'][l],
                          p['rw_r_k'][l], p['rw_ln_w'][l], p['rw_ln_b'][l])
        y_na = na_mixer(z_q, z_k, z_v, Bn, p['na_q_g'][l], p['na_k_g'][l], p['na_rel_bias'][l])
        ys = [y_s5, y_hy, y_rw, y_na]
        x = out_proj(x, ys, p['grp_g'][l], p['w_out'][l].astype(bf))
        x = mlp(x, p['ln2_g'][l], p['w_mlp1'][l].astype(bf), p['w_mlp2'][l].astype(bf))
    return x.reshape(Bn, L, D_MODEL)


def kernel(x_prompt, x_sample, ln1_g, w_in, s5_lam_re, s5_lam_im, s5_log_dt, s5_b_re, s5_b_im,
           s5_c_re, s5_c_im, s5_d, s5_glu_w, s5_glu_b, hy_conv_w, hy_conv_b, hy_f_w1, hy_f_b1,
           hy_f_freq, hy_f_w2, hy_f_b2, hy_f_w3, hy_log_rate, hy_d, rw_mu, rw_w0, rw_w2, rw_a0,
           rw_a2, rw_g2, rw_k_k, rw_k_a, rw_r_k, rw_ln_w, rw_ln_b, na_q_g, na_k_g, na_rel_bias,
           grp_g, w_out, ln2_g, w_mlp1, w_mlp2):
    p = dict(ln1_g=ln1_g, w_in=w_in, s5_lam_re=s5_lam_re, s5_lam_im=s5_lam_im, s5_log_dt=s5_log_dt,
             s5_b_re=s5_b_re, s5_b_im=s5_b_im, s5_c_re=s5_c_re, s5_c_im=s5_c_im, s5_d=s5_d,
             s5_glu_w=s5_glu_w, s5_glu_b=s5_glu_b, hy_conv_w=hy_conv_w, hy_conv_b=hy_conv_b,
             hy_f_w1=hy_f_w1, hy_f_b1=hy_f_b1, hy_f_freq=hy_f_freq, hy_f_w2=hy_f_w2, hy_f_b2=hy_f_b2,
             hy_f_w3=hy_f_w3, hy_log_rate=hy_log_rate, hy_d=hy_d, rw_mu=rw_mu, rw_w0=rw_w0, rw_w2=rw_w2,
             rw_a0=rw_a0, rw_a2=rw_a2, rw_g2=rw_g2, rw_k_k=rw_k_k, rw_k_a=rw_k_a, rw_r_k=rw_r_k,
             rw_ln_w=rw_ln_w, rw_ln_b=rw_ln_b, na_q_g=na_q_g, na_k_g=na_k_g, na_rel_bias=na_rel_bias,
             grp_g=grp_g, w_out=w_out, ln2_g=ln2_g, w_mlp1=w_mlp1, w_mlp2=w_mlp2)
    nb = x_prompt.shape[0]
    y = _trunk(jnp.concatenate([x_prompt, x_sample], axis=0), p)
    return (y[:nb], y[nb:])
```

```python
import math

import jax
import jax.numpy as jnp
import numpy as np
from jax import lax
from jax.experimental import pallas as pl
from jax.experimental.pallas import tpu as pltpu

D_MODEL = 1024
DEPTH = 4
GRID_W = 64
W_GROUP = 256
N_MIXERS = 4
D_FF = 4 * D_MODEL
NORM_EPS = 1e-6

S5_CH = 16
S5_GROUPS = W_GROUP // S5_CH
S5_STATE = 64
S5_IN = W_GROUP

HY_ORDER = 2
HY_BANDS = 8
HY_IN = (HY_ORDER + 1) * W_GROUP

RW_HEAD = 64
RW_HEADS = W_GROUP // RW_HEAD
RW_DECAY_RANK = 64
RW_A_RANK = 64
RW_G_RANK = 128
RW_LN_EPS = 64e-5
RW_IN = 3 * W_GROUP + RW_DECAY_RANK + RW_A_RANK + RW_G_RANK
RW_SPLITS = (W_GROUP, 2 * W_GROUP, 3 * W_GROUP, 3 * W_GROUP + RW_DECAY_RANK,
             3 * W_GROUP + RW_DECAY_RANK + RW_A_RANK)

NA_HEAD = 64
NA_HEADS = W_GROUP // NA_HEAD
NA_WIN_R = 8
NA_WIN_C = 16
NEG_INF = -1e30

D_IN = S5_IN + HY_IN + RW_IN + 3 * W_GROUP

V7X_LANES = 128
VMEM_LIMIT = 48 * 1024 * 1024

ROW_TILE = 512
FF_TILE = 1024
MLP_ROW_TILE = 1024
SCAN_T = 16
SCAN_UNROLL = 32


def _rms(x):
    return x * lax.rsqrt(jnp.mean(x * x, axis=-1, keepdims=True) + NORM_EPS)


IN_WIDTHS = (S5_IN, HY_IN, RW_IN, W_GROUP, W_GROUP, W_GROUP)


def _in_proj_kernel(x_ref, g_ref, w_ref, *o_refs):
    h = (_rms(x_ref[...]) * g_ref[...]).astype(jnp.bfloat16)
    lo = 0
    for o_ref, width in zip(o_refs, IN_WIDTHS):
        o_ref[...] = jnp.dot(h, w_ref[:, lo:lo + width], preferred_element_type=jnp.float32)
        lo += width


def in_proj(x, g, w_bf16):
    n = x.shape[0]
    return pl.pallas_call(
        _in_proj_kernel,
        grid=(n // ROW_TILE,),
        in_specs=[pl.BlockSpec((ROW_TILE, D_MODEL), lambda i: (i, 0)),
                  pl.BlockSpec((1, D_MODEL), lambda i: (0, 0)),
                  pl.BlockSpec((D_MODEL, D_IN), lambda i: (0, 0))],
        out_specs=[pl.BlockSpec((ROW_TILE, w), lambda i: (i, 0)) for w in IN_WIDTHS],
        out_shape=[jax.ShapeDtypeStruct((n, w), jnp.float32) for w in IN_WIDTHS],
        compiler_params=pltpu.CompilerParams(dimension_semantics=("parallel",),
                                             vmem_limit_bytes=VMEM_LIMIT),
        name="in_proj",
    )(x, g.reshape(1, D_MODEL), w_bf16)


def _out_mlp_kernel(x_ref, y0_ref, y1_ref, y2_ref, y3_ref, gg_ref, wo_ref, g_ref, w1_ref, w2_ref, o_ref, h_ref):
    j = pl.program_id(1)

    @pl.when(j == 0)
    def _():
        x = x_ref[...]
        for i, y_ref in enumerate((y0_ref, y1_ref, y2_ref, y3_ref)):
            n = (_rms(y_ref[...]) * gg_ref[i:i + 1, :]).astype(jnp.bfloat16)
            x = x + jnp.dot(n, wo_ref[i * W_GROUP:(i + 1) * W_GROUP, :], preferred_element_type=jnp.float32)
        h_ref[...] = (_rms(x) * g_ref[...]).astype(jnp.bfloat16)
        o_ref[...] = x

    a = jnp.dot(h_ref[...], w1_ref[...], preferred_element_type=jnp.float32)
    a = jnp.square(jnp.maximum(a, 0.0)).astype(jnp.bfloat16)
    o_ref[...] += jnp.dot(a, w2_ref[...], preferred_element_type=jnp.float32)


def out_mlp(x, ys, grp_g, wo_bf16, g, w1_bf16, w2_bf16):
    n = x.shape[0]
    row = lambda w: pl.BlockSpec((MLP_ROW_TILE, w), lambda i, j: (i, 0))
    return pl.pallas_call(
        _out_mlp_kernel,
        grid=(n // MLP_ROW_TILE, D_FF // FF_TILE),
        in_specs=[row(D_MODEL)] + [row(W_GROUP)] * N_MIXERS
                 + [pl.BlockSpec((N_MIXERS, W_GROUP), lambda i, j: (0, 0)),
                    pl.BlockSpec((D_MODEL, D_MODEL), lambda i, j: (0, 0)),
                    pl.BlockSpec((1, D_MODEL), lambda i, j: (0, 0)),
                    pl.BlockSpec((D_MODEL, FF_TILE), lambda i, j: (0, j)),
                    pl.BlockSpec((FF_TILE, D_MODEL), lambda i, j: (j, 0))],
        out_specs=row(D_MODEL),
        out_shape=jax.ShapeDtypeStruct((n, D_MODEL), jnp.float32),
        scratch_shapes=[pltpu.VMEM((MLP_ROW_TILE, D_MODEL), jnp.bfloat16)],
        compiler_params=pltpu.CompilerParams(dimension_semantics=("parallel", "arbitrary"),
                                             vmem_limit_bytes=VMEM_LIMIT),
        name="out_mlp",
    )(x, *ys, grp_g, wo_bf16, g.reshape(1, D_MODEL), w1_bf16, w2_bf16)


def _head_sum(x):
    lane = lax.broadcasted_iota(jnp.int32, (W_GROUP, W_GROUP), 0) // RW_HEAD
    col = lax.broadcasted_iota(jnp.int32, (W_GROUP, W_GROUP), 1) // RW_HEAD
    ones = (lane == col).astype(jnp.float32)
    return jnp.dot(x, ones, precision=lax.Precision.HIGHEST, preferred_element_type=jnp.float32)


def _softplus(x):
    return jnp.maximum(x, 0.0) + jnp.log(1.0 + jnp.exp(-jnp.abs(x)))


def _shifted(z, prev_row, next_row):
    t = z.shape[0]
    row = lax.broadcasted_iota(jnp.int32, z.shape, 0)
    zp = jnp.where(row == 0, prev_row, pltpu.roll(z, 1, 0))
    zn = jnp.where(row == t - 1, next_row, pltpu.roll(z, t - 1, 0))
    return zp, zn


def _store_heads(o_ref, x):
    for h in range(RW_HEADS):
        o_ref[h] = x[:, h * RW_HEAD:(h + 1) * RW_HEAD]


def _rwkv_prep_kernel(z_ref, zp_ref, zn_ref, mu_ref, w0_ref, w2_ref, a0_ref, a2_ref, g2_ref, kk_ref, ka_ref,
                      rk_ref, nkk_ref, r_ref, v_ref, dec0_ref, dec1_ref, kd0_ref, kd1_ref, b0_ref, b1_ref,
                      g_ref, bonus_ref):
    z = z_ref[...]
    zp, zn = _shifted(z, zp_ref[0], zn_ref[0])
    z = z + mu_ref[...] * (0.5 * (zp + zn) - z)
    r = z[:, 0:W_GROUP]
    k = z[:, W_GROUP:2 * W_GROUP]
    v = z[:, 2 * W_GROUP:3 * W_GROUP]
    wd = z[:, RW_SPLITS[2]:RW_SPLITS[3]]
    ad = z[:, RW_SPLITS[3]:RW_SPLITS[4]]
    gd = z[:, RW_SPLITS[4]:RW_IN]
    bf = jnp.bfloat16
    g_ref[...] = jnp.dot(jax.nn.sigmoid(gd).astype(bf), g2_ref[...], preferred_element_type=jnp.float32)
    kk = k * kk_ref[...]
    kk = kk / jnp.maximum(jnp.sqrt(_head_sum(kk * kk)), 1e-12)
    _store_heads(nkk_ref, -kk)
    _store_heads(r_ref, r)
    _store_heads(v_ref, v)
    bonus_ref[...] = _head_sum(r * k * rk_ref[...]) * v
    tw = jnp.tanh(wd).astype(bf)
    adb = ad.astype(bf)
    for d, (dec_ref, kd_ref, b_ref) in enumerate(((dec0_ref, kd0_ref, b0_ref), (dec1_ref, kd1_ref, b1_ref))):
        w = w0_ref[d:d + 1, :] + jnp.dot(tw, w2_ref[d], preferred_element_type=jnp.float32)
        w = -_softplus(-w) - 0.5
        _store_heads(dec_ref, jnp.exp(-jnp.exp(w)))
        a = jax.nn.sigmoid(a0_ref[d:d + 1, :] + jnp.dot(adb, a2_ref[d], preferred_element_type=jnp.float32))
        _store_heads(kd_ref, k * (1.0 + (a - 1.0) * ka_ref[...]))
        _store_heads(b_ref, kk * a)


def _halo_rows(z, n_seq, tile):
    n, c = z.shape
    zt = z.reshape(n_seq, n // n_seq // tile, tile, c)
    zero = jnp.zeros((n_seq, 1, c), z.dtype)
    prev = jnp.concatenate([zero, zt[:, :-1, -1]], axis=1).reshape(n // tile, 1, c)
    nxt = jnp.concatenate([zt[:, 1:, 0], zero], axis=1).reshape(n // tile, 1, c)
    return prev, nxt


def rwkv_prep(z, n_seq, mu, w0, w2, a0, a2, g2, k_k, k_a, r_k):
    n = z.shape[0]
    prev, nxt = _halo_rows(z, n_seq, ROW_TILE)
    bf = jnp.bfloat16
    full = lambda *s: pl.BlockSpec(s, lambda i: (0,) * len(s))
    row = pl.BlockSpec((ROW_TILE, W_GROUP), lambda i: (i, 0))
    heads = pl.BlockSpec((RW_HEADS, ROW_TILE, RW_HEAD), lambda i: (0, i, 0))
    halo = pl.BlockSpec((1, 1, RW_IN), lambda i: (i, 0, 0))
    vec = lambda x: x.reshape(1, -1)
    return pl.pallas_call(
        _rwkv_prep_kernel,
        grid=(n // ROW_TILE,),
        in_specs=[pl.BlockSpec((ROW_TILE, RW_IN), lambda i: (i, 0)), halo, halo, full(1, RW_IN),
                  full(2, W_GROUP), full(2, RW_DECAY_RANK, W_GROUP), full(2, W_GROUP),
                  full(2, RW_A_RANK, W_GROUP), full(RW_G_RANK, W_GROUP), full(1, W_GROUP), full(1, W_GROUP),
                  full(1, W_GROUP)],
        out_specs=[heads] * 9 + [row] * 2,
        out_shape=[jax.ShapeDtypeStruct((RW_HEADS, n, RW_HEAD), jnp.float32)] * 9
                  + [jax.ShapeDtypeStruct((n, W_GROUP), jnp.float32)] * 2,
        compiler_params=pltpu.CompilerParams(dimension_semantics=("parallel",),
                                             vmem_limit_bytes=VMEM_LIMIT),
        name="rwkv_prep",
    )(z, prev, nxt, vec(mu), w0, w2.astype(bf), a0, a2.astype(bf), g2.astype(bf), vec(k_k), vec(k_a), vec(r_k))


def _rwkv_scan_kernel(af_ref, wf_ref, bf_ref, kf_ref, rf_ref, vf_ref, ab_ref, wb_ref, bb_ref, kb_ref, rb_ref,
                      vb_ref, yf_ref, yb_ref, s_ref):
    @pl.when(pl.program_id(0) == 0)
    def _():
        s_ref[...] = jnp.zeros_like(s_ref)

    dirs = ((af_ref, wf_ref, bf_ref, kf_ref, rf_ref, vf_ref, yf_ref),
            (ab_ref, wb_ref, bb_ref, kb_ref, rb_ref, vb_ref, yb_ref))

    def step(i, carry):
        for j, (a_ref, w_ref, b_ref, k_ref, r_ref, v_ref, y_ref) in enumerate(dirs):
            t = i if j == 0 else SCAN_T - 1 - i
            vt = v_ref[t]

            def reduce_a(k, sa):
                return sa + s_ref[j, k] * a_ref[t, pl.ds(k, 1), :]

            sa = lax.fori_loop(0, RW_HEAD, reduce_a, jnp.zeros_like(vt), unroll=SCAN_UNROLL)

            def update(k, y):
                s = (s_ref[j, k] * w_ref[t, pl.ds(k, 1), :] + sa * b_ref[t, pl.ds(k, 1), :]
                     + vt * k_ref[t, pl.ds(k, 1), :])
                s_ref[j, k] = s
                return y + s * r_ref[t, pl.ds(k, 1), :]

            y_ref[t] = lax.fori_loop(0, RW_HEAD, update, jnp.zeros_like(vt), unroll=SCAN_UNROLL)
        return carry

    lax.fori_loop(0, SCAN_T, step, 0)


def rwkv_scan(a, r, v, w_f, b_f, k_f, w_b, b_b, k_b):
    L, _, nc = a.shape
    nblk = L // SCAN_T
    fwd = pl.BlockSpec((SCAN_T, RW_HEAD, nc), lambda i: (i, 0, 0))
    bwd = pl.BlockSpec((SCAN_T, RW_HEAD, nc), lambda i: (nblk - 1 - i, 0, 0))
    out = jax.ShapeDtypeStruct((L, RW_HEAD, nc), jnp.float32)
    return pl.pallas_call(
        _rwkv_scan_kernel,
        grid=(nblk,),
        in_specs=[fwd] * 6 + [bwd] * 6,
        out_specs=[fwd, bwd],
        out_shape=[out, out],
        scratch_shapes=[pltpu.VMEM((2, RW_HEAD, RW_HEAD, nc), jnp.float32)],
        compiler_params=pltpu.CompilerParams(dimension_semantics=("arbitrary",),
                                             vmem_limit_bytes=VMEM_LIMIT),
        name="rwkv_scan",
    )(a, w_f, b_f, k_f, r, v, a, w_b, b_b, k_b, r, v)


def _rwkv_post_kernel(yf_ref, yb_ref, bonus_ref, g_ref, lw_ref, lb_ref, o_ref):
    y = jnp.concatenate([yf_ref[h] + yb_ref[h] for h in range(RW_HEADS)], axis=-1)
    mean = _head_sum(y) * (1.0 / RW_HEAD)
    c = y - mean
    var = _head_sum(c * c) * (1.0 / RW_HEAD)
    y = c * lax.rsqrt(var + RW_LN_EPS) * lw_ref[...] + lb_ref[...]
    o_ref[...] = (y + bonus_ref[...]) * g_ref[...]


def rwkv_post(y_f, y_b, bonus, g, ln_w, ln_b):
    n = bonus.shape[0]
    row = pl.BlockSpec((ROW_TILE, W_GROUP), lambda i: (i, 0))
    heads = pl.BlockSpec((RW_HEADS, ROW_TILE, RW_HEAD), lambda i: (0, i, 0))
    vec = pl.BlockSpec((1, W_GROUP), lambda i: (0, 0))
    return pl.pallas_call(
        _rwkv_post_kernel,
        grid=(n // ROW_TILE,),
        in_specs=[heads, heads, row, row, vec, vec],
        out_specs=row,
        out_shape=jax.ShapeDtypeStruct((n, W_GROUP), jnp.float32),
        compiler_params=pltpu.CompilerParams(dimension_semantics=("parallel",),
                                             vmem_limit_bytes=VMEM_LIMIT),
        name="rwkv_post",
    )(y_f, y_b, bonus, g, ln_w.reshape(1, W_GROUP), ln_b.reshape(1, W_GROUP))


def rwkv_mixer(z, n_seq, mu, w0, w2, a0, a2, g2, k_k, k_a, r_k, ln_w, ln_b):
    n = z.shape[0]
    L = n // n_seq
    n_chain = n_seq * RW_HEADS
    assert n_chain <= V7X_LANES
    nkk, r, v, dec0, dec1, kd0, kd1, b0, b1, g, bonus = rwkv_prep(z, n_seq, mu, w0, w2, a0, a2, g2, k_k, k_a, r_k)

    def chains(x):
        return x.reshape(RW_HEADS, n_seq, L, RW_HEAD).transpose(2, 3, 0, 1).reshape(L, RW_HEAD, n_chain)

    def tokens(y):
        return y.reshape(L, RW_HEAD, RW_HEADS, n_seq).transpose(2, 3, 0, 1).reshape(RW_HEADS, n, RW_HEAD)

    y_f, y_b = rwkv_scan(chains(nkk), chains(r), chains(v), chains(dec0), chains(b0), chains(kd0),
                         chains(dec1), chains(b1), chains(kd1))
    return rwkv_post(tokens(y_f), tokens(y_b), bonus, g, ln_w, ln_b)


S5_NSTATE = S5_GROUPS * S5_STATE
S5_T = 512


def _cmul(ar, ai, br, bi):
    return ar * br - ai * bi, ar * bi + ai * br


def _s5_operators(lam_re, lam_im, log_dt, b_re, b_im, c_re, c_im):
    dt = jnp.exp(log_dt)[..., None]
    mag = jnp.exp(lam_re * dt)
    ab_re = mag * jnp.cos(lam_im * dt)
    ab_im = mag * jnp.sin(lam_im * dt)
    den = lam_re * lam_re + lam_im * lam_im
    n_re = ab_re - 1.0
    f_re = (n_re * lam_re + ab_im * lam_im) / den
    f_im = (ab_im * lam_re - n_re * lam_im) / den
    bb_re, bb_im = _cmul(f_re[..., None], f_im[..., None], b_re, b_im)
    eye = jnp.eye(S5_GROUPS, dtype=jnp.float32)

    def in_map(bb):
        return jnp.einsum('gh,dhnc->dgchn', eye, bb).reshape(2, W_GROUP, S5_NSTATE)

    def out_map(c):
        return jnp.einsum('hg,dgcn->dhngc', eye, c).reshape(2, S5_NSTATE, W_GROUP)

    bmat = jnp.concatenate([in_map(bb_re), in_map(bb_im)], axis=-1)
    cmat = jnp.concatenate([out_map(c_re), -out_map(c_im)], axis=1)
    lam = jnp.stack([ab_re.reshape(2, S5_NSTATE), ab_im.reshape(2, S5_NSTATE)], axis=1)
    return lam, bmat.astype(jnp.bfloat16), cmat.astype(jnp.bfloat16)


def _s5_scan_kernel(u_ref, lam_ref, bmat_ref, cmat_ref, y_ref, st_ref, carry_ref):
    d = pl.program_id(1)

    @pl.when(pl.program_id(2) == 0)
    def _():
        carry_ref[...] = jnp.zeros_like(carry_ref)

    st_ref[...] = jnp.dot(u_ref[...].astype(jnp.bfloat16), bmat_ref[0], preferred_element_type=jnp.float32)
    lam_r = lam_ref[0, 0:1, :]
    lam_i = lam_ref[0, 1:2, :]
    re = slice(0, S5_NSTATE)
    im = slice(S5_NSTATE, 2 * S5_NSTATE)

    def step(i, carry):
        sr, si = carry
        t = i + d * (S5_T - 1 - 2 * i)
        nr = lam_r * sr - lam_i * si + st_ref[pl.ds(t, 1), re]
        ni = lam_r * si + lam_i * sr + st_ref[pl.ds(t, 1), im]
        st_ref[pl.ds(t, 1), re] = nr
        st_ref[pl.ds(t, 1), im] = ni
        return nr, ni

    sr, si = lax.fori_loop(0, S5_T, step, (carry_ref[0:1, re], carry_ref[0:1, im]), unroll=4)
    carry_ref[0:1, re] = sr
    carry_ref[0:1, im] = si
    y_ref[0] = jnp.dot(st_ref[...].astype(jnp.bfloat16), cmat_ref[0], preferred_element_type=jnp.float32)


def s5_scan(z, lam, bmat, cmat, n_seq):
    n = z.shape[0]
    nch = n // n_seq // S5_T

    def row_block(b, d, c):
        return b * nch + c + d * (nch - 1 - 2 * c)

    return pl.pallas_call(
        _s5_scan_kernel,
        grid=(n_seq, 2, nch),
        in_specs=[pl.BlockSpec((S5_T, W_GROUP), lambda b, d, c: (row_block(b, d, c), 0)),
                  pl.BlockSpec((1, 2, S5_NSTATE), lambda b, d, c: (d, 0, 0)),
                  pl.BlockSpec((1, W_GROUP, 2 * S5_NSTATE), lambda b, d, c: (d, 0, 0)),
                  pl.BlockSpec((1, 2 * S5_NSTATE, W_GROUP), lambda b, d, c: (d, 0, 0))],
        out_specs=pl.BlockSpec((1, S5_T, W_GROUP), lambda b, d, c: (d, row_block(b, d, c), 0)),
        out_shape=jax.ShapeDtypeStruct((2, n, W_GROUP), jnp.float32),
        scratch_shapes=[pltpu.VMEM((S5_T, 2 * S5_NSTATE), jnp.float32),
                        pltpu.VMEM((8, 2 * S5_NSTATE), jnp.float32)],
        compiler_params=pltpu.CompilerParams(dimension_semantics=("arbitrary", "arbitrary", "arbitrary"),
                                             vmem_limit_bytes=VMEM_LIMIT),
        name="s5_scan",
    )(z, lam, bmat, cmat)


def _s5_finish_kernel(y_ref, z_ref, d_ref, w_ref, b_ref, o_ref):
    y = y_ref[0] + y_ref[1] + d_ref[...] * z_ref[...]
    g = jax.nn.gelu(y)
    gate = jnp.dot(g.astype(jnp.bfloat16), w_ref[...], preferred_element_type=jnp.float32) + b_ref[...]
    o_ref[...] = g * jax.nn.sigmoid(gate)


def s5_finish(y2, z, d, glu_w_bf16, glu_b):
    n = z.shape[0]
    vec = pl.BlockSpec((1, W_GROUP), lambda i: (0, 0))
    return pl.pallas_call(
        _s5_finish_kernel,
        grid=(n // ROW_TILE,),
        in_specs=[pl.BlockSpec((2, ROW_TILE, W_GROUP), lambda i: (0, i, 0)),
                  pl.BlockSpec((ROW_TILE, W_GROUP), lambda i: (i, 0)),
                  vec, pl.BlockSpec((W_GROUP, W_GROUP), lambda i: (0, 0)), vec],
        out_specs=pl.BlockSpec((ROW_TILE, W_GROUP), lambda i: (i, 0)),
        out_shape=jax.ShapeDtypeStruct((n, W_GROUP), jnp.float32),
        compiler_params=pltpu.CompilerParams(dimension_semantics=("parallel",),
                                             vmem_limit_bytes=VMEM_LIMIT),
        name="s5_finish",
    )(y2, z, d.reshape(1, W_GROUP), glu_w_bf16, glu_b.reshape(1, W_GROUP))


def s5_mixer(z, n_seq, lam_re, lam_im, log_dt, b_re, b_im, c_re, c_im, d, glu_w, glu_b):
    lam, bmat, cmat = _s5_operators(lam_re, lam_im, log_dt, b_re, b_im, c_re, c_im)
    y2 = s5_scan(z, lam, bmat, cmat, n_seq)
    return s5_finish(y2, z, d, glu_w.astype(jnp.bfloat16), glu_b)


HY_L = 4096
HY_N = 2 * HY_L
HY_N1 = 64
HY_N2 = 128
HY_K1 = HY_N1 // 2 + 1
HY_KB = 11
HY_KP = 24
HY_C = V7X_LANES


def _hyena_dft_tables():
    n1 = np.arange(HY_N1 // 2)
    k1 = np.arange(HY_K1)
    ang1 = 2.0 * np.pi * np.outer(k1, n1) / HY_N1
    weight = np.where((k1 == 0) | (k1 == HY_N1 // 2), 1.0, 2.0) / HY_N
    nblk = HY_K1 // HY_KB
    fa = np.zeros((nblk, HY_KP, HY_N1 // 2), np.float32)
    fi = np.zeros((nblk, HY_N1 // 2, HY_KP), np.float32)
    for b in range(nblk):
        sl = slice(b * HY_KB, (b + 1) * HY_KB)
        fa[b, :HY_KB] = np.cos(ang1[sl])
        fa[b, HY_KB:2 * HY_KB] = -np.sin(ang1[sl])
        fi[b, :, :HY_KB] = (np.cos(ang1[sl]) * weight[sl, None]).T
        fi[b, :, HY_KB:2 * HY_KB] = (-np.sin(ang1[sl]) * weight[sl, None]).T
    n2 = np.arange(HY_N2)
    k = k1[:, None] + HY_N1 * np.arange(HY_N2)[None, :]
    ang = 2.0 * np.pi * (k[:, :, None] * n2[None, None, :] % HY_N) / HY_N
    c, s = np.cos(ang), np.sin(ang)
    g = np.concatenate([np.concatenate([c, s], axis=2), np.concatenate([-s, c], axis=2)], axis=1)
    gi = np.transpose(g, (0, 2, 1))
    bf = jnp.bfloat16
    return (jnp.asarray(fa, bf), jnp.asarray(fi, bf), jnp.asarray(g, bf), jnp.asarray(gi, bf))


def _hyena_filter_spectra(w1, b1, freq, w2, b2, w3, log_rate):
    L = HY_L
    t = jnp.arange(L, dtype=jnp.float32) / L
    ang = 2.0 * math.pi * t[:, None] * jnp.arange(1, HY_BANDS + 1, dtype=jnp.float32)
    feats = jnp.concatenate([t[:, None], jnp.sin(ang), jnp.cos(ang)], axis=-1)
    h = jnp.sin(freq[0] * (feats @ w1 + b1))
    h = jnp.sin(freq[1] * (h @ w2 + b2))
    h = (h @ w3).reshape(L, 2, HY_ORDER, W_GROUP)
    h = h * jnp.exp(-jnp.exp(log_rate)[None] * t[:, None, None, None])
    fwd, bwd = h[:, 0], h[:, 1]
    k = jnp.concatenate([fwd, jnp.zeros_like(fwd[:1]), bwd[:0:-1]], axis=0)
    k = k / jnp.sum(jnp.abs(k), axis=0, keepdims=True)
    kf = jnp.fft.fft(k, axis=0).reshape(HY_N2, HY_N1, HY_ORDER, W_GROUP)[:, :HY_K1]
    kf = jnp.transpose(kf, (2, 1, 0, 3))
    return jnp.stack([jnp.real(kf), jnp.imag(kf)], axis=2).astype(jnp.float32)


def _hy_pre_kernel(z_ref, zp_ref, zn_ref, w_ref, b_ref, v_ref, x1_ref, x2_ref):
    z = z_ref[...]
    zp, zn = _shifted(z, zp_ref[0], zn_ref[0])
    z = w_ref[0:1, :] * zp + w_ref[1:2, :] * z + w_ref[2:3, :] * zn + b_ref[...]
    v_ref[...] = z[:, 0:W_GROUP]
    x1_ref[...] = z[:, W_GROUP:2 * W_GROUP]
    x2_ref[...] = z[:, 2 * W_GROUP:3 * W_GROUP]


def hy_pre(z, n_seq, conv_w, conv_b):
    n = z.shape[0]
    prev, nxt = _halo_rows(z, n_seq, ROW_TILE)
    row = pl.BlockSpec((ROW_TILE, W_GROUP), lambda i: (i, 0))
    halo = pl.BlockSpec((1, 1, HY_IN), lambda i: (i, 0, 0))
    return pl.pallas_call(
        _hy_pre_kernel,
        grid=(n // ROW_TILE,),
        in_specs=[pl.BlockSpec((ROW_TILE, HY_IN), lambda i: (i, 0)), halo, halo,
                  pl.BlockSpec((3, HY_IN), lambda i: (0, 0)), pl.BlockSpec((1, HY_IN), lambda i: (0, 0))],
        out_specs=[row] * 3,
        out_shape=[jax.ShapeDtypeStruct((n, W_GROUP), jnp.float32)] * 3,
        compiler_params=pltpu.CompilerParams(dimension_semantics=("parallel",),
                                             vmem_limit_bytes=VMEM_LIMIT),
        name="hy_pre",
    )(z, prev, nxt, conv_w, conv_b.reshape(1, HY_IN))


def _hy_conv_kernel(u_ref, fa_ref, fi_ref, g_ref, gi_ref, kf_ref, o_ref, as_ref, bs_ref):
    kb = pl.program_id(2)
    nblk = HY_K1 // HY_KB
    bf = jnp.bfloat16

    @pl.when(kb == 0)
    def _():
        fa = fa_ref[...].reshape(nblk * HY_KP, HY_N1 // 2)
        for n2 in range(HY_N2):
            rows = u_ref[pl.ds(n2, HY_N1 // 2, stride=HY_N2), :].astype(bf)
            a_all = jnp.dot(fa, rows, preferred_element_type=jnp.float32)
            for b in range(nblk):
                as_ref[b, n2] = a_all[b * HY_KP:(b + 1) * HY_KP]

    bs_ref[kb, :, 2 * HY_KB:, :] = jnp.zeros((HY_N2, HY_KP - 2 * HY_KB, HY_C), jnp.float32)
    a_blk = as_ref.at[kb]
    b_blk = bs_ref.at[kb]
    for j in range(HY_KB):
        a = jnp.concatenate([a_blk[:, j, :], a_blk[:, HY_KB + j, :]], axis=0).astype(bf)
        x = jnp.dot(g_ref[j], a, preferred_element_type=jnp.float32)
        xr, xi = x[:HY_N2], x[HY_N2:]
        kr, ki = kf_ref[0, j, 0], kf_ref[0, j, 1]
        y = jnp.concatenate([xr * kr - xi * ki, xr * ki + xi * kr], axis=0).astype(bf)
        b = jnp.dot(gi_ref[j], y, preferred_element_type=jnp.float32)
        b_blk[:, j, :] = b[:HY_N2]
        b_blk[:, HY_KB + j, :] = b[HY_N2:]

    @pl.when(kb == nblk - 1)
    def _():
        for n2 in range(HY_N2):
            part = jnp.dot(fi_ref[0], bs_ref[0, n2].astype(bf), preferred_element_type=jnp.float32)
            for b in range(1, nblk):
                part = part + jnp.dot(fi_ref[b], bs_ref[b, n2].astype(bf), preferred_element_type=jnp.float32)
            o_ref[pl.ds(n2, HY_N1 // 2, stride=HY_N2), :] = part


def hy_conv(u, n_seq, kf, tables):
    n = u.shape[0]
    assert n // n_seq == HY_L
    fa, fi, g, gi = tables
    nblk = HY_K1 // HY_KB
    seq = pl.BlockSpec((HY_L, HY_C), lambda b, c, k: (b, c))
    return pl.pallas_call(
        _hy_conv_kernel,
        grid=(n_seq, W_GROUP // HY_C, nblk),
        in_specs=[seq,
                  pl.BlockSpec((nblk, HY_KP, HY_N1 // 2), lambda b, c, k: (0, 0, 0)),
                  pl.BlockSpec((nblk, HY_N1 // 2, HY_KP), lambda b, c, k: (0, 0, 0)),
                  pl.BlockSpec((HY_KB, 2 * HY_N2, 2 * HY_N2), lambda b, c, k: (k, 0, 0)),
                  pl.BlockSpec((HY_KB, 2 * HY_N2, 2 * HY_N2), lambda b, c, k: (k, 0, 0)),
                  pl.BlockSpec((1, HY_KB, 2, HY_N2, HY_C), lambda b, c, k: (0, k, 0, 0, c))],
        out_specs=seq,
        out_shape=jax.ShapeDtypeStruct((n, W_GROUP), jnp.float32),
        scratch_shapes=[pltpu.VMEM((nblk, HY_N2, HY_KP, HY_C), jnp.float32),
                        pltpu.VMEM((nblk, HY_N2, HY_KP, HY_C), jnp.float32)],
        compiler_params=pltpu.CompilerParams(dimension_semantics=("arbitrary", "arbitrary", "arbitrary"),
                                             vmem_limit_bytes=VMEM_LIMIT),
        name="hy_conv",
    )(u, fa, fi, g, gi, kf)


def _hy_gate_kernel(c_ref, u_ref, x_ref, d_ref, o_ref):
    o_ref[...] = x_ref[...] * (c_ref[...] + d_ref[...] * u_ref[...])


def hy_gate(c, u, gate, d):
    n = u.shape[0]
    row = pl.BlockSpec((ROW_TILE, W_GROUP), lambda i: (i, 0))
    return pl.pallas_call(
        _hy_gate_kernel,
        grid=(n // ROW_TILE,),
        in_specs=[row, row, row, pl.BlockSpec((1, W_GROUP), lambda i: (0, 0))],
        out_specs=row,
        out_shape=jax.ShapeDtypeStruct((n, W_GROUP), jnp.float32),
        compiler_params=pltpu.CompilerParams(dimension_semantics=("parallel",),
                                             vmem_limit_bytes=VMEM_LIMIT),
        name="hy_gate",
    )(c, u, gate, d.reshape(1, W_GROUP))


def hyena_mixer(z, n_seq, conv_w, conv_b, f_w1, f_b1, f_freq, f_w2, f_b2, f_w3, log_rate, d):
    v, x1, x2 = hy_pre(z, n_seq, conv_w, conv_b)
    kf = _hyena_filter_spectra(f_w1, f_b1, f_freq, f_w2, f_b2, f_w3, log_rate)
    tables = _hyena_dft_tables()
    u = v
    for o, gate in enumerate((x1, x2)):
        u = hy_gate(hy_conv(u, n_seq, kf[o:o + 1], tables), u, gate, d[o])
    return u


NA_BAND = 8
NA_KROWS = 2 * NA_BAND
NA_KBLK = 4


def _na_bias_tables(rel_bias, rows):
    hi = lax.Precision.HIGHEST
    qc = np.arange(GRID_W)
    ws = np.clip(qc - NA_WIN_C // 2, 0, GRID_W - NA_WIN_C)
    col_ok = (qc[None, :] >= ws[:, None]) & (qc[None, :] < ws[:, None] + NA_WIN_C)
    dc = np.clip(qc[None, :] - qc[:, None] + NA_WIN_C - 1, 0, 2 * NA_WIN_C - 2)
    dc_sel = (dc[..., None] == np.arange(2 * NA_WIN_C - 1)).astype(np.float32)
    col_bias = jnp.einsum('qkc,hdc->hdqk', dc_sel, rel_bias, precision=hi)
    tabs = []
    for band in (0, 1, rows // NA_BAND - 1):
        kb = int(np.clip(NA_BAND * band - NA_WIN_R // 2, 0, rows - NA_KROWS))
        r = NA_BAND * band + np.arange(NA_BAND)
        rs = np.clip(r - NA_WIN_R // 2, 0, rows - NA_WIN_R)
        krow = kb + np.arange(NA_KROWS)
        row_ok = (krow[None, :] >= rs[:, None]) & (krow[None, :] < rs[:, None] + NA_WIN_R)
        dr = np.clip(krow[None, :] - r[:, None] + NA_WIN_R - 1, 0, 2 * NA_WIN_R - 2)
        dr_sel = (dr[..., None] == np.arange(2 * NA_WIN_R - 1)).astype(np.float32)
        bias = jnp.einsum('jwd,hdqk->hjqwk', dr_sel, col_bias, precision=hi)
        ok = row_ok[:, None, :, None] & col_ok[None, :, None, :]
        tab = jnp.where(jnp.asarray(ok)[None], bias, NEG_INF)
        tabs.append(tab.reshape(NA_HEADS, NA_BAND * GRID_W, NA_KROWS * GRID_W))
    return jnp.stack(tabs)


def _na_kernel(q_ref, k0_ref, k1_ref, k2_ref, k3_ref, v0_ref, v1_ref, v2_ref, v3_ref, qg_ref, kg_ref,
               tab_ref, o_ref):
    q = q_ref[...]
    k = jnp.concatenate([k0_ref[...], k1_ref[...], k2_ref[...], k3_ref[...]], axis=0)
    v = jnp.concatenate([v0_ref[...], v1_ref[...], v2_ref[...], v3_ref[...]], axis=0)
    for h in range(NA_HEADS):
        sl = slice(h * NA_HEAD, (h + 1) * NA_HEAD)
        qh = (_rms(q[:, sl]) * qg_ref[...] * (NA_HEAD ** -0.5)).astype(jnp.bfloat16)
        kh = (_rms(k[:, sl]) * kg_ref[...]).astype(jnp.bfloat16)
        s = lax.dot_general(qh, kh, (((1,), (1,)), ((), ())), preferred_element_type=jnp.float32)
        s = s + tab_ref[0, h]
        p = jnp.exp(s - jnp.max(s, axis=-1, keepdims=True))
        den = jnp.sum(p, axis=-1, keepdims=True)
        o = jnp.dot(p.astype(jnp.bfloat16), v[:, sl].astype(jnp.bfloat16), preferred_element_type=jnp.float32)
        o_ref[:, sl] = o / den


def na_mixer(q, k, v, n_seq, q_g, k_g, rel_bias):
    n = q.shape[0]
    rows = n // n_seq // GRID_W
    n_band = rows // NA_BAND
    tab = _na_bias_tables(rel_bias, rows)
    qtok = NA_BAND * GRID_W
    ktok = NA_KBLK * GRID_W
    kblk_per_seq = rows // NA_KBLK

    def kv_spec(j):
        def index(i, b):
            first = jnp.clip(NA_BAND // NA_KBLK * i - 1, 0, kblk_per_seq - NA_KROWS // NA_KBLK)
            return (b * kblk_per_seq + first + j, 0)
        return pl.BlockSpec((ktok, W_GROUP), index)

    def tab_index(i, b):
        return (jnp.where(i == 0, 0, jnp.where(i == n_band - 1, 2, 1)), 0, 0, 0)

    qspec = pl.BlockSpec((qtok, W_GROUP), lambda i, b: (b * n_band + i, 0))
    gspec = pl.BlockSpec((1, NA_HEAD), lambda i, b: (0, 0))
    kvs = [kv_spec(j) for j in range(NA_KROWS // NA_KBLK)]
    return pl.pallas_call(
        _na_kernel,
        grid=(n_band, n_seq),
        in_specs=[qspec] + kvs + kvs + [gspec, gspec,
                  pl.BlockSpec((1, NA_HEADS, qtok, NA_KROWS * GRID_W), tab_index)],
        out_specs=qspec,
        out_shape=jax.ShapeDtypeStruct((n, W_GROUP), jnp.float32),
        compiler_params=pltpu.CompilerParams(dimension_semantics=("arbitrary", "arbitrary"),
                                             vmem_limit_bytes=VMEM_LIMIT),
        name="na_attn",
    )(q, k, k, k, k, v, v, v, v, q_g.reshape(1, NA_HEAD), k_g.reshape(1, NA_HEAD), tab)


def _trunk(x, p):
    Bn, L, _ = x.shape
    n = Bn * L
    x = x.reshape(n, D_MODEL)
    bf = jnp.bfloat16
    for l in range(DEPTH):
        z_s5, z_hy, z_rw, z_q, z_k, z_v = in_proj(x, p['ln1_g'][l], p['w_in'][l].astype(bf))
        y_s5 = s5_mixer(z_s5, Bn, p['s5_lam_re'][l], p['s5_lam_im'][l], p['s5_log_dt'][l],
                        p['s5_b_re'][l], p['s5_b_im'][l], p['s5_c_re'][l], p['s5_c_im'][l], p['s5_d'][l],
                        p['s5_glu_w'][l], p['s5_glu_b'][l])
        y_hy = hyena_mixer(z_hy, Bn, p['hy_conv_w'][l], p['hy_conv_b'][l], p['hy_f_w1'][l],
                           p['hy_f_b1'][l], p['hy_f_freq'][l], p['hy_f_w2'][l], p['hy_f_b2'][l],
                           p['hy_f_w3'][l], p['hy_log_rate'][l], p['hy_d'][l])
        y_rw = rwkv_mixer(z_rw, Bn, p['rw_mu'][l], p['rw_w0'][l], p['rw_w2'][l],
                          p['rw_a0'][l], p['rw_a2'][l], p['rw_g2'][l], p['rw_k_k'][l], p['rw_k_a'][l],
                          p['rw_r_k'][l], p['rw_ln_w'][l], p['rw_ln_b'][l])
        y_na = na_mixer(z_q, z_k, z_v, Bn, p['na_q_g'][l], p['na_k_g'][l], p['na_rel_bias'][l])
        ys = [y_s5, y_hy, y_rw, y_na]
        x = out_mlp(x, ys, p['grp_g'][l], p['w_out'][l].astype(bf), p['ln2_g'][l], p['w_mlp1'][l].astype(bf),
                    p['w_mlp2'][l].astype(bf))
    return x.reshape(Bn, L, D_MODEL)


def kernel(x_prompt, x_sample, ln1_g, w_in, s5_lam_re, s5_lam_im, s5_log_dt, s5_b_re, s5_b_im,
           s5_c_re, s5_c_im, s5_d, s5_glu_w, s5_glu_b, hy_conv_w, hy_conv_b, hy_f_w1, hy_f_b1,
           hy_f_freq, hy_f_w2, hy_f_b2, hy_f_w3, hy_log_rate, hy_d, rw_mu, rw_w0, rw_w2, rw_a0,
           rw_a2, rw_g2, rw_k_k, rw_k_a, rw_r_k, rw_ln_w, rw_ln_b, na_q_g, na_k_g, na_rel_bias,
           grp_g, w_out, ln2_g, w_mlp1, w_mlp2):
    p = dict(ln1_g=ln1_g, w_in=w_in, s5_lam_re=s5_lam_re, s5_lam_im=s5_lam_im, s5_log_dt=s5_log_dt,
             s5_b_re=s5_b_re, s5_b_im=s5_b_im, s5_c_re=s5_c_re, s5_c_im=s5_c_im, s5_d=s5_d,
             s5_glu_w=s5_glu_w, s5_glu_b=s5_glu_b, hy_conv_w=hy_conv_w, hy_conv_b=hy_conv_b,
             hy_f_w1=hy_f_w1, hy_f_b1=hy_f_b1, hy_f_freq=hy_f_freq, hy_f_w2=hy_f_w2, hy_f_b2=hy_f_b2,
             hy_f_w3=hy_f_w3, hy_log_rate=hy_log_rate, hy_d=hy_d, rw_mu=rw_mu, rw_w0=rw_w0, rw_w2=rw_w2,
             rw_a0=rw_a0, rw_a2=rw_a2, rw_g2=rw_g2, rw_k_k=rw_k_k, rw_k_a=rw_k_a, rw_r_k=rw_r_k,
             rw_ln_w=rw_ln_w, rw_ln_b=rw_ln_b, na_q_g=na_q_g, na_k_g=na_k_g, na_rel_bias=na_rel_bias,
             grp_g=grp_g, w_out=w_out, ln2_g=ln2_g, w_mlp1=w_mlp1, w_mlp2=w_mlp2)
    nb = x_prompt.shape[0]
    y = _trunk(jnp.concatenate([x_prompt, x_sample], axis=0), p)
    return (y[:nb], y[nb:])
```

```python
import math

import jax
import jax.numpy as jnp
import numpy as np
from jax import lax
from jax.experimental import pallas as pl
from jax.experimental.pallas import tpu as pltpu

D_MODEL = 1024
DEPTH = 4
GRID_W = 64
W_GROUP = 256
N_MIXERS = 4
D_FF = 4 * D_MODEL
NORM_EPS = 1e-6

S5_CH = 16
S5_GROUPS = W_GROUP // S5_CH
S5_STATE = 64
S5_IN = W_GROUP

HY_ORDER = 2
HY_BANDS = 8
HY_IN = (HY_ORDER + 1) * W_GROUP

RW_HEAD = 64
RW_HEADS = W_GROUP // RW_HEAD
RW_DECAY_RANK = 64
RW_A_RANK = 64
RW_G_RANK = 128
RW_LN_EPS = 64e-5
RW_IN = 3 * W_GROUP + RW_DECAY_RANK + RW_A_RANK + RW_G_RANK
RW_SPLITS = (W_GROUP, 2 * W_GROUP, 3 * W_GROUP, 3 * W_GROUP + RW_DECAY_RANK,
             3 * W_GROUP + RW_DECAY_RANK + RW_A_RANK)

NA_HEAD = 64
NA_HEADS = W_GROUP // NA_HEAD
NA_WIN_R = 8
NA_WIN_C = 16
NEG_INF = -1e30

D_IN = S5_IN + HY_IN + RW_IN + 3 * W_GROUP

V7X_LANES = 128
VMEM_LIMIT = 48 * 1024 * 1024

ROW_TILE = 512
FF_TILE = 1024
MLP_ROW_TILE = 1024
SCAN_T = 16
SCAN_UNROLL = 32


def _rms(x):
    return x * lax.rsqrt(jnp.mean(x * x, axis=-1, keepdims=True) + NORM_EPS)


IN_WIDTHS = (S5_IN, HY_IN, RW_IN, W_GROUP, W_GROUP, W_GROUP)


def _in_proj_kernel(x_ref, g_ref, w_ref, *o_refs):
    h = (_rms(x_ref[...]) * g_ref[...]).astype(jnp.bfloat16)
    lo = 0
    for o_ref, width in zip(o_refs, IN_WIDTHS):
        o_ref[...] = jnp.dot(h, w_ref[:, lo:lo + width], preferred_element_type=jnp.float32)
        lo += width


def in_proj(x, g, w_bf16):
    n = x.shape[0]
    return pl.pallas_call(
        _in_proj_kernel,
        grid=(n // ROW_TILE,),
        in_specs=[pl.BlockSpec((ROW_TILE, D_MODEL), lambda i: (i, 0)),
                  pl.BlockSpec((1, D_MODEL), lambda i: (0, 0)),
                  pl.BlockSpec((D_MODEL, D_IN), lambda i: (0, 0))],
        out_specs=[pl.BlockSpec((ROW_TILE, w), lambda i: (i, 0)) for w in IN_WIDTHS],
        out_shape=[jax.ShapeDtypeStruct((n, w), jnp.float32) for w in IN_WIDTHS],
        compiler_params=pltpu.CompilerParams(dimension_semantics=("parallel",),
                                             vmem_limit_bytes=VMEM_LIMIT),
        name="in_proj",
    )(x, g.reshape(1, D_MODEL), w_bf16)


def _out_mlp_kernel(x_ref, y0_ref, y1_ref, y2_ref, y3_ref, gg_ref, wo_ref, g_ref, w1_ref, w2_ref, o_ref, h_ref):
    j = pl.program_id(1)

    @pl.when(j == 0)
    def _():
        x = x_ref[...]
        for i, y_ref in enumerate((y0_ref, y1_ref, y2_ref, y3_ref)):
            n = (_rms(y_ref[...]) * gg_ref[i:i + 1, :]).astype(jnp.bfloat16)
            x = x + jnp.dot(n, wo_ref[i * W_GROUP:(i + 1) * W_GROUP, :], preferred_element_type=jnp.float32)
        h_ref[...] = (_rms(x) * g_ref[...]).astype(jnp.bfloat16)
        o_ref[...] = x

    a = jnp.dot(h_ref[...], w1_ref[...], preferred_element_type=jnp.float32)
    a = jnp.square(jnp.maximum(a, 0.0)).astype(jnp.bfloat16)
    o_ref[...] += jnp.dot(a, w2_ref[...], preferred_element_type=jnp.float32)


def out_mlp(x, ys, grp_g, wo_bf16, g, w1_bf16, w2_bf16):
    n = x.shape[0]
    row = lambda w: pl.BlockSpec((MLP_ROW_TILE, w), lambda i, j: (i, 0))
    return pl.pallas_call(
        _out_mlp_kernel,
        grid=(n // MLP_ROW_TILE, D_FF // FF_TILE),
        in_specs=[row(D_MODEL)] + [row(W_GROUP)] * N_MIXERS
                 + [pl.BlockSpec((N_MIXERS, W_GROUP), lambda i, j: (0, 0)),
                    pl.BlockSpec((D_MODEL, D_MODEL), lambda i, j: (0, 0)),
                    pl.BlockSpec((1, D_MODEL), lambda i, j: (0, 0)),
                    pl.BlockSpec((D_MODEL, FF_TILE), lambda i, j: (0, j)),
                    pl.BlockSpec((FF_TILE, D_MODEL), lambda i, j: (j, 0))],
        out_specs=row(D_MODEL),
        out_shape=jax.ShapeDtypeStruct((n, D_MODEL), jnp.float32),
        scratch_shapes=[pltpu.VMEM((MLP_ROW_TILE, D_MODEL), jnp.bfloat16)],
        compiler_params=pltpu.CompilerParams(dimension_semantics=("parallel", "arbitrary"),
                                             vmem_limit_bytes=VMEM_LIMIT),
        name="out_mlp",
    )(x, *ys, grp_g, wo_bf16, g.reshape(1, D_MODEL), w1_bf16, w2_bf16)


def _head_sum(x):
    lane = lax.broadcasted_iota(jnp.int32, (W_GROUP, W_GROUP), 0) // RW_HEAD
    col = lax.broadcasted_iota(jnp.int32, (W_GROUP, W_GROUP), 1) // RW_HEAD
    ones = (lane == col).astype(jnp.float32)
    return jnp.dot(x, ones, precision=lax.Precision.HIGHEST, preferred_element_type=jnp.float32)


def _softplus(x):
    return jnp.maximum(x, 0.0) + jnp.log(1.0 + jnp.exp(-jnp.abs(x)))


def _shifted(z, prev_row, next_row):
    t = z.shape[0]
    row = lax.broadcasted_iota(jnp.int32, z.shape, 0)
    zp = jnp.where(row == 0, prev_row, pltpu.roll(z, 1, 0))
    zn = jnp.where(row == t - 1, next_row, pltpu.roll(z, t - 1, 0))
    return zp, zn


def _store_heads(o_ref, x):
    for h in range(RW_HEADS):
        o_ref[h] = x[:, h * RW_HEAD:(h + 1) * RW_HEAD]


def _rwkv_prep_kernel(z_ref, zp_ref, zn_ref, mu_ref, w0_ref, w2_ref, a0_ref, a2_ref, g2_ref, kk_ref, ka_ref,
                      rk_ref, nkk_ref, r_ref, v_ref, dec0_ref, dec1_ref, kd0_ref, kd1_ref, b0_ref, b1_ref,
                      g_ref, bonus_ref):
    z = z_ref[...]
    zp, zn = _shifted(z, zp_ref[0], zn_ref[0])
    z = z + mu_ref[...] * (0.5 * (zp + zn) - z)
    r = z[:, 0:W_GROUP]
    k = z[:, W_GROUP:2 * W_GROUP]
    v = z[:, 2 * W_GROUP:3 * W_GROUP]
    wd = z[:, RW_SPLITS[2]:RW_SPLITS[3]]
    ad = z[:, RW_SPLITS[3]:RW_SPLITS[4]]
    gd = z[:, RW_SPLITS[4]:RW_IN]
    bf = jnp.bfloat16
    g_ref[...] = jnp.dot(jax.nn.sigmoid(gd).astype(bf), g2_ref[...], preferred_element_type=jnp.float32)
    kk = k * kk_ref[...]
    kk = kk / jnp.maximum(jnp.sqrt(_head_sum(kk * kk)), 1e-12)
    _store_heads(nkk_ref, -kk)
    _store_heads(r_ref, r)
    _store_heads(v_ref, v)
    bonus_ref[...] = _head_sum(r * k * rk_ref[...]) * v
    tw = jnp.tanh(wd).astype(bf)
    adb = ad.astype(bf)
    for d, (dec_ref, kd_ref, b_ref) in enumerate(((dec0_ref, kd0_ref, b0_ref), (dec1_ref, kd1_ref, b1_ref))):
        w = w0_ref[d:d + 1, :] + jnp.dot(tw, w2_ref[d], preferred_element_type=jnp.float32)
        w = -_softplus(-w) - 0.5
        _store_heads(dec_ref, jnp.exp(-jnp.exp(w)))
        a = jax.nn.sigmoid(a0_ref[d:d + 1, :] + jnp.dot(adb, a2_ref[d], preferred_element_type=jnp.float32))
        _store_heads(kd_ref, k * (1.0 + (a - 1.0) * ka_ref[...]))
        _store_heads(b_ref, kk * a)


def _halo_rows(z, n_seq, tile):
    n, c = z.shape
    zt = z.reshape(n_seq, n // n_seq // tile, tile, c)
    zero = jnp.zeros((n_seq, 1, c), z.dtype)
    prev = jnp.concatenate([zero, zt[:, :-1, -1]], axis=1).reshape(n // tile, 1, c)
    nxt = jnp.concatenate([zt[:, 1:, 0], zero], axis=1).reshape(n // tile, 1, c)
    return prev, nxt


def rwkv_prep(z, n_seq, mu, w0, w2, a0, a2, g2, k_k, k_a, r_k):
    n = z.shape[0]
    prev, nxt = _halo_rows(z, n_seq, ROW_TILE)
    bf = jnp.bfloat16
    full = lambda *s: pl.BlockSpec(s, lambda i: (0,) * len(s))
    row = pl.BlockSpec((ROW_TILE, W_GROUP), lambda i: (i, 0))
    heads = pl.BlockSpec((RW_HEADS, ROW_TILE, RW_HEAD), lambda i: (0, i, 0))
    halo = pl.BlockSpec((1, 1, RW_IN), lambda i: (i, 0, 0))
    vec = lambda x: x.reshape(1, -1)
    return pl.pallas_call(
        _rwkv_prep_kernel,
        grid=(n // ROW_TILE,),
        in_specs=[pl.BlockSpec((ROW_TILE, RW_IN), lambda i: (i, 0)), halo, halo, full(1, RW_IN),
                  full(2, W_GROUP), full(2, RW_DECAY_RANK, W_GROUP), full(2, W_GROUP),
                  full(2, RW_A_RANK, W_GROUP), full(RW_G_RANK, W_GROUP), full(1, W_GROUP), full(1, W_GROUP),
                  full(1, W_GROUP)],
        out_specs=[heads] * 9 + [row] * 2,
        out_shape=[jax.ShapeDtypeStruct((RW_HEADS, n, RW_HEAD), jnp.float32)] * 9
                  + [jax.ShapeDtypeStruct((n, W_GROUP), jnp.float32)] * 2,
        compiler_params=pltpu.CompilerParams(dimension_semantics=("parallel",),
                                             vmem_limit_bytes=VMEM_LIMIT),
        name="rwkv_prep",
    )(z, prev, nxt, vec(mu), w0, w2.astype(bf), a0, a2.astype(bf), g2.astype(bf), vec(k_k), vec(k_a), vec(r_k))


def _rwkv_scan_kernel(af_ref, wf_ref, bf_ref, kf_ref, rf_ref, vf_ref, ab_ref, wb_ref, bb_ref, kb_ref, rb_ref,
                      vb_ref, yf_ref, yb_ref, s_ref):
    @pl.when(pl.program_id(0) == 0)
    def _():
        s_ref[...] = jnp.zeros_like(s_ref)

    dirs = ((af_ref, wf_ref, bf_ref, kf_ref, rf_ref, vf_ref, yf_ref),
            (ab_ref, wb_ref, bb_ref, kb_ref, rb_ref, vb_ref, yb_ref))

    def step(i, carry):
        for j, (a_ref, w_ref, b_ref, k_ref, r_ref, v_ref, y_ref) in enumerate(dirs):
            t = i if j == 0 else SCAN_T - 1 - i
            vt = v_ref[t]

            def reduce_a(k, sa):
                return sa + s_ref[j, k] * a_ref[t, pl.ds(k, 1), :]

            sa = lax.fori_loop(0, RW_HEAD, reduce_a, jnp.zeros_like(vt), unroll=SCAN_UNROLL)

            def update(k, y):
                s = (s_ref[j, k] * w_ref[t, pl.ds(k, 1), :] + sa * b_ref[t, pl.ds(k, 1), :]
                     + vt * k_ref[t, pl.ds(k, 1), :])
                s_ref[j, k] = s
                return y + s * r_ref[t, pl.ds(k, 1), :]

            y_ref[t] = lax.fori_loop(0, RW_HEAD, update, jnp.zeros_like(vt), unroll=SCAN_UNROLL)
        return carry

    lax.fori_loop(0, SCAN_T, step, 0)


def rwkv_scan(a, r, v, w_f, b_f, k_f, w_b, b_b, k_b):
    L, _, nc = a.shape
    nblk = L // SCAN_T
    fwd = pl.BlockSpec((SCAN_T, RW_HEAD, nc), lambda i: (i, 0, 0))
    bwd = pl.BlockSpec((SCAN_T, RW_HEAD, nc), lambda i: (nblk - 1 - i, 0, 0))
    out = jax.ShapeDtypeStruct((L, RW_HEAD, nc), jnp.float32)
    return pl.pallas_call(
        _rwkv_scan_kernel,
        grid=(nblk,),
        in_specs=[fwd] * 6 + [bwd] * 6,
        out_specs=[fwd, bwd],
        out_shape=[out, out],
        scratch_shapes=[pltpu.VMEM((2, RW_HEAD, RW_HEAD, nc), jnp.float32)],
        compiler_params=pltpu.CompilerParams(dimension_semantics=("arbitrary",),
                                             vmem_limit_bytes=VMEM_LIMIT),
        name="rwkv_scan",
    )(a, w_f, b_f, k_f, r, v, a, w_b, b_b, k_b, r, v)


def _rwkv_post_kernel(yf_ref, yb_ref, bonus_ref, g_ref, lw_ref, lb_ref, o_ref):
    y = jnp.concatenate([yf_ref[h] + yb_ref[h] for h in range(RW_HEADS)], axis=-1)
    mean = _head_sum(y) * (1.0 / RW_HEAD)
    c = y - mean
    var = _head_sum(c * c) * (1.0 / RW_HEAD)
    y = c * lax.rsqrt(var + RW_LN_EPS) * lw_ref[...] + lb_ref[...]
    o_ref[...] = (y + bonus_ref[...]) * g_ref[...]


def rwkv_post(y_f, y_b, bonus, g, ln_w, ln_b):
    n = bonus.shape[0]
    row = pl.BlockSpec((ROW_TILE, W_GROUP), lambda i: (i, 0))
    heads = pl.BlockSpec((RW_HEADS, ROW_TILE, RW_HEAD), lambda i: (0, i, 0))
    vec = pl.BlockSpec((1, W_GROUP), lambda i: (0, 0))
    return pl.pallas_call(
        _rwkv_post_kernel,
        grid=(n // ROW_TILE,),
        in_specs=[heads, heads, row, row, vec, vec],
        out_specs=row,
        out_shape=jax.ShapeDtypeStruct((n, W_GROUP), jnp.float32),
        compiler_params=pltpu.CompilerParams(dimension_semantics=("parallel",),
                                             vmem_limit_bytes=VMEM_LIMIT),
        name="rwkv_post",
    )(y_f, y_b, bonus, g, ln_w.reshape(1, W_GROUP), ln_b.reshape(1, W_GROUP))


def rwkv_mixer(z, n_seq, mu, w0, w2, a0, a2, g2, k_k, k_a, r_k, ln_w, ln_b):
    n = z.shape[0]
    L = n // n_seq
    n_chain = n_seq * RW_HEADS
    assert n_chain <= V7X_LANES
    nkk, r, v, dec0, dec1, kd0, kd1, b0, b1, g, bonus = rwkv_prep(z, n_seq, mu, w0, w2, a0, a2, g2, k_k, k_a, r_k)

    def chains(x):
        return x.reshape(RW_HEADS, n_seq, L, RW_HEAD).transpose(2, 3, 0, 1).reshape(L, RW_HEAD, n_chain)

    def tokens(y):
        return y.reshape(L, RW_HEAD, RW_HEADS, n_seq).transpose(2, 3, 0, 1).reshape(RW_HEADS, n, RW_HEAD)

    y_f, y_b = rwkv_scan(chains(nkk), chains(r), chains(v), chains(dec0), chains(b0), chains(kd0),
                         chains(dec1), chains(b1), chains(kd1))
    return rwkv_post(tokens(y_f), tokens(y_b), bonus, g, ln_w, ln_b)


S5_NSTATE = S5_GROUPS * S5_STATE
S5_T = 1024


def _cmul(ar, ai, br, bi):
    return ar * br - ai * bi, ar * bi + ai * br


def _s5_operators(lam_re, lam_im, log_dt, b_re, b_im, c_re, c_im):
    dt = jnp.exp(log_dt)[..., None]
    mag = jnp.exp(lam_re * dt)
    ab_re = mag * jnp.cos(lam_im * dt)
    ab_im = mag * jnp.sin(lam_im * dt)
    den = lam_re * lam_re + lam_im * lam_im
    n_re = ab_re - 1.0
    f_re = (n_re * lam_re + ab_im * lam_im) / den
    f_im = (ab_im * lam_re - n_re * lam_im) / den
    bb_re, bb_im = _cmul(f_re[..., None], f_im[..., None], b_re, b_im)
    eye = jnp.eye(S5_GROUPS, dtype=jnp.float32)

    def in_map(bb):
        return jnp.einsum('gh,dhnc->dgchn', eye, bb).reshape(2, W_GROUP, S5_NSTATE)

    def out_map(c):
        return jnp.einsum('hg,dgcn->dhngc', eye, c).reshape(2, S5_NSTATE, W_GROUP)

    bmat = jnp.concatenate([in_map(bb_re), in_map(bb_im)], axis=-1)
    cmat = jnp.concatenate([out_map(c_re), -out_map(c_im)], axis=1)
    lam = jnp.stack([ab_re.reshape(2, S5_NSTATE), ab_im.reshape(2, S5_NSTATE)], axis=1)
    return lam, bmat.astype(jnp.bfloat16), cmat.astype(jnp.bfloat16)


def _s5_scan_kernel(u_ref, lam_ref, bmat_ref, cmat_ref, y_ref, st_ref, carry_ref):
    d = pl.program_id(1)

    @pl.when(pl.program_id(2) == 0)
    def _():
        carry_ref[...] = jnp.zeros_like(carry_ref)

    st_ref[...] = jnp.dot(u_ref[...].astype(jnp.bfloat16), bmat_ref[0], preferred_element_type=jnp.float32)
    lam_r = lam_ref[0, 0:1, :]
    lam_i = lam_ref[0, 1:2, :]
    re = slice(0, S5_NSTATE)
    im = slice(S5_NSTATE, 2 * S5_NSTATE)

    def step(i, carry):
        sr, si = carry
        t = i + d * (S5_T - 1 - 2 * i)
        nr = lam_r * sr - lam_i * si + st_ref[pl.ds(t, 1), re]
        ni = lam_r * si + lam_i * sr + st_ref[pl.ds(t, 1), im]
        st_ref[pl.ds(t, 1), re] = nr
        st_ref[pl.ds(t, 1), im] = ni
        return nr, ni

    sr, si = lax.fori_loop(0, S5_T, step, (carry_ref[0:1, re], carry_ref[0:1, im]), unroll=4)
    carry_ref[0:1, re] = sr
    carry_ref[0:1, im] = si
    y_ref[0] = jnp.dot(st_ref[...].astype(jnp.bfloat16), cmat_ref[0], preferred_element_type=jnp.float32)


def s5_scan(z, lam, bmat, cmat, n_seq):
    n = z.shape[0]
    nch = n // n_seq // S5_T

    def row_block(b, d, c):
        return b * nch + c + d * (nch - 1 - 2 * c)

    return pl.pallas_call(
        _s5_scan_kernel,
        grid=(n_seq, 2, nch),
        in_specs=[pl.BlockSpec((S5_T, W_GROUP), lambda b, d, c: (row_block(b, d, c), 0)),
                  pl.BlockSpec((1, 2, S5_NSTATE), lambda b, d, c: (d, 0, 0)),
                  pl.BlockSpec((1, W_GROUP, 2 * S5_NSTATE), lambda b, d, c: (d, 0, 0)),
                  pl.BlockSpec((1, 2 * S5_NSTATE, W_GROUP), lambda b, d, c: (d, 0, 0))],
        out_specs=pl.BlockSpec((1, S5_T, W_GROUP), lambda b, d, c: (d, row_block(b, d, c), 0)),
        out_shape=jax.ShapeDtypeStruct((2, n, W_GROUP), jnp.float32),
        scratch_shapes=[pltpu.VMEM((S5_T, 2 * S5_NSTATE), jnp.float32),
                        pltpu.VMEM((8, 2 * S5_NSTATE), jnp.float32)],
        compiler_params=pltpu.CompilerParams(dimension_semantics=("arbitrary", "arbitrary", "arbitrary"),
                                             vmem_limit_bytes=VMEM_LIMIT),
        name="s5_scan",
    )(z, lam, bmat, cmat)


def _s5_finish_kernel(y_ref, z_ref, d_ref, w_ref, b_ref, o_ref):
    y = y_ref[0] + y_ref[1] + d_ref[...] * z_ref[...]
    g = jax.nn.gelu(y)
    gate = jnp.dot(g.astype(jnp.bfloat16), w_ref[...], preferred_element_type=jnp.float32) + b_ref[...]
    o_ref[...] = g * jax.nn.sigmoid(gate)


def s5_finish(y2, z, d, glu_w_bf16, glu_b):
    n = z.shape[0]
    vec = pl.BlockSpec((1, W_GROUP), lambda i: (0, 0))
    return pl.pallas_call(
        _s5_finish_kernel,
        grid=(n // ROW_TILE,),
        in_specs=[pl.BlockSpec((2, ROW_TILE, W_GROUP), lambda i: (0, i, 0)),
                  pl.BlockSpec((ROW_TILE, W_GROUP), lambda i: (i, 0)),
                  vec, pl.BlockSpec((W_GROUP, W_GROUP), lambda i: (0, 0)), vec],
        out_specs=pl.BlockSpec((ROW_TILE, W_GROUP), lambda i: (i, 0)),
        out_shape=jax.ShapeDtypeStruct((n, W_GROUP), jnp.float32),
        compiler_params=pltpu.CompilerParams(dimension_semantics=("parallel",),
                                             vmem_limit_bytes=VMEM_LIMIT),
        name="s5_finish",
    )(y2, z, d.reshape(1, W_GROUP), glu_w_bf16, glu_b.reshape(1, W_GROUP))


def s5_mixer(z, n_seq, lam_re, lam_im, log_dt, b_re, b_im, c_re, c_im, d, glu_w, glu_b):
    lam, bmat, cmat = _s5_operators(lam_re, lam_im, log_dt, b_re, b_im, c_re, c_im)
    y2 = s5_scan(z, lam, bmat, cmat, n_seq)
    return s5_finish(y2, z, d, glu_w.astype(jnp.bfloat16), glu_b)


HY_L = 4096
HY_N = 2 * HY_L
HY_N1 = 64
HY_N2 = 128
HY_K1 = HY_N1 // 2 + 1
HY_KB = 11
HY_KP = 24
HY_C = V7X_LANES


def _hyena_dft_tables():
    n1 = np.arange(HY_N1 // 2)
    k1 = np.arange(HY_K1)
    ang1 = 2.0 * np.pi * np.outer(k1, n1) / HY_N1
    weight = np.where((k1 == 0) | (k1 == HY_N1 // 2), 1.0, 2.0) / HY_N
    nblk = HY_K1 // HY_KB
    fa = np.zeros((nblk, HY_KP, HY_N1 // 2), np.float32)
    fi = np.zeros((nblk, HY_N1 // 2, HY_KP), np.float32)
    for b in range(nblk):
        sl = slice(b * HY_KB, (b + 1) * HY_KB)
        fa[b, :HY_KB] = np.cos(ang1[sl])
        fa[b, HY_KB:2 * HY_KB] = -np.sin(ang1[sl])
        fi[b, :, :HY_KB] = (np.cos(ang1[sl]) * weight[sl, None]).T
        fi[b, :, HY_KB:2 * HY_KB] = (-np.sin(ang1[sl]) * weight[sl, None]).T
    n2 = np.arange(HY_N2)
    k = k1[:, None] + HY_N1 * np.arange(HY_N2)[None, :]
    ang = 2.0 * np.pi * (k[:, :, None] * n2[None, None, :] % HY_N) / HY_N
    c, s = np.cos(ang), np.sin(ang)
    g = np.concatenate([np.concatenate([c, s], axis=2), np.concatenate([-s, c], axis=2)], axis=1)
    gi = np.transpose(g, (0, 2, 1))
    bf = jnp.bfloat16
    return (jnp.asarray(fa, bf), jnp.asarray(fi, bf), jnp.asarray(g, bf), jnp.asarray(gi, bf))


def _hyena_filter_spectra(w1, b1, freq, w2, b2, w3, log_rate):
    L = HY_L
    t = jnp.arange(L, dtype=jnp.float32) / L
    ang = 2.0 * math.pi * t[:, None] * jnp.arange(1, HY_BANDS + 1, dtype=jnp.float32)
    feats = jnp.concatenate([t[:, None], jnp.sin(ang), jnp.cos(ang)], axis=-1)
    h = jnp.sin(freq[0] * (feats @ w1 + b1))
    h = jnp.sin(freq[1] * (h @ w2 + b2))
    h = (h @ w3).reshape(L, 2, HY_ORDER, W_GROUP)
    h = h * jnp.exp(-jnp.exp(log_rate)[None] * t[:, None, None, None])
    fwd, bwd = h[:, 0], h[:, 1]
    k = jnp.concatenate([fwd, jnp.zeros_like(fwd[:1]), bwd[:0:-1]], axis=0)
    k = k / jnp.sum(jnp.abs(k), axis=0, keepdims=True)
    kf = jnp.fft.fft(k, axis=0).reshape(HY_N2, HY_N1, HY_ORDER, W_GROUP)[:, :HY_K1]
    kf = jnp.transpose(kf, (2, 1, 0, 3))
    return jnp.stack([jnp.real(kf), jnp.imag(kf)], axis=2).astype(jnp.float32)


def _hy_pre_kernel(z_ref, zp_ref, zn_ref, w_ref, b_ref, v_ref, x1_ref, x2_ref):
    z = z_ref[...]
    zp, zn = _shifted(z, zp_ref[0], zn_ref[0])
    z = w_ref[0:1, :] * zp + w_ref[1:2, :] * z + w_ref[2:3, :] * zn + b_ref[...]
    v_ref[...] = z[:, 0:W_GROUP]
    x1_ref[...] = z[:, W_GROUP:2 * W_GROUP]
    x2_ref[...] = z[:, 2 * W_GROUP:3 * W_GROUP]


def hy_pre(z, n_seq, conv_w, conv_b):
    n = z.shape[0]
    prev, nxt = _halo_rows(z, n_seq, ROW_TILE)
    row = pl.BlockSpec((ROW_TILE, W_GROUP), lambda i: (i, 0))
    halo = pl.BlockSpec((1, 1, HY_IN), lambda i: (i, 0, 0))
    return pl.pallas_call(
        _hy_pre_kernel,
        grid=(n // ROW_TILE,),
        in_specs=[pl.BlockSpec((ROW_TILE, HY_IN), lambda i: (i, 0)), halo, halo,
                  pl.BlockSpec((3, HY_IN), lambda i: (0, 0)), pl.BlockSpec((1, HY_IN), lambda i: (0, 0))],
        out_specs=[row] * 3,
        out_shape=[jax.ShapeDtypeStruct((n, W_GROUP), jnp.float32)] * 3,
        compiler_params=pltpu.CompilerParams(dimension_semantics=("parallel",),
                                             vmem_limit_bytes=VMEM_LIMIT),
        name="hy_pre",
    )(z, prev, nxt, conv_w, conv_b.reshape(1, HY_IN))


def _hy_conv_kernel(u_ref, fa_ref, fi_ref, g_ref, gi_ref, kf_ref, o_ref, as_ref, bs_ref):
    kb = pl.program_id(2)
    nblk = HY_K1 // HY_KB
    bf = jnp.bfloat16

    @pl.when(kb == 0)
    def _():
        fa = fa_ref[...].reshape(nblk * HY_KP, HY_N1 // 2)
        for n2 in range(HY_N2):
            rows = u_ref[pl.ds(n2, HY_N1 // 2, stride=HY_N2), :].astype(bf)
            a_all = jnp.dot(fa, rows, preferred_element_type=jnp.float32)
            for b in range(nblk):
                as_ref[b, n2] = a_all[b * HY_KP:(b + 1) * HY_KP]

    bs_ref[kb, :, 2 * HY_KB:, :] = jnp.zeros((HY_N2, HY_KP - 2 * HY_KB, HY_C), jnp.float32)
    a_blk = as_ref.at[kb]
    b_blk = bs_ref.at[kb]
    for j in range(HY_KB):
        a = jnp.concatenate([a_blk[:, j, :], a_blk[:, HY_KB + j, :]], axis=0).astype(bf)
        x = jnp.dot(g_ref[j], a, preferred_element_type=jnp.float32)
        xr, xi = x[:HY_N2], x[HY_N2:]
        kr, ki = kf_ref[0, j, 0], kf_ref[0, j, 1]
        y = jnp.concatenate([xr * kr - xi * ki, xr * ki + xi * kr], axis=0).astype(bf)
        b = jnp.dot(gi_ref[j], y, preferred_element_type=jnp.float32)
        b_blk[:, j, :] = b[:HY_N2]
        b_blk[:, HY_KB + j, :] = b[HY_N2:]

    @pl.when(kb == nblk - 1)
    def _():
        for n2 in range(HY_N2):
            part = jnp.dot(fi_ref[0], bs_ref[0, n2].astype(bf), preferred_element_type=jnp.float32)
            for b in range(1, nblk):
                part = part + jnp.dot(fi_ref[b], bs_ref[b, n2].astype(bf), preferred_element_type=jnp.float32)
            o_ref[pl.ds(n2, HY_N1 // 2, stride=HY_N2), :] = part


def hy_conv(u, n_seq, kf, tables):
    n = u.shape[0]
    assert n // n_seq == HY_L
    fa, fi, g, gi = tables
    nblk = HY_K1 // HY_KB
    seq = pl.BlockSpec((HY_L, HY_C), lambda b, c, k: (b, c))
    return pl.pallas_call(
        _hy_conv_kernel,
        grid=(n_seq, W_GROUP // HY_C, nblk),
        in_specs=[seq,
                  pl.BlockSpec((nblk, HY_KP, HY_N1 // 2), lambda b, c, k: (0, 0, 0)),
                  pl.BlockSpec((nblk, HY_N1 // 2, HY_KP), lambda b, c, k: (0, 0, 0)),
                  pl.BlockSpec((HY_KB, 2 * HY_N2, 2 * HY_N2), lambda b, c, k: (k, 0, 0)),
                  pl.BlockSpec((HY_KB, 2 * HY_N2, 2 * HY_N2), lambda b, c, k: (k, 0, 0)),
                  pl.BlockSpec((1, HY_KB, 2, HY_N2, HY_C), lambda b, c, k: (0, k, 0, 0, c))],
        out_specs=seq,
        out_shape=jax.ShapeDtypeStruct((n, W_GROUP), jnp.float32),
        scratch_shapes=[pltpu.VMEM((nblk, HY_N2, HY_KP, HY_C), jnp.float32),
                        pltpu.VMEM((nblk, HY_N2, HY_KP, HY_C), jnp.float32)],
        compiler_params=pltpu.CompilerParams(dimension_semantics=("arbitrary", "arbitrary", "arbitrary"),
                                             vmem_limit_bytes=VMEM_LIMIT),
        name="hy_conv",
    )(u, fa, fi, g, gi, kf)


def _hy_gate_kernel(c_ref, u_ref, x_ref, d_ref, o_ref):
    o_ref[...] = x_ref[...] * (c_ref[...] + d_ref[...] * u_ref[...])


def hy_gate(c, u, gate, d):
    n = u.shape[0]
    row = pl.BlockSpec((ROW_TILE, W_GROUP), lambda i: (i, 0))
    return pl.pallas_call(
        _hy_gate_kernel,
        grid=(n // ROW_TILE,),
        in_specs=[row, row, row, pl.BlockSpec((1, W_GROUP), lambda i: (0, 0))],
        out_specs=row,
        out_shape=jax.ShapeDtypeStruct((n, W_GROUP), jnp.float32),
        compiler_params=pltpu.CompilerParams(dimension_semantics=("parallel",),
                                             vmem_limit_bytes=VMEM_LIMIT),
        name="hy_gate",
    )(c, u, gate, d.reshape(1, W_GROUP))


def hyena_mixer(z, n_seq, conv_w, conv_b, f_w1, f_b1, f_freq, f_w2, f_b2, f_w3, log_rate, d):
    v, x1, x2 = hy_pre(z, n_seq, conv_w, conv_b)
    kf = _hyena_filter_spectra(f_w1, f_b1, f_freq, f_w2, f_b2, f_w3, log_rate)
    tables = _hyena_dft_tables()
    u = v
    for o, gate in enumerate((x1, x2)):
        u = hy_gate(hy_conv(u, n_seq, kf[o:o + 1], tables), u, gate, d[o])
    return u


NA_BAND = 8
NA_KROWS = 2 * NA_BAND
NA_KBLK = 4


def _na_bias_tables(rel_bias, rows):
    hi = lax.Precision.HIGHEST
    qc = np.arange(GRID_W)
    ws = np.clip(qc - NA_WIN_C // 2, 0, GRID_W - NA_WIN_C)
    col_ok = (qc[None, :] >= ws[:, None]) & (qc[None, :] < ws[:, None] + NA_WIN_C)
    dc = np.clip(qc[None, :] - qc[:, None] + NA_WIN_C - 1, 0, 2 * NA_WIN_C - 2)
    dc_sel = (dc[..., None] == np.arange(2 * NA_WIN_C - 1)).astype(np.float32)
    col_bias = jnp.einsum('qkc,hdc->hdqk', dc_sel, rel_bias, precision=hi)
    tabs = []
    for band in (0, 1, rows // NA_BAND - 1):
        kb = int(np.clip(NA_BAND * band - NA_WIN_R // 2, 0, rows - NA_KROWS))
        r = NA_BAND * band + np.arange(NA_BAND)
        rs = np.clip(r - NA_WIN_R // 2, 0, rows - NA_WIN_R)
        krow = kb + np.arange(NA_KROWS)
        row_ok = (krow[None, :] >= rs[:, None]) & (krow[None, :] < rs[:, None] + NA_WIN_R)
        dr = np.clip(krow[None, :] - r[:, None] + NA_WIN_R - 1, 0, 2 * NA_WIN_R - 2)
        dr_sel = (dr[..., None] == np.arange(2 * NA_WIN_R - 1)).astype(np.float32)
        bias = jnp.einsum('jwd,hdqk->hjqwk', dr_sel, col_bias, precision=hi)
        ok = row_ok[:, None, :, None] & col_ok[None, :, None, :]
        tab = jnp.where(jnp.asarray(ok)[None], bias, NEG_INF)
        tabs.append(tab.reshape(NA_HEADS, NA_BAND * GRID_W, NA_KROWS * GRID_W))
    return jnp.stack(tabs)


def _na_kernel(q_ref, k0_ref, k1_ref, k2_ref, k3_ref, v0_ref, v1_ref, v2_ref, v3_ref, qg_ref, kg_ref,
               tab_ref, o_ref):
    q = q_ref[...]
    k = jnp.concatenate([k0_ref[...], k1_ref[...], k2_ref[...], k3_ref[...]], axis=0)
    v = jnp.concatenate([v0_ref[...], v1_ref[...], v2_ref[...], v3_ref[...]], axis=0)
    for h in range(NA_HEADS):
        sl = slice(h * NA_HEAD, (h + 1) * NA_HEAD)
        qh = (_rms(q[:, sl]) * qg_ref[...] * (NA_HEAD ** -0.5)).astype(jnp.bfloat16)
        kh = (_rms(k[:, sl]) * kg_ref[...]).astype(jnp.bfloat16)
        s = lax.dot_general(qh, kh, (((1,), (1,)), ((), ())), preferred_element_type=jnp.float32)
        s = s + tab_ref[0, h]
        p = jnp.exp(s - jnp.max(s, axis=-1, keepdims=True))
        den = jnp.sum(p, axis=-1, keepdims=True)
        o = jnp.dot(p.astype(jnp.bfloat16), v[:, sl].astype(jnp.bfloat16), preferred_element_type=jnp.float32)
        o_ref[:, sl] = o / den


def na_mixer(q, k, v, n_seq, q_g, k_g, rel_bias):
    n = q.shape[0]
    rows = n // n_seq // GRID_W
    n_band = rows // NA_BAND
    tab = _na_bias_tables(rel_bias, rows)
    qtok = NA_BAND * GRID_W
    ktok = NA_KBLK * GRID_W
    kblk_per_seq = rows // NA_KBLK

    def kv_spec(j):
        def index(i, b):
            first = jnp.clip(NA_BAND // NA_KBLK * i - 1, 0, kblk_per_seq - NA_KROWS // NA_KBLK)
            return (b * kblk_per_seq + first + j, 0)
        return pl.BlockSpec((ktok, W_GROUP), index)

    def tab_index(i, b):
        return (jnp.where(i == 0, 0, jnp.where(i == n_band - 1, 2, 1)), 0, 0, 0)

    qspec = pl.BlockSpec((qtok, W_GROUP), lambda i, b: (b * n_band + i, 0))
    gspec = pl.BlockSpec((1, NA_HEAD), lambda i, b: (0, 0))
    kvs = [kv_spec(j) for j in range(NA_KROWS // NA_KBLK)]
    return pl.pallas_call(
        _na_kernel,
        grid=(n_band, n_seq),
        in_specs=[qspec] + kvs + kvs + [gspec, gspec,
                  pl.BlockSpec((1, NA_HEADS, qtok, NA_KROWS * GRID_W), tab_index)],
        out_specs=qspec,
        out_shape=jax.ShapeDtypeStruct((n, W_GROUP), jnp.float32),
        compiler_params=pltpu.CompilerParams(dimension_semantics=("arbitrary", "arbitrary"),
                                             vmem_limit_bytes=VMEM_LIMIT),
        name="na_attn",
    )(q, k, k, k, k, v, v, v, v, q_g.reshape(1, NA_HEAD), k_g.reshape(1, NA_HEAD), tab)


def _trunk(x, p):
    Bn, L, _ = x.shape
    n = Bn * L
    x = x.reshape(n, D_MODEL)
    bf = jnp.bfloat16
    for l in range(DEPTH):
        z_s5, z_hy, z_rw, z_q, z_k, z_v = in_proj(x, p['ln1_g'][l], p['w_in'][l].astype(bf))
        y_s5 = s5_mixer(z_s5, Bn, p['s5_lam_re'][l], p['s5_lam_im'][l], p['s5_log_dt'][l],
                        p['s5_b_re'][l], p['s5_b_im'][l], p['s5_c_re'][l], p['s5_c_im'][l], p['s5_d'][l],
                        p['s5_glu_w'][l], p['s5_glu_b'][l])
        y_hy = hyena_mixer(z_hy, Bn, p['hy_conv_w'][l], p['hy_conv_b'][l], p['hy_f_w1'][l],
                           p['hy_f_b1'][l], p['hy_f_freq'][l], p['hy_f_w2'][l], p['hy_f_b2'][l],
                           p['hy_f_w3'][l], p['hy_log_rate'][l], p['hy_d'][l])
        y_rw = rwkv_mixer(z_rw, Bn, p['rw_mu'][l], p['rw_w0'][l], p['rw_w2'][l],
                          p['rw_a0'][l], p['rw_a2'][l], p['rw_g2'][l], p['rw_k_k'][l], p['rw_k_a'][l],
                          p['rw_r_k'][l], p['rw_ln_w'][l], p['rw_ln_b'][l])
        y_na = na_mixer(z_q, z_k, z_v, Bn, p['na_q_g'][l], p['na_k_g'][l], p['na_rel_bias'][l])
        ys = [y_s5, y_hy, y_rw, y_na]
        x = out_mlp(x, ys, p['grp_g'][l], p['w_out'][l].astype(bf), p['ln2_g'][l], p['w_mlp1'][l].astype(bf),
                    p['w_mlp2'][l].astype(bf))
    return x.reshape(Bn, L, D_MODEL)


def kernel(x_prompt, x_sample, ln1_g, w_in, s5_lam_re, s5_lam_im, s5_log_dt, s5_b_re, s5_b_im,
           s5_c_re, s5_c_im, s5_d, s5_glu_w, s5_glu_b, hy_conv_w, hy_conv_b, hy_f_w1, hy_f_b1,
           hy_f_freq, hy_f_w2, hy_f_b2, hy_f_w3, hy_log_rate, hy_d, rw_mu, rw_w0, rw_w2, rw_a0,
           rw_a2, rw_g2, rw_k_k, rw_k_a, rw_r_k, rw_ln_w, rw_ln_b, na_q_g, na_k_g, na_rel_bias,
           grp_g, w_out, ln2_g, w_mlp1, w_mlp2):
    p = dict(ln1_g=ln1_g, w_in=w_in, s5_lam_re=s5_lam_re, s5_lam_im=s5_lam_im, s5_log_dt=s5_log_dt,
             s5_b_re=s5_b_re, s5_b_im=s5_b_im, s5_c_re=s5_c_re, s5_c_im=s5_c_im, s5_d=s5_d,
             s5_glu_w=s5_glu_w, s5_glu_b=s5_glu_b, hy_conv_w=hy_conv_w, hy_conv_b=hy_conv_b,
             hy_f_w1=hy_f_w1, hy_f_b1=hy_f_b1, hy_f_freq=hy_f_freq, hy_f_w2=hy_f_w2, hy_f_b2=hy_f_b2,
             hy_f_w3=hy_f_w3, hy_log_rate=hy_log_rate, hy_d=hy_d, rw_mu=rw_mu, rw_w0=rw_w0, rw_w2=rw_w2,
             rw_a0=rw_a0, rw_a2=rw_a2, rw_g2=rw_g2, rw_k_k=rw_k_k, rw_k_a=rw_k_a, rw_r_k=rw_r_k,
             rw_ln_w=rw_ln_w, rw_ln_b=rw_ln_b, na_q_g=na_q_g, na_k_g=na_k_g, na_rel_bias=na_rel_bias,
             grp_g=grp_g, w_out=w_out, ln2_g=ln2_g, w_mlp1=w_mlp1, w_mlp2=w_mlp2)
    nb = x_prompt.shape[0]
    y = _trunk(jnp.concatenate([x_prompt, x_sample], axis=0), p)
    return (y[:nb], y[nb:])
```

```python
import math

import jax
import jax.numpy as jnp
import numpy as np
from jax import lax
from jax.experimental import pallas as pl
from jax.experimental.pallas import tpu as pltpu

D_MODEL = 1024
DEPTH = 4
GRID_W = 64
W_GROUP = 256
N_MIXERS = 4
D_FF = 4 * D_MODEL
NORM_EPS = 1e-6

S5_CH = 16
S5_GROUPS = W_GROUP // S5_CH
S5_STATE = 64
S5_IN = W_GROUP

HY_ORDER = 2
HY_BANDS = 8
HY_IN = (HY_ORDER + 1) * W_GROUP

RW_HEAD = 64
RW_HEADS = W_GROUP // RW_HEAD
RW_DECAY_RANK = 64
RW_A_RANK = 64
RW_G_RANK = 128
RW_LN_EPS = 64e-5
RW_IN = 3 * W_GROUP + RW_DECAY_RANK + RW_A_RANK + RW_G_RANK
RW_SPLITS = (W_GROUP, 2 * W_GROUP, 3 * W_GROUP, 3 * W_GROUP + RW_DECAY_RANK,
             3 * W_GROUP + RW_DECAY_RANK + RW_A_RANK)

NA_HEAD = 64
NA_HEADS = W_GROUP // NA_HEAD
NA_WIN_R = 8
NA_WIN_C = 16
NEG_INF = -1e30

D_IN = S5_IN + HY_IN + RW_IN + 3 * W_GROUP

V7X_LANES = 128
VMEM_LIMIT = 48 * 1024 * 1024

ROW_TILE = 512
EW_ROW_TILE = 1024
FF_TILE = 1024
MLP_ROW_TILE = 1024
SCAN_T = 16
SCAN_UNROLL = 32


def _rms(x):
    return x * lax.rsqrt(jnp.mean(x * x, axis=-1, keepdims=True) + NORM_EPS)


IN_WIDTHS = (S5_IN, HY_IN, RW_IN, W_GROUP, W_GROUP, W_GROUP)


def _in_proj_kernel(x_ref, g_ref, w_ref, *o_refs):
    h = (_rms(x_ref[...]) * g_ref[...]).astype(jnp.bfloat16)
    lo = 0
    for o_ref, width in zip(o_refs, IN_WIDTHS):
        o_ref[...] = jnp.dot(h, w_ref[:, lo:lo + width], preferred_element_type=jnp.float32)
        lo += width


def in_proj(x, g, w_bf16):
    n = x.shape[0]
    return pl.pallas_call(
        _in_proj_kernel,
        grid=(n // ROW_TILE,),
        in_specs=[pl.BlockSpec((ROW_TILE, D_MODEL), lambda i: (i, 0)),
                  pl.BlockSpec((1, D_MODEL), lambda i: (0, 0)),
                  pl.BlockSpec((D_MODEL, D_IN), lambda i: (0, 0))],
        out_specs=[pl.BlockSpec((ROW_TILE, w), lambda i: (i, 0)) for w in IN_WIDTHS],
        out_shape=[jax.ShapeDtypeStruct((n, w), jnp.float32) for w in IN_WIDTHS],
        compiler_params=pltpu.CompilerParams(dimension_semantics=("parallel",),
                                             vmem_limit_bytes=VMEM_LIMIT),
        name="in_proj",
    )(x, g.reshape(1, D_MODEL), w_bf16)


def _out_mlp_kernel(x_ref, y0_ref, y1_ref, y2_ref, y3_ref, gg_ref, wo_ref, g_ref, w1_ref, w2_ref, o_ref, h_ref):
    j = pl.program_id(1)

    @pl.when(j == 0)
    def _():
        x = x_ref[...]
        for i, y_ref in enumerate((y0_ref, y1_ref, y2_ref, y3_ref)):
            n = (_rms(y_ref[...]) * gg_ref[i:i + 1, :]).astype(jnp.bfloat16)
            x = x + jnp.dot(n, wo_ref[i * W_GROUP:(i + 1) * W_GROUP, :], preferred_element_type=jnp.float32)
        h_ref[...] = (_rms(x) * g_ref[...]).astype(jnp.bfloat16)
        o_ref[...] = x

    a = jnp.dot(h_ref[...], w1_ref[...], preferred_element_type=jnp.float32)
    a = jnp.square(jnp.maximum(a, 0.0)).astype(jnp.bfloat16)
    o_ref[...] += jnp.dot(a, w2_ref[...], preferred_element_type=jnp.float32)


def out_mlp(x, ys, grp_g, wo_bf16, g, w1_bf16, w2_bf16):
    n = x.shape[0]
    row = lambda w: pl.BlockSpec((MLP_ROW_TILE, w), lambda i, j: (i, 0))
    return pl.pallas_call(
        _out_mlp_kernel,
        grid=(n // MLP_ROW_TILE, D_FF // FF_TILE),
        in_specs=[row(D_MODEL)] + [row(W_GROUP)] * N_MIXERS
                 + [pl.BlockSpec((N_MIXERS, W_GROUP), lambda i, j: (0, 0)),
                    pl.BlockSpec((D_MODEL, D_MODEL), lambda i, j: (0, 0)),
                    pl.BlockSpec((1, D_MODEL), lambda i, j: (0, 0)),
                    pl.BlockSpec((D_MODEL, FF_TILE), lambda i, j: (0, j)),
                    pl.BlockSpec((FF_TILE, D_MODEL), lambda i, j: (j, 0))],
        out_specs=row(D_MODEL),
        out_shape=jax.ShapeDtypeStruct((n, D_MODEL), jnp.float32),
        scratch_shapes=[pltpu.VMEM((MLP_ROW_TILE, D_MODEL), jnp.bfloat16)],
        compiler_params=pltpu.CompilerParams(dimension_semantics=("parallel", "arbitrary"),
                                             vmem_limit_bytes=VMEM_LIMIT),
        name="out_mlp",
    )(x, *ys, grp_g, wo_bf16, g.reshape(1, D_MODEL), w1_bf16, w2_bf16)


def _head_sum(x):
    lane = lax.broadcasted_iota(jnp.int32, (W_GROUP, W_GROUP), 0) // RW_HEAD
    col = lax.broadcasted_iota(jnp.int32, (W_GROUP, W_GROUP), 1) // RW_HEAD
    ones = (lane == col).astype(jnp.float32)
    return jnp.dot(x, ones, precision=lax.Precision.HIGHEST, preferred_element_type=jnp.float32)


def _softplus(x):
    return jnp.maximum(x, 0.0) + jnp.log(1.0 + jnp.exp(-jnp.abs(x)))


def _shifted(z, prev_row, next_row):
    t = z.shape[0]
    row = lax.broadcasted_iota(jnp.int32, z.shape, 0)
    zp = jnp.where(row == 0, prev_row, pltpu.roll(z, 1, 0))
    zn = jnp.where(row == t - 1, next_row, pltpu.roll(z, t - 1, 0))
    return zp, zn


def _store_heads(o_ref, x):
    for h in range(RW_HEADS):
        o_ref[h] = x[:, h * RW_HEAD:(h + 1) * RW_HEAD]


def _rwkv_prep_kernel(z_ref, zp_ref, zn_ref, mu_ref, w0_ref, w2_ref, a0_ref, a2_ref, g2_ref, kk_ref, ka_ref,
                      rk_ref, nkk_ref, r_ref, v_ref, dec0_ref, dec1_ref, kd0_ref, kd1_ref, b0_ref, b1_ref,
                      g_ref, bonus_ref):
    z = z_ref[...]
    zp, zn = _shifted(z, zp_ref[0], zn_ref[0])
    z = z + mu_ref[...] * (0.5 * (zp + zn) - z)
    r = z[:, 0:W_GROUP]
    k = z[:, W_GROUP:2 * W_GROUP]
    v = z[:, 2 * W_GROUP:3 * W_GROUP]
    wd = z[:, RW_SPLITS[2]:RW_SPLITS[3]]
    ad = z[:, RW_SPLITS[3]:RW_SPLITS[4]]
    gd = z[:, RW_SPLITS[4]:RW_IN]
    bf = jnp.bfloat16
    g_ref[...] = jnp.dot(jax.nn.sigmoid(gd).astype(bf), g2_ref[...], preferred_element_type=jnp.float32)
    kk = k * kk_ref[...]
    kk = kk / jnp.maximum(jnp.sqrt(_head_sum(kk * kk)), 1e-12)
    _store_heads(nkk_ref, -kk)
    _store_heads(r_ref, r)
    _store_heads(v_ref, v)
    bonus_ref[...] = _head_sum(r * k * rk_ref[...]) * v
    tw = jnp.tanh(wd).astype(bf)
    adb = ad.astype(bf)
    for d, (dec_ref, kd_ref, b_ref) in enumerate(((dec0_ref, kd0_ref, b0_ref), (dec1_ref, kd1_ref, b1_ref))):
        w = w0_ref[d:d + 1, :] + jnp.dot(tw, w2_ref[d], preferred_element_type=jnp.float32)
        w = -_softplus(-w) - 0.5
        _store_heads(dec_ref, jnp.exp(-jnp.exp(w)))
        a = jax.nn.sigmoid(a0_ref[d:d + 1, :] + jnp.dot(adb, a2_ref[d], preferred_element_type=jnp.float32))
        _store_heads(kd_ref, k * (1.0 + (a - 1.0) * ka_ref[...]))
        _store_heads(b_ref, kk * a)


def _halo_rows(z, n_seq, tile):
    n, c = z.shape
    zt = z.reshape(n_seq, n // n_seq // tile, tile, c)
    zero = jnp.zeros((n_seq, 1, c), z.dtype)
    prev = jnp.concatenate([zero, zt[:, :-1, -1]], axis=1).reshape(n // tile, 1, c)
    nxt = jnp.concatenate([zt[:, 1:, 0], zero], axis=1).reshape(n // tile, 1, c)
    return prev, nxt


def rwkv_prep(z, n_seq, mu, w0, w2, a0, a2, g2, k_k, k_a, r_k):
    n = z.shape[0]
    prev, nxt = _halo_rows(z, n_seq, ROW_TILE)
    bf = jnp.bfloat16
    full = lambda *s: pl.BlockSpec(s, lambda i: (0,) * len(s))
    row = pl.BlockSpec((ROW_TILE, W_GROUP), lambda i: (i, 0))
    heads = pl.BlockSpec((RW_HEADS, ROW_TILE, RW_HEAD), lambda i: (0, i, 0))
    halo = pl.BlockSpec((1, 1, RW_IN), lambda i: (i, 0, 0))
    vec = lambda x: x.reshape(1, -1)
    return pl.pallas_call(
        _rwkv_prep_kernel,
        grid=(n // ROW_TILE,),
        in_specs=[pl.BlockSpec((ROW_TILE, RW_IN), lambda i: (i, 0)), halo, halo, full(1, RW_IN),
                  full(2, W_GROUP), full(2, RW_DECAY_RANK, W_GROUP), full(2, W_GROUP),
                  full(2, RW_A_RANK, W_GROUP), full(RW_G_RANK, W_GROUP), full(1, W_GROUP), full(1, W_GROUP),
                  full(1, W_GROUP)],
        out_specs=[heads] * 9 + [row] * 2,
        out_shape=[jax.ShapeDtypeStruct((RW_HEADS, n, RW_HEAD), jnp.float32)] * 9
                  + [jax.ShapeDtypeStruct((n, W_GROUP), jnp.float32)] * 2,
        compiler_params=pltpu.CompilerParams(dimension_semantics=("parallel",),
                                             vmem_limit_bytes=VMEM_LIMIT),
        name="rwkv_prep",
    )(z, prev, nxt, vec(mu), w0, w2.astype(bf), a0, a2.astype(bf), g2.astype(bf), vec(k_k), vec(k_a), vec(r_k))


def _rwkv_scan_kernel(af_ref, wf_ref, bf_ref, kf_ref, rf_ref, vf_ref, ab_ref, wb_ref, bb_ref, kb_ref, rb_ref,
                      vb_ref, yf_ref, yb_ref, s_ref):
    @pl.when(pl.program_id(0) == 0)
    def _():
        s_ref[...] = jnp.zeros_like(s_ref)

    dirs = ((af_ref, wf_ref, bf_ref, kf_ref, rf_ref, vf_ref, yf_ref),
            (ab_ref, wb_ref, bb_ref, kb_ref, rb_ref, vb_ref, yb_ref))

    def step(i, carry):
        for j, (a_ref, w_ref, b_ref, k_ref, r_ref, v_ref, y_ref) in enumerate(dirs):
            t = i if j == 0 else SCAN_T - 1 - i
            vt = v_ref[t]

            def reduce_a(k, sa):
                return sa + s_ref[j, k] * a_ref[t, pl.ds(k, 1), :]

            sa = lax.fori_loop(0, RW_HEAD, reduce_a, jnp.zeros_like(vt), unroll=SCAN_UNROLL)

            def update(k, y):
                s = (s_ref[j, k] * w_ref[t, pl.ds(k, 1), :] + sa * b_ref[t, pl.ds(k, 1), :]
                     + vt * k_ref[t, pl.ds(k, 1), :])
                s_ref[j, k] = s
                return y + s * r_ref[t, pl.ds(k, 1), :]

            y_ref[t] = lax.fori_loop(0, RW_HEAD, update, jnp.zeros_like(vt), unroll=SCAN_UNROLL)
        return carry

    lax.fori_loop(0, SCAN_T, step, 0)


def rwkv_scan(a, r, v, w_f, b_f, k_f, w_b, b_b, k_b):
    L, _, nc = a.shape
    nblk = L // SCAN_T
    fwd = pl.BlockSpec((SCAN_T, RW_HEAD, nc), lambda i: (i, 0, 0))
    bwd = pl.BlockSpec((SCAN_T, RW_HEAD, nc), lambda i: (nblk - 1 - i, 0, 0))
    out = jax.ShapeDtypeStruct((L, RW_HEAD, nc), jnp.float32)
    return pl.pallas_call(
        _rwkv_scan_kernel,
        grid=(nblk,),
        in_specs=[fwd] * 6 + [bwd] * 6,
        out_specs=[fwd, bwd],
        out_shape=[out, out],
        scratch_shapes=[pltpu.VMEM((2, RW_HEAD, RW_HEAD, nc), jnp.float32)],
        compiler_params=pltpu.CompilerParams(dimension_semantics=("arbitrary",),
                                             vmem_limit_bytes=VMEM_LIMIT),
        name="rwkv_scan",
    )(a, w_f, b_f, k_f, r, v, a, w_b, b_b, k_b, r, v)


def _rwkv_post_kernel(yf_ref, yb_ref, bonus_ref, g_ref, lw_ref, lb_ref, o_ref):
    y = jnp.concatenate([yf_ref[h] + yb_ref[h] for h in range(RW_HEADS)], axis=-1)
    mean = _head_sum(y) * (1.0 / RW_HEAD)
    c = y - mean
    var = _head_sum(c * c) * (1.0 / RW_HEAD)
    y = c * lax.rsqrt(var + RW_LN_EPS) * lw_ref[...] + lb_ref[...]
    o_ref[...] = (y + bonus_ref[...]) * g_ref[...]


def rwkv_post(y_f, y_b, bonus, g, ln_w, ln_b):
    n = bonus.shape[0]
    row = pl.BlockSpec((EW_ROW_TILE, W_GROUP), lambda i: (i, 0))
    heads = pl.BlockSpec((RW_HEADS, EW_ROW_TILE, RW_HEAD), lambda i: (0, i, 0))
    vec = pl.BlockSpec((1, W_GROUP), lambda i: (0, 0))
    return pl.pallas_call(
        _rwkv_post_kernel,
        grid=(n // EW_ROW_TILE,),
        in_specs=[heads, heads, row, row, vec, vec],
        out_specs=row,
        out_shape=jax.ShapeDtypeStruct((n, W_GROUP), jnp.float32),
        compiler_params=pltpu.CompilerParams(dimension_semantics=("parallel",),
                                             vmem_limit_bytes=VMEM_LIMIT),
        name="rwkv_post",
    )(y_f, y_b, bonus, g, ln_w.reshape(1, W_GROUP), ln_b.reshape(1, W_GROUP))


def rwkv_mixer(z, n_seq, mu, w0, w2, a0, a2, g2, k_k, k_a, r_k, ln_w, ln_b):
    n = z.shape[0]
    L = n // n_seq
    n_chain = n_seq * RW_HEADS
    assert n_chain <= V7X_LANES
    nkk, r, v, dec0, dec1, kd0, kd1, b0, b1, g, bonus = rwkv_prep(z, n_seq, mu, w0, w2, a0, a2, g2, k_k, k_a, r_k)

    def chains(x):
        return x.reshape(RW_HEADS, n_seq, L, RW_HEAD).transpose(2, 3, 0, 1).reshape(L, RW_HEAD, n_chain)

    def tokens(y):
        return y.reshape(L, RW_HEAD, RW_HEADS, n_seq).transpose(2, 3, 0, 1).reshape(RW_HEADS, n, RW_HEAD)

    y_f, y_b = rwkv_scan(chains(nkk), chains(r), chains(v), chains(dec0), chains(b0), chains(kd0),
                         chains(dec1), chains(b1), chains(kd1))
    return rwkv_post(tokens(y_f), tokens(y_b), bonus, g, ln_w, ln_b)


S5_NSTATE = S5_GROUPS * S5_STATE
S5_T = 1024


def _cmul(ar, ai, br, bi):
    return ar * br - ai * bi, ar * bi + ai * br


def _s5_operators(lam_re, lam_im, log_dt, b_re, b_im, c_re, c_im):
    dt = jnp.exp(log_dt)[..., None]
    mag = jnp.exp(lam_re * dt)
    ab_re = mag * jnp.cos(lam_im * dt)
    ab_im = mag * jnp.sin(lam_im * dt)
    den = lam_re * lam_re + lam_im * lam_im
    n_re = ab_re - 1.0
    f_re = (n_re * lam_re + ab_im * lam_im) / den
    f_im = (ab_im * lam_re - n_re * lam_im) / den
    bb_re, bb_im = _cmul(f_re[..., None], f_im[..., None], b_re, b_im)
    eye = jnp.eye(S5_GROUPS, dtype=jnp.float32)

    def in_map(bb):
        return jnp.einsum('gh,dhnc->dgchn', eye, bb).reshape(2, W_GROUP, S5_NSTATE)

    def out_map(c):
        return jnp.einsum('hg,dgcn->dhngc', eye, c).reshape(2, S5_NSTATE, W_GROUP)

    bmat = jnp.concatenate([in_map(bb_re), in_map(bb_im)], axis=-1)
    cmat = jnp.concatenate([out_map(c_re), -out_map(c_im)], axis=1)
    lam = jnp.stack([ab_re.reshape(2, S5_NSTATE), ab_im.reshape(2, S5_NSTATE)], axis=1)
    return lam, bmat.astype(jnp.bfloat16), cmat.astype(jnp.bfloat16)


def _s5_scan_kernel(u_ref, lam_ref, bmat_ref, cmat_ref, y_ref, st_ref, carry_ref):
    d = pl.program_id(1)

    @pl.when(pl.program_id(2) == 0)
    def _():
        carry_ref[...] = jnp.zeros_like(carry_ref)

    st_ref[...] = jnp.dot(u_ref[...].astype(jnp.bfloat16), bmat_ref[0], preferred_element_type=jnp.float32)
    lam_r = lam_ref[0, 0:1, :]
    lam_i = lam_ref[0, 1:2, :]
    re = slice(0, S5_NSTATE)
    im = slice(S5_NSTATE, 2 * S5_NSTATE)

    def step(i, carry):
        sr, si = carry
        t = i + d * (S5_T - 1 - 2 * i)
        nr = lam_r * sr - lam_i * si + st_ref[pl.ds(t, 1), re]
        ni = lam_r * si + lam_i * sr + st_ref[pl.ds(t, 1), im]
        st_ref[pl.ds(t, 1), re] = nr
        st_ref[pl.ds(t, 1), im] = ni
        return nr, ni

    sr, si = lax.fori_loop(0, S5_T, step, (carry_ref[0:1, re], carry_ref[0:1, im]), unroll=4)
    carry_ref[0:1, re] = sr
    carry_ref[0:1, im] = si
    y_ref[0] = jnp.dot(st_ref[...].astype(jnp.bfloat16), cmat_ref[0], preferred_element_type=jnp.float32)


def s5_scan(z, lam, bmat, cmat, n_seq):
    n = z.shape[0]
    nch = n // n_seq // S5_T

    def row_block(b, d, c):
        return b * nch + c + d * (nch - 1 - 2 * c)

    return pl.pallas_call(
        _s5_scan_kernel,
        grid=(n_seq, 2, nch),
        in_specs=[pl.BlockSpec((S5_T, W_GROUP), lambda b, d, c: (row_block(b, d, c), 0)),
                  pl.BlockSpec((1, 2, S5_NSTATE), lambda b, d, c: (d, 0, 0)),
                  pl.BlockSpec((1, W_GROUP, 2 * S5_NSTATE), lambda b, d, c: (d, 0, 0)),
                  pl.BlockSpec((1, 2 * S5_NSTATE, W_GROUP), lambda b, d, c: (d, 0, 0))],
        out_specs=pl.BlockSpec((1, S5_T, W_GROUP), lambda b, d, c: (d, row_block(b, d, c), 0)),
        out_shape=jax.ShapeDtypeStruct((2, n, W_GROUP), jnp.float32),
        scratch_shapes=[pltpu.VMEM((S5_T, 2 * S5_NSTATE), jnp.float32),
                        pltpu.VMEM((8, 2 * S5_NSTATE), jnp.float32)],
        compiler_params=pltpu.CompilerParams(dimension_semantics=("arbitrary", "arbitrary", "arbitrary"),
                                             vmem_limit_bytes=VMEM_LIMIT),
        name="s5_scan",
    )(z, lam, bmat, cmat)


def _s5_finish_kernel(y_ref, z_ref, d_ref, w_ref, b_ref, o_ref):
    y = y_ref[0] + y_ref[1] + d_ref[...] * z_ref[...]
    g = jax.nn.gelu(y)
    gate = jnp.dot(g.astype(jnp.bfloat16), w_ref[...], preferred_element_type=jnp.float32) + b_ref[...]
    o_ref[...] = g * jax.nn.sigmoid(gate)


def s5_finish(y2, z, d, glu_w_bf16, glu_b):
    n = z.shape[0]
    vec = pl.BlockSpec((1, W_GROUP), lambda i: (0, 0))
    return pl.pallas_call(
        _s5_finish_kernel,
        grid=(n // EW_ROW_TILE,),
        in_specs=[pl.BlockSpec((2, EW_ROW_TILE, W_GROUP), lambda i: (0, i, 0)),
                  pl.BlockSpec((EW_ROW_TILE, W_GROUP), lambda i: (i, 0)),
                  vec, pl.BlockSpec((W_GROUP, W_GROUP), lambda i: (0, 0)), vec],
        out_specs=pl.BlockSpec((EW_ROW_TILE, W_GROUP), lambda i: (i, 0)),
        out_shape=jax.ShapeDtypeStruct((n, W_GROUP), jnp.float32),
        compiler_params=pltpu.CompilerParams(dimension_semantics=("parallel",),
                                             vmem_limit_bytes=VMEM_LIMIT),
        name="s5_finish",
    )(y2, z, d.reshape(1, W_GROUP), glu_w_bf16, glu_b.reshape(1, W_GROUP))


def s5_mixer(z, n_seq, lam_re, lam_im, log_dt, b_re, b_im, c_re, c_im, d, glu_w, glu_b):
    lam, bmat, cmat = _s5_operators(lam_re, lam_im, log_dt, b_re, b_im, c_re, c_im)
    y2 = s5_scan(z, lam, bmat, cmat, n_seq)
    return s5_finish(y2, z, d, glu_w.astype(jnp.bfloat16), glu_b)


HY_L = 4096
HY_N = 2 * HY_L
HY_N1 = 64
HY_N2 = 128
HY_K1 = HY_N1 // 2 + 1
HY_KB = 11
V7X_SUBLANES = 8
HY_KP = -(-2 * HY_KB // V7X_SUBLANES) * V7X_SUBLANES
HY_C = V7X_LANES


def _hyena_dft_tables():
    n1 = np.arange(HY_N1 // 2)
    k1 = np.arange(HY_K1)
    ang1 = 2.0 * np.pi * np.outer(k1, n1) / HY_N1
    weight = np.where((k1 == 0) | (k1 == HY_N1 // 2), 1.0, 2.0) / HY_N
    nblk = HY_K1 // HY_KB
    fa = np.zeros((nblk, HY_KP, HY_N1 // 2), np.float32)
    fi = np.zeros((nblk, HY_N1 // 2, HY_KP), np.float32)
    for b in range(nblk):
        sl = slice(b * HY_KB, (b + 1) * HY_KB)
        fa[b, :HY_KB] = np.cos(ang1[sl])
        fa[b, HY_KB:2 * HY_KB] = -np.sin(ang1[sl])
        fi[b, :, :HY_KB] = (np.cos(ang1[sl]) * weight[sl, None]).T
        fi[b, :, HY_KB:2 * HY_KB] = (-np.sin(ang1[sl]) * weight[sl, None]).T
    n2 = np.arange(HY_N2)
    k = k1[:, None] + HY_N1 * np.arange(HY_N2)[None, :]
    ang = 2.0 * np.pi * (k[:, :, None] * n2[None, None, :] % HY_N) / HY_N
    c, s = np.cos(ang), np.sin(ang)
    g = np.concatenate([np.concatenate([c, s], axis=2), np.concatenate([-s, c], axis=2)], axis=1)
    gi = np.transpose(g, (0, 2, 1))
    bf = jnp.bfloat16
    return (jnp.asarray(fa, bf), jnp.asarray(fi, bf), jnp.asarray(g, bf), jnp.asarray(gi, bf))


def _hyena_filter_spectra(w1, b1, freq, w2, b2, w3, log_rate):
    L = HY_L
    t = jnp.arange(L, dtype=jnp.float32) / L
    ang = 2.0 * math.pi * t[:, None] * jnp.arange(1, HY_BANDS + 1, dtype=jnp.float32)
    feats = jnp.concatenate([t[:, None], jnp.sin(ang), jnp.cos(ang)], axis=-1)
    h = jnp.sin(freq[0] * (feats @ w1 + b1))
    h = jnp.sin(freq[1] * (h @ w2 + b2))
    h = (h @ w3).reshape(L, 2, HY_ORDER, W_GROUP)
    h = h * jnp.exp(-jnp.exp(log_rate)[None] * t[:, None, None, None])
    fwd, bwd = h[:, 0], h[:, 1]
    k = jnp.concatenate([fwd, jnp.zeros_like(fwd[:1]), bwd[:0:-1]], axis=0)
    k = k / jnp.sum(jnp.abs(k), axis=0, keepdims=True)
    kf = jnp.fft.fft(k, axis=0).reshape(HY_N2, HY_N1, HY_ORDER, W_GROUP)[:, :HY_K1]
    kf = jnp.transpose(kf, (2, 1, 0, 3))
    return jnp.stack([jnp.real(kf), jnp.imag(kf)], axis=2).astype(jnp.float32)


def _hy_pre_kernel(z_ref, zp_ref, zn_ref, w_ref, b_ref, v_ref, x1_ref, x2_ref):
    z = z_ref[...]
    zp, zn = _shifted(z, zp_ref[0], zn_ref[0])
    z = w_ref[0:1, :] * zp + w_ref[1:2, :] * z + w_ref[2:3, :] * zn + b_ref[...]
    v_ref[...] = z[:, 0:W_GROUP]
    x1_ref[...] = z[:, W_GROUP:2 * W_GROUP]
    x2_ref[...] = z[:, 2 * W_GROUP:3 * W_GROUP]


def hy_pre(z, n_seq, conv_w, conv_b):
    n = z.shape[0]
    prev, nxt = _halo_rows(z, n_seq, EW_ROW_TILE)
    row = pl.BlockSpec((EW_ROW_TILE, W_GROUP), lambda i: (i, 0))
    halo = pl.BlockSpec((1, 1, HY_IN), lambda i: (i, 0, 0))
    return pl.pallas_call(
        _hy_pre_kernel,
        grid=(n // EW_ROW_TILE,),
        in_specs=[pl.BlockSpec((EW_ROW_TILE, HY_IN), lambda i: (i, 0)), halo, halo,
                  pl.BlockSpec((3, HY_IN), lambda i: (0, 0)), pl.BlockSpec((1, HY_IN), lambda i: (0, 0))],
        out_specs=[row] * 3,
        out_shape=[jax.ShapeDtypeStruct((n, W_GROUP), jnp.float32)] * 3,
        compiler_params=pltpu.CompilerParams(dimension_semantics=("parallel",),
                                             vmem_limit_bytes=VMEM_LIMIT),
        name="hy_pre",
    )(z, prev, nxt, conv_w, conv_b.reshape(1, HY_IN))


def _hy_conv_kernel(u_ref, fa_ref, fi_ref, g_ref, gi_ref, kf_ref, o_ref, as_ref, bs_ref):
    kb = pl.program_id(2)
    nblk = HY_K1 // HY_KB
    bf = jnp.bfloat16

    @pl.when(kb == 0)
    def _():
        fa = fa_ref[...].reshape(nblk * HY_KP, HY_N1 // 2)
        for n2 in range(HY_N2):
            rows = u_ref[pl.ds(n2, HY_N1 // 2, stride=HY_N2), :].astype(bf)
            a_all = jnp.dot(fa, rows, preferred_element_type=jnp.float32)
            for b in range(nblk):
                as_ref[b, n2] = a_all[b * HY_KP:(b + 1) * HY_KP]

    bs_ref[kb, :, 2 * HY_KB:, :] = jnp.zeros((HY_N2, HY_KP - 2 * HY_KB, HY_C), jnp.float32)
    a_blk = as_ref.at[kb]
    b_blk = bs_ref.at[kb]
    for j in range(HY_KB):
        a = jnp.concatenate([a_blk[:, j, :], a_blk[:, HY_KB + j, :]], axis=0).astype(bf)
        x = jnp.dot(g_ref[j], a, preferred_element_type=jnp.float32)
        xr, xi = x[:HY_N2], x[HY_N2:]
        kr, ki = kf_ref[0, j, 0], kf_ref[0, j, 1]
        y = jnp.concatenate([xr * kr - xi * ki, xr * ki + xi * kr], axis=0).astype(bf)
        b = jnp.dot(gi_ref[j], y, preferred_element_type=jnp.float32)
        b_blk[:, j, :] = b[:HY_N2]
        b_blk[:, HY_KB + j, :] = b[HY_N2:]

    @pl.when(kb == nblk - 1)
    def _():
        for n2 in range(HY_N2):
            part = jnp.dot(fi_ref[0], bs_ref[0, n2].astype(bf), preferred_element_type=jnp.float32)
            for b in range(1, nblk):
                part = part + jnp.dot(fi_ref[b], bs_ref[b, n2].astype(bf), preferred_element_type=jnp.float32)
            o_ref[pl.ds(n2, HY_N1 // 2, stride=HY_N2), :] = part


def hy_conv(u, n_seq, kf, tables):
    n = u.shape[0]
    assert n // n_seq == HY_L
    fa, fi, g, gi = tables
    nblk = HY_K1 // HY_KB
    seq = pl.BlockSpec((HY_L, HY_C), lambda b, c, k: (b, c))
    return pl.pallas_call(
        _hy_conv_kernel,
        grid=(n_seq, W_GROUP // HY_C, nblk),
        in_specs=[seq,
                  pl.BlockSpec((nblk, HY_KP, HY_N1 // 2), lambda b, c, k: (0, 0, 0)),
                  pl.BlockSpec((nblk, HY_N1 // 2, HY_KP), lambda b, c, k: (0, 0, 0)),
                  pl.BlockSpec((HY_KB, 2 * HY_N2, 2 * HY_N2), lambda b, c, k: (k, 0, 0)),
                  pl.BlockSpec((HY_KB, 2 * HY_N2, 2 * HY_N2), lambda b, c, k: (k, 0, 0)),
                  pl.BlockSpec((1, HY_KB, 2, HY_N2, HY_C), lambda b, c, k: (0, k, 0, 0, c))],
        out_specs=seq,
        out_shape=jax.ShapeDtypeStruct((n, W_GROUP), jnp.float32),
        scratch_shapes=[pltpu.VMEM((nblk, HY_N2, HY_KP, HY_C), jnp.float32),
                        pltpu.VMEM((nblk, HY_N2, HY_KP, HY_C), jnp.float32)],
        compiler_params=pltpu.CompilerParams(dimension_semantics=("arbitrary", "arbitrary", "arbitrary"),
                                             vmem_limit_bytes=VMEM_LIMIT),
        name="hy_conv",
    )(u, fa, fi, g, gi, kf)


def _hy_gate_kernel(c_ref, u_ref, x_ref, d_ref, o_ref):
    o_ref[...] = x_ref[...] * (c_ref[...] + d_ref[...] * u_ref[...])


def hy_gate(c, u, gate, d):
    n = u.shape[0]
    row = pl.BlockSpec((EW_ROW_TILE, W_GROUP), lambda i: (i, 0))
    return pl.pallas_call(
        _hy_gate_kernel,
        grid=(n // EW_ROW_TILE,),
        in_specs=[row, row, row, pl.BlockSpec((1, W_GROUP), lambda i: (0, 0))],
        out_specs=row,
        out_shape=jax.ShapeDtypeStruct((n, W_GROUP), jnp.float32),
        compiler_params=pltpu.CompilerParams(dimension_semantics=("parallel",),
                                             vmem_limit_bytes=VMEM_LIMIT),
        name="hy_gate",
    )(c, u, gate, d.reshape(1, W_GROUP))


def hyena_mixer(z, n_seq, conv_w, conv_b, f_w1, f_b1, f_freq, f_w2, f_b2, f_w3, log_rate, d):
    v, x1, x2 = hy_pre(z, n_seq, conv_w, conv_b)
    kf = _hyena_filter_spectra(f_w1, f_b1, f_freq, f_w2, f_b2, f_w3, log_rate)
    tables = _hyena_dft_tables()
    u = v
    for o, gate in enumerate((x1, x2)):
        u = hy_gate(hy_conv(u, n_seq, kf[o:o + 1], tables), u, gate, d[o])
    return u


NA_BAND = 8
NA_KROWS = 2 * NA_BAND
NA_KBLK = 4


def _na_bias_tables(rel_bias, rows):
    hi = lax.Precision.HIGHEST
    qc = np.arange(GRID_W)
    ws = np.clip(qc - NA_WIN_C // 2, 0, GRID_W - NA_WIN_C)
    col_ok = (qc[None, :] >= ws[:, None]) & (qc[None, :] < ws[:, None] + NA_WIN_C)
    dc = np.clip(qc[None, :] - qc[:, None] + NA_WIN_C - 1, 0, 2 * NA_WIN_C - 2)
    dc_sel = (dc[..., None] == np.arange(2 * NA_WIN_C - 1)).astype(np.float32)
    col_bias = jnp.einsum('qkc,hdc->hdqk', dc_sel, rel_bias, precision=hi)
    tabs = []
    for band in (0, 1, rows // NA_BAND - 1):
        kb = int(np.clip(NA_BAND * band - NA_WIN_R // 2, 0, rows - NA_KROWS))
        r = NA_BAND * band + np.arange(NA_BAND)
        rs = np.clip(r - NA_WIN_R // 2, 0, rows - NA_WIN_R)
        krow = kb + np.arange(NA_KROWS)
        row_ok = (krow[None, :] >= rs[:, None]) & (krow[None, :] < rs[:, None] + NA_WIN_R)
        dr = np.clip(krow[None, :] - r[:, None] + NA_WIN_R - 1, 0, 2 * NA_WIN_R - 2)
        dr_sel = (dr[..., None] == np.arange(2 * NA_WIN_R - 1)).astype(np.float32)
        bias = jnp.einsum('jwd,hdqk->hjqwk', dr_sel, col_bias, precision=hi)
        ok = row_ok[:, None, :, None] & col_ok[None, :, None, :]
        tab = jnp.where(jnp.asarray(ok)[None], bias, NEG_INF)
        tabs.append(tab.reshape(NA_HEADS, NA_BAND * GRID_W, NA_KROWS * GRID_W))
    return jnp.stack(tabs)


def _na_kernel(q_ref, k0_ref, k1_ref, k2_ref, k3_ref, v0_ref, v1_ref, v2_ref, v3_ref, qg_ref, kg_ref,
               tab_ref, o_ref):
    q = q_ref[...]
    k = jnp.concatenate([k0_ref[...], k1_ref[...], k2_ref[...], k3_ref[...]], axis=0)
    v = jnp.concatenate([v0_ref[...], v1_ref[...], v2_ref[...], v3_ref[...]], axis=0)
    for h in range(NA_HEADS):
        sl = slice(h * NA_HEAD, (h + 1) * NA_HEAD)
        qh = (_rms(q[:, sl]) * qg_ref[...] * (NA_HEAD ** -0.5)).astype(jnp.bfloat16)
        kh = (_rms(k[:, sl]) * kg_ref[...]).astype(jnp.bfloat16)
        s = lax.dot_general(qh, kh, (((1,), (1,)), ((), ())), preferred_element_type=jnp.float32)
        s = s + tab_ref[0, h]
        p = jnp.exp(s - jnp.max(s, axis=-1, keepdims=True))
        den = jnp.sum(p, axis=-1, keepdims=True)
        o = jnp.dot(p.astype(jnp.bfloat16), v[:, sl].astype(jnp.bfloat16), preferred_element_type=jnp.float32)
        o_ref[:, sl] = o / den


def na_mixer(q, k, v, n_seq, q_g, k_g, rel_bias):
    n = q.shape[0]
    rows = n // n_seq // GRID_W
    n_band = rows // NA_BAND
    tab = _na_bias_tables(rel_bias, rows)
    qtok = NA_BAND * GRID_W
    ktok = NA_KBLK * GRID_W
    kblk_per_seq = rows // NA_KBLK

    def kv_spec(j):
        def index(i, b):
            first = jnp.clip(NA_BAND // NA_KBLK * i - 1, 0, kblk_per_seq - NA_KROWS // NA_KBLK)
            return (b * kblk_per_seq + first + j, 0)
        return pl.BlockSpec((ktok, W_GROUP), index)

    def tab_index(i, b):
        return (jnp.where(i == 0, 0, jnp.where(i == n_band - 1, 2, 1)), 0, 0, 0)

    qspec = pl.BlockSpec((qtok, W_GROUP), lambda i, b: (b * n_band + i, 0))
    gspec = pl.BlockSpec((1, NA_HEAD), lambda i, b: (0, 0))
    kvs = [kv_spec(j) for j in range(NA_KROWS // NA_KBLK)]
    return pl.pallas_call(
        _na_kernel,
        grid=(n_band, n_seq),
        in_specs=[qspec] + kvs + kvs + [gspec, gspec,
                  pl.BlockSpec((1, NA_HEADS, qtok, NA_KROWS * GRID_W), tab_index)],
        out_specs=qspec,
        out_shape=jax.ShapeDtypeStruct((n, W_GROUP), jnp.float32),
        compiler_params=pltpu.CompilerParams(dimension_semantics=("arbitrary", "arbitrary"),
                                             vmem_limit_bytes=VMEM_LIMIT),
        name="na_attn",
    )(q, k, k, k, k, v, v, v, v, q_g.reshape(1, NA_HEAD), k_g.reshape(1, NA_HEAD), tab)


def _trunk(x, p):
    Bn, L, _ = x.shape
    n = Bn * L
    x = x.reshape(n, D_MODEL)
    bf = jnp.bfloat16
    for l in range(DEPTH):
        z_s5, z_hy, z_rw, z_q, z_k, z_v = in_proj(x, p['ln1_g'][l], p['w_in'][l].astype(bf))
        y_s5 = s5_mixer(z_s5, Bn, p['s5_lam_re'][l], p['s5_lam_im'][l], p['s5_log_dt'][l],
                        p['s5_b_re'][l], p['s5_b_im'][l], p['s5_c_re'][l], p['s5_c_im'][l], p['s5_d'][l],
                        p['s5_glu_w'][l], p['s5_glu_b'][l])
        y_hy = hyena_mixer(z_hy, Bn, p['hy_conv_w'][l], p['hy_conv_b'][l], p['hy_f_w1'][l],
                           p['hy_f_b1'][l], p['hy_f_freq'][l], p['hy_f_w2'][l], p['hy_f_b2'][l],
                           p['hy_f_w3'][l], p['hy_log_rate'][l], p['hy_d'][l])
        y_rw = rwkv_mixer(z_rw, Bn, p['rw_mu'][l], p['rw_w0'][l], p['rw_w2'][l],
                          p['rw_a0'][l], p['rw_a2'][l], p['rw_g2'][l], p['rw_k_k'][l], p['rw_k_a'][l],
                          p['rw_r_k'][l], p['rw_ln_w'][l], p['rw_ln_b'][l])
        y_na = na_mixer(z_q, z_k, z_v, Bn, p['na_q_g'][l], p['na_k_g'][l], p['na_rel_bias'][l])
        ys = [y_s5, y_hy, y_rw, y_na]
        x = out_mlp(x, ys, p['grp_g'][l], p['w_out'][l].astype(bf), p['ln2_g'][l], p['w_mlp1'][l].astype(bf),
                    p['w_mlp2'][l].astype(bf))
    return x.reshape(Bn, L, D_MODEL)


def kernel(x_prompt, x_sample, ln1_g, w_in, s5_lam_re, s5_lam_im, s5_log_dt, s5_b_re, s5_b_im,
           s5_c_re, s5_c_im, s5_d, s5_glu_w, s5_glu_b, hy_conv_w, hy_conv_b, hy_f_w1, hy_f_b1,
           hy_f_freq, hy_f_w2, hy_f_b2, hy_f_w3, hy_log_rate, hy_d, rw_mu, rw_w0, rw_w2, rw_a0,
           rw_a2, rw_g2, rw_k_k, rw_k_a, rw_r_k, rw_ln_w, rw_ln_b, na_q_g, na_k_g, na_rel_bias,
           grp_g, w_out, ln2_g, w_mlp1, w_mlp2):
    p = dict(ln1_g=ln1_g, w_in=w_in, s5_lam_re=s5_lam_re, s5_lam_im=s5_lam_im, s5_log_dt=s5_log_dt,
             s5_b_re=s5_b_re, s5_b_im=s5_b_im, s5_c_re=s5_c_re, s5_c_im=s5_c_im, s5_d=s5_d,
             s5_glu_w=s5_glu_w, s5_glu_b=s5_glu_b, hy_conv_w=hy_conv_w, hy_conv_b=hy_conv_b,
             hy_f_w1=hy_f_w1, hy_f_b1=hy_f_b1, hy_f_freq=hy_f_freq, hy_f_w2=hy_f_w2, hy_f_b2=hy_f_b2,
             hy_f_w3=hy_f_w3, hy_log_rate=hy_log_rate, hy_d=hy_d, rw_mu=rw_mu, rw_w0=rw_w0, rw_w2=rw_w2,
             rw_a0=rw_a0, rw_a2=rw_a2, rw_g2=rw_g2, rw_k_k=rw_k_k, rw_k_a=rw_k_a, rw_r_k=rw_r_k,
             rw_ln_w=rw_ln_w, rw_ln_b=rw_ln_b, na_q_g=na_q_g, na_k_g=na_k_g, na_rel_bias=na_rel_bias,
             grp_g=grp_g, w_out=w_out, ln2_g=ln2_g, w_mlp1=w_mlp1, w_mlp2=w_mlp2)
    nb = x_prompt.shape[0]
    y = _trunk(jnp.concatenate([x_prompt, x_sample], axis=0), p)
    return (y[:nb], y[nb:])
```
